```python
import math
import jax
import jax.numpy as jnp
from jax import lax
import numpy as np

D_MODEL = 2048
BATCH = 2
SEQ = 4096
DEPTH = 1

CTX_LEN = 256
GRID_W = 64
N_DIR = 2
S5_WIDTH = D_MODEL // 2
S5_GROUP = 16
S5_GROUPS = S5_WIDTH // S5_GROUP
S5_STATE = 64
S5_DT_MIN = 1e-3
S5_DT_MAX = 1e-1
RW_WIDTH = D_MODEL - S5_WIDTH
RW_HEAD = 64
RW_HEADS = RW_WIDTH // RW_HEAD
RW_DECAY_LORA = 64
RW_AAA_LORA = 64
RW_GATE_LORA = 160
RW_GN_EPS = 64e-5
RW_COLS = 3 * RW_WIDTH + N_DIR * RW_DECAY_LORA + N_DIR * RW_AAA_LORA + RW_GATE_LORA
D_MIX = S5_WIDTH + RW_WIDTH
IN_COLS = S5_WIDTH + RW_COLS
N_EXPERTS = 32
TOP_K = 4
D_EXPERT = D_MODEL
SWIGLU_ALPHA = 1.702
SWIGLU_LIMIT = 7.0
MOE_BLOCK = 128
NORM_EPS = 1e-5

kernel_name = 'hybrid_s5_rwkv7_moe_prefix_block'


def rmsnorm(x, g):
    xf = x.astype(jnp.float32)
    y = xf * lax.rsqrt(jnp.mean(xf * xf, axis=-1, keepdims=True) + NORM_EPS)
    return (y * g.astype(jnp.float32)).astype(x.dtype)


def adaln_params(cond, w, b):
    m = jnp.matmul(jax.nn.silu(cond), w) + b
    return jnp.split(m[..., None, :], 6, axis=-1)


def _complex(re, im):
    return lax.complex(re.astype(jnp.float32), im.astype(jnp.float32))


def diag_linear_scan(lam_bar, bu, h0, reverse):
    first = -1 if reverse else 0
    bu = bu.at[:, first].add(lam_bar * h0)
    a = jnp.broadcast_to(lam_bar, bu.shape)

    def combine(left, right):
        a1, b1 = left
        a2, b2 = right
        return a1 * a2, a2 * b1 + b2

    _, h = lax.associative_scan(combine, (a, bu), reverse=reverse, axis=1)
    return h


def s5_mixer(u_ctx, u_lat, a_re, a_im, log_dt, b_re, b_im, c_re, c_im, d_skip, glu_w, glu_b,
             ctx_out):
    def groups(u):
        return u.astype(jnp.float32).reshape(u.shape[0], u.shape[1], S5_GROUPS, S5_GROUP)

    uc, ul = groups(u_ctx), groups(u_lat)
    ucc, ulc = uc.astype(jnp.complex64), ul.astype(jnp.complex64)
    y_ctx = jnp.zeros_like(uc)
    y_lat = jnp.zeros_like(ul)
    h_zero = jnp.zeros((uc.shape[0], S5_GROUPS, S5_STATE), jnp.complex64)
    for d in range(N_DIR):
        rev = d == 1
        lam = _complex(a_re[d], a_im[d])
        dt = jnp.exp(log_dt[d].astype(jnp.float32))[:, None]
        lam_bar = jnp.exp(lam * dt)
        b_bar = ((lam_bar - 1.0) / lam)[..., None] * _complex(b_re[d], b_im[d])
        c_mat = _complex(c_re[d], c_im[d])
        h_ctx = diag_linear_scan(lam_bar, jnp.einsum('blgh,gph->blgp', ucc, b_bar), h_zero, rev)
        h0 = h_ctx[:, 0] if rev else h_ctx[:, -1]
        h_lat = diag_linear_scan(lam_bar, jnp.einsum('blgh,gph->blgp', ulc, b_bar), h0, rev)
        y_lat = y_lat + jnp.real(jnp.einsum('blgp,ghp->blgh', h_lat, c_mat))
        if ctx_out:
            y_ctx = y_ctx + jnp.real(jnp.einsum('blgp,ghp->blgh', h_ctx, c_mat))

    d_vec = d_skip.astype(jnp.float32).reshape(S5_GROUPS, S5_GROUP)
    w_g = glu_w.astype(jnp.float32)
    b_g = glu_b.astype(jnp.float32).reshape(S5_GROUPS, S5_GROUP)

    def glu(y, u):
        y = jax.nn.gelu(y + d_vec * u)
        gate = jnp.einsum('blgh,ghk->blgk', y, w_g) + b_g
        out = y * jax.nn.sigmoid(gate)
        return out.reshape(out.shape[0], out.shape[1], S5_WIDTH)

    out_ctx = glu(y_ctx, uc) if ctx_out else None
    return out_ctx, glu(y_lat, ul)


def heads(t):
    return t.reshape(t.shape[:-1] + (RW_HEADS, RW_HEAD))


def neighbour_diff(g, axis, offset):
    n = g.shape[axis]
    idx = jnp.arange(n) + offset
    valid = ((idx >= 0) & (idx < n)).reshape([n if i == axis else 1 for i in range(g.ndim)])
    return jnp.where(valid, jnp.roll(g, -offset, axis=axis) - g, jnp.zeros((), g.dtype))


def centred_shift(z, mu, rows, width):
    bsz, length, cz = z.shape
    g = z.reshape(bsz, rows, width, cz)
    out = g
    for j, (axis, off) in enumerate(((2, -1), (2, 1), (1, -1), (1, 1))):
        out = out + mu[j] * neighbour_diff(g, axis, off)
    return out.reshape(bsz, length, cz)


def rwkv_features(z, rows, width, mu, w0, w2, a0, a2, g2, k_k, k_a):
    bsz, length, _ = z.shape
    z = centred_shift(z, mu, rows, width).astype(jnp.float32)
    i1, i2, i3 = RW_WIDTH, 2 * RW_WIDTH, 3 * RW_WIDTH
    i4 = i3 + N_DIR * RW_DECAY_LORA
    i5 = i4 + N_DIR * RW_AAA_LORA
    r, k, v = z[..., :i1], z[..., i1:i2], z[..., i2:i3]
    xw = z[..., i3:i4].reshape(bsz, length, N_DIR, RW_DECAY_LORA)
    xa = z[..., i4:i5].reshape(bsz, length, N_DIR, RW_AAA_LORA)
    xg = z[..., i5:]
    dec = w0.astype(jnp.float32) + jnp.einsum('blnr,nrc->blnc', jnp.tanh(xw), w2.astype(jnp.float32))
    w = jnp.exp(-math.exp(-0.5) * jax.nn.sigmoid(dec))
    a = jax.nn.sigmoid(a0.astype(jnp.float32)
                       + jnp.einsum('blnr,nrc->blnc', xa, a2.astype(jnp.float32)))
    g = jnp.matmul(jax.nn.sigmoid(xg), g2.astype(jnp.float32))
    kk = heads(k * k_k.astype(jnp.float32))
    kh = kk / jnp.maximum(jnp.linalg.norm(kk, axis=-1, keepdims=True), 1e-12)
    kt = k[:, :, None, :] * (1.0 + (a - 1.0) * k_a.astype(jnp.float32))
    return r, v, g, w, a, kh, kt


def rwkv_scan(w, kh, a, kt, v, r, s0, reverse):
    xs = tuple(jnp.moveaxis(t, 1, 0) for t in (w, kh, a * kh, kt, v, r))

    def step(s, inp):
        w_t, kh_t, akh_t, kt_t, v_t, r_t = inp
        s_k = jnp.einsum('bhvk,bhk->bhv', s, kh_t)
        s = (s * w_t[:, :, None, :] - s_k[..., None] * akh_t[:, :, None, :]
             + v_t[..., None] * kt_t[:, :, None, :])
        return s, jnp.einsum('bhvk,bhk->bhv', s, r_t)

    s_fin, y = lax.scan(step, s0, xs, reverse=reverse)
    return s_fin, jnp.moveaxis(y, 0, 1)


def rwkv_readout(y, feats, r_k, ln_w, ln_b):
    r, v, g, _, _, _, kt = feats
    mean = jnp.mean(y, axis=-1, keepdims=True)
    var = jnp.mean(jnp.square(y - mean), axis=-1, keepdims=True)
    yn = ((y - mean) * lax.rsqrt(var + RW_GN_EPS) * heads(ln_w.astype(jnp.float32))
          + heads(ln_b.astype(jnp.float32)))
    bonus = jnp.sum(heads(r)[:, :, None] * heads(kt) * r_k.astype(jnp.float32),
                    axis=(2, 4))[..., None] * heads(v)
    out = yn + bonus
    return out.reshape(out.shape[0], out.shape[1], RW_WIDTH) * g


def rwkv_mixer(z_ctx, z_lat, lat_rows, mu, w0, w2, a0, a2, g2, k_k, k_a, r_k, ln_w, ln_b, ctx_out):
    fc = rwkv_features(z_ctx, 1, z_ctx.shape[1], mu, w0, w2, a0, a2, g2, k_k, k_a)
    fl = rwkv_features(z_lat, lat_rows, GRID_W, mu, w0, w2, a0, a2, g2, k_k, k_a)
    s0 = jnp.zeros((z_lat.shape[0], RW_HEADS, RW_HEAD, RW_HEAD), jnp.float32)

    def dir_inputs(f, d):
        r, v, _, w, a, kh, kt = f
        return heads(w[:, :, d]), kh, heads(a[:, :, d]), heads(kt[:, :, d]), heads(v), heads(r)

    y_ctx = 0.0
    y_lat = 0.0
    for d in range(N_DIR):
        rev = d == 1
        s_ctx, yc = rwkv_scan(*dir_inputs(fc, d), s0, rev)
        _, yl = rwkv_scan(*dir_inputs(fl, d), s_ctx, rev)
        y_ctx = y_ctx + yc
        y_lat = y_lat + yl
    out_ctx = rwkv_readout(y_ctx, fc, r_k, ln_w, ln_b) if ctx_out else None
    return out_ctx, rwkv_readout(y_lat, fl, r_k, ln_w, ln_b)


def moe_ffn(h, router_w, router_b, w_gu, b_gu, w_dn, b_dn):
    n_tok, d_model = h.shape
    logits = (jnp.matmul(h, router_w) + router_b).astype(jnp.float32)
    top_val, top_idx = lax.top_k(logits, TOP_K)
    gates = jax.nn.softmax(top_val, axis=-1)
    n_assign = n_tok * TOP_K
    flat_e = top_idx.reshape(-1)
    flat_tok = jnp.repeat(jnp.arange(n_tok, dtype=jnp.int32), TOP_K)
    flat_g = gates.reshape(-1)
    order = jnp.argsort(flat_e)
    sorted_e = flat_e[order]
    counts = jnp.bincount(flat_e, length=N_EXPERTS)
    start = jnp.cumsum(counts) - counts
    padded = (counts + MOE_BLOCK - 1) // MOE_BLOCK * MOE_BLOCK
    padded_end = jnp.cumsum(padded)
    padded_start = padded_end - padded
    dest = padded_start[sorted_e] + jnp.arange(n_assign) - start[sorted_e]
    n_blocks = -(-(n_assign + N_EXPERTS * (MOE_BLOCK - 1)) // MOE_BLOCK)
    n_slots = n_blocks * MOE_BLOCK
    slot_tok = jnp.zeros((n_slots,), jnp.int32).at[dest].set(flat_tok[order])
    slot_gate = jnp.zeros((n_slots,), jnp.float32).at[dest].set(flat_g[order])
    block_e = jnp.minimum(
        jnp.searchsorted(padded_end, jnp.arange(n_blocks) * MOE_BLOCK, side='right'),
        N_EXPERTS - 1)

    def expert_block(args):
        tok, e = args
        xb = h[tok]
        gu = jnp.matmul(xb, w_gu[e]) + b_gu[e]
        glu, lin = jnp.split(gu, 2, axis=-1)
        glu = jnp.minimum(glu, SWIGLU_LIMIT)
        lin = jnp.clip(lin, -SWIGLU_LIMIT, SWIGLU_LIMIT)
        act = (lin + 1.0) * glu * jax.nn.sigmoid(SWIGLU_ALPHA * glu)
        return jnp.matmul(act, w_dn[e]) + b_dn[e]

    y = lax.map(expert_block, (slot_tok.reshape(n_blocks, MOE_BLOCK), block_e))
    y = y.reshape(n_slots, d_model) * slot_gate[:, None].astype(y.dtype)
    return jax.ops.segment_sum(y, slot_tok, num_segments=n_tok)


def setup_inputs(seed: int = 0) -> dict:
    key = jax.random.key(seed)
    keys = iter(jax.random.split(key, 48))

    def nrm(shape, scale):
        return scale * jax.random.normal(next(keys), shape, jnp.float32)

    def unif(shape, lo, hi):
        return jax.random.uniform(next(keys), shape, jnp.float32, lo, hi)

    nl, g, p, h = DEPTH, S5_GROUPS, S5_STATE, S5_GROUP
    return {
        'x': nrm((BATCH, SEQ, D_MODEL), 1.0),
        'c': nrm((BATCH, D_MODEL), 1.0),
        'ctx': nrm((BATCH, CTX_LEN, D_MODEL), 1.0),
        'c_ctx': nrm((D_MODEL,), 1.0),
        'mod_w': nrm((nl, D_MODEL, 6 * D_MODEL), D_MODEL ** -0.5),
        'mod_b': nrm((nl, 6 * D_MODEL), 0.02),
        'norm1_g': 1.0 + nrm((nl, D_MODEL), 0.02),
        'w_in': nrm((nl, D_MODEL, IN_COLS), D_MODEL ** -0.5),
        's5_a_re': -0.5 + nrm((nl, N_DIR, g, p), 0.01),
        's5_a_im': jnp.pi * jnp.arange(p, dtype=jnp.float32) + nrm((nl, N_DIR, g, p), 0.01),
        's5_log_dt': unif((nl, N_DIR, g), math.log(S5_DT_MIN), math.log(S5_DT_MAX)),
        's5_b_re': nrm((nl, N_DIR, g, p, h), (2 * h) ** -0.5),
        's5_b_im': nrm((nl, N_DIR, g, p, h), (2 * h) ** -0.5),
        's5_c_re': nrm((nl, N_DIR, g, h, p), p ** -0.5),
        's5_c_im': nrm((nl, N_DIR, g, h, p), p ** -0.5),
        's5_d': nrm((nl, S5_WIDTH), 1.0),
        's5_glu_w': nrm((nl, g, h, h), h ** -0.5),
        's5_glu_b': nrm((nl, S5_WIDTH), 0.02),
        'rw_mu': unif((nl, 4, RW_COLS), 0.0, 0.25),
        'rw_w0': unif((nl, N_DIR, RW_WIDTH), -6.0, 1.0),
        'rw_w2': nrm((nl, N_DIR, RW_DECAY_LORA, RW_WIDTH), 0.1 * RW_DECAY_LORA ** -0.5),
        'rw_a0': nrm((nl, N_DIR, RW_WIDTH), 0.5),
        'rw_a2': nrm((nl, N_DIR, RW_AAA_LORA, RW_WIDTH), 0.1 * RW_AAA_LORA ** -0.5),
        'rw_g2': nrm((nl, RW_GATE_LORA, RW_WIDTH), RW_GATE_LORA ** -0.5),
        'rw_k_k': 0.85 + nrm((nl, RW_WIDTH), 0.05),
        'rw_k_a': 1.0 + nrm((nl, RW_WIDTH), 0.05),
        'rw_r_k': nrm((nl, RW_HEADS, RW_HEAD), 0.1),
        'rw_ln_w': 1.0 + nrm((nl, RW_WIDTH), 0.02),
        'rw_ln_b': nrm((nl, RW_WIDTH), 0.02),
        'w_out': nrm((nl, D_MIX, D_MODEL), D_MIX ** -0.5),
        'norm2_g': 1.0 + nrm((nl, D_MODEL), 0.02),
        'router_w': nrm((nl, D_MODEL, N_EXPERTS), D_MODEL ** -0.5),
        'router_b': nrm((nl, N_EXPERTS), 0.01),
        'exp_w_gu': nrm((nl, N_EXPERTS, D_MODEL, 2 * D_EXPERT), D_MODEL ** -0.5),
        'exp_b_gu': nrm((nl, N_EXPERTS, 2 * D_EXPERT), 0.01),
        'exp_w_dn': nrm((nl, N_EXPERTS, D_EXPERT, D_MODEL), D_EXPERT ** -0.5),
        'exp_b_dn': nrm((nl, N_EXPERTS, D_MODEL), 0.01),
        'final_g': 1.0 + nrm((D_MODEL,), 0.02),
    }


def reference(x, c, ctx, c_ctx, mod_w, mod_b, norm1_g, w_in, s5_a_re, s5_a_im, s5_log_dt,
              s5_b_re, s5_b_im, s5_c_re, s5_c_im, s5_d, s5_glu_w, s5_glu_b, rw_mu, rw_w0, rw_w2,
              rw_a0, rw_a2, rw_g2, rw_k_k, rw_k_a, rw_r_k, rw_ln_w, rw_ln_b, w_out, norm2_g,
              router_w, router_b, exp_w_gu, exp_b_gu, exp_w_dn, exp_b_dn, final_g):
    bsz, n_lat, d_model = x.shape
    n_ctx = ctx.shape[1]
    rows = n_lat // GRID_W
    for i in range(DEPTH):
        last = i == DEPTH - 1
        sh1, sc1, gt1, sh2, sc2, gt2 = adaln_params(c, mod_w[i], mod_b[i])
        csh1, csc1, cgt1, csh2, csc2, cgt2 = adaln_params(c_ctx, mod_w[i], mod_b[i])

        p_lat = jnp.matmul(rmsnorm(x, norm1_g[i]) * (1.0 + sc1) + sh1, w_in[i])
        p_ctx = jnp.matmul(rmsnorm(ctx, norm1_g[i]) * (1.0 + csc1) + csh1, w_in[i])
        y5_ctx, y5_lat = s5_mixer(p_ctx[..., :S5_WIDTH], p_lat[..., :S5_WIDTH],
                                  s5_a_re[i], s5_a_im[i], s5_log_dt[i], s5_b_re[i], s5_b_im[i],
                                  s5_c_re[i], s5_c_im[i], s5_d[i], s5_glu_w[i], s5_glu_b[i],
                                  not last)
        yr_ctx, yr_lat = rwkv_mixer(p_ctx[..., S5_WIDTH:], p_lat[..., S5_WIDTH:], rows,
                                    rw_mu[i], rw_w0[i], rw_w2[i], rw_a0[i], rw_a2[i], rw_g2[i],
                                    rw_k_k[i], rw_k_a[i], rw_r_k[i], rw_ln_w[i], rw_ln_b[i],
                                    not last)
        mix_lat = jnp.matmul(jnp.concatenate([y5_lat, yr_lat], axis=-1).astype(x.dtype), w_out[i])
        x = x + gt1 * mix_lat
        if not last:
            mix_ctx = jnp.matmul(jnp.concatenate([y5_ctx, yr_ctx], axis=-1).astype(ctx.dtype),
                                 w_out[i])
            ctx = ctx + cgt1 * mix_ctx

        f_lat = (rmsnorm(x, norm2_g[i]) * (1.0 + sc2) + sh2).reshape(bsz * n_lat, d_model)
        moe_args = (router_w[i], router_b[i], exp_w_gu[i], exp_b_gu[i], exp_w_dn[i], exp_b_dn[i])
        if last:
            x = x + gt2 * moe_ffn(f_lat, *moe_args).reshape(bsz, n_lat, d_model)
        else:
            f_ctx = (rmsnorm(ctx, norm2_g[i]) * (1.0 + csc2) + csh2).reshape(bsz * n_ctx, d_model)
            f = moe_ffn(jnp.concatenate([f_lat, f_ctx], axis=0), *moe_args)
            x = x + gt2 * f[:bsz * n_lat].reshape(bsz, n_lat, d_model)
            ctx = ctx + cgt2 * f[bsz * n_lat:].reshape(bsz, n_ctx, d_model)
    return rmsnorm(x, final_g)
```

```python
import functools
import math

import jax
import jax.numpy as jnp
from jax import lax
from jax.experimental import pallas as pl
from jax.experimental.pallas import tpu as pltpu

F32 = jnp.float32
BF16 = jnp.bfloat16
HIGHEST = lax.Precision.HIGHEST

LANES = 128
VMEM_LIMIT_BYTES = 56 * 1024 * 1024

NORM_EPS = 1e-5
N_DIR = 2
S5_GROUP = 16
S5_STATE = 64
S5_CHUNK = 16
S5_GROUP_BLOCK = 8
RW_HEAD = 64
RW_CHUNK = 64
RW_GN_EPS = 64e-5
GRID_W = 64
TOP_K = 4
SWIGLU_ALPHA = 1.702
SWIGLU_LIMIT = 7.0
ROW_TILE = 256
MOE_ROWS = 256


def _params(*sem):
    return pltpu.CompilerParams(dimension_semantics=sem, vmem_limit_bytes=VMEM_LIMIT_BYTES)


def _mm(a, b, dims=((1,), (0,))):
    return lax.dot_general(a.astype(BF16), b.astype(BF16), (dims, ((), ())),
                           preferred_element_type=F32)


def _mm_f32(a, b, dims=((1,), (0,))):
    return lax.dot_general(a, b, (dims, ((), ())), precision=HIGHEST,
                           preferred_element_type=F32)


def _adaln_kernel(cond_ref, w_ref, b_ref, o_ref):
    c = cond_ref[...]
    o_ref[...] = _mm_f32(c * jax.nn.sigmoid(c), w_ref[...]) + b_ref[...]


def _adaln(cond8, w, b):
    d, n = w.shape
    tn = 1536
    return pl.pallas_call(
        _adaln_kernel,
        grid=(n // tn,),
        in_specs=[pl.BlockSpec((8, d), lambda j: (0, 0)),
                  pl.BlockSpec((d, tn), lambda j: (0, j)),
                  pl.BlockSpec((1, tn), lambda j: (0, j))],
        out_specs=pl.BlockSpec((8, tn), lambda j: (0, j)),
        out_shape=jax.ShapeDtypeStruct((8, n), F32),
        compiler_params=_params("parallel"),
        name="adaln",
    )(cond8, w, b.reshape(1, n))


def _rms_mod(x, g, sc, sh):
    y = x * lax.rsqrt(jnp.mean(x * x, axis=-1, keepdims=True) + NORM_EPS)
    return (y * g) * (1.0 + sc) + sh


def _inproj_kernel(x_ref, g_ref, sc_ref, sh_ref, wu_ref, wz_ref, u_ref, z_ref):
    h = _rms_mod(x_ref[0], g_ref[...], sc_ref[0], sh_ref[0]).astype(BF16)
    u_ref[0] = jnp.dot(h, wu_ref[...], preferred_element_type=F32)
    z_ref[0] = jnp.dot(h, wz_ref[...], preferred_element_type=F32)


def _inproj(x, g, sc, sh, wu, wz):
    bsz, n, d = x.shape
    nu, nz = wu.shape[1], wz.shape[1]
    tm = ROW_TILE
    const = dict(pipeline_mode=pl.Buffered(1))
    return pl.pallas_call(
        _inproj_kernel,
        grid=(bsz, n // tm),
        in_specs=[pl.BlockSpec((1, tm, d), lambda b, i: (b, i, 0)),
                  pl.BlockSpec((1, d), lambda b, i: (0, 0)),
                  pl.BlockSpec((1, 1, d), lambda b, i: (b, 0, 0)),
                  pl.BlockSpec((1, 1, d), lambda b, i: (b, 0, 0)),
                  pl.BlockSpec((d, nu), lambda b, i: (0, 0), **const),
                  pl.BlockSpec((d, nz), lambda b, i: (0, 0), **const)],
        out_specs=[pl.BlockSpec((1, tm, nu), lambda b, i: (b, i, 0)),
                   pl.BlockSpec((1, tm, nz), lambda b, i: (b, i, 0))],
        out_shape=[jax.ShapeDtypeStruct((bsz, n, nu), F32),
                   jax.ShapeDtypeStruct((bsz, n, nz), F32)],
        compiler_params=_params("parallel", "parallel"),
        name="inproj",
    )(x, g.reshape(1, d), sc, sh, wu, wz)


def _s5_constants(a_re, a_im, log_dt, b_re, b_im, c_re, c_im, d_skip, glu_w, glu_b):
    t = S5_CHUNK
    g, p = a_re.shape[1], a_re.shape[2]
    h = S5_GROUP
    n = jnp.arange(t + 1, dtype=F32)[:, None, None]
    fm, em, lam_re, lam_im = [], [], [], []
    kl = []
    for d in range(N_DIR):
        ar, ai = a_re[d].astype(F32), a_im[d].astype(F32)
        dt = jnp.exp(log_dt[d].astype(F32))[:, None]
        mag = jnp.exp(n * (ar * dt))
        pr, pi = mag * jnp.cos(n * (ai * dt)), mag * jnp.sin(n * (ai * dt))
        lr, li = pr[1], pi[1]
        den = ar * ar + ai * ai
        cf_re = ((lr - 1.0) * ar + li * ai) / den
        cf_im = (li * ar - (lr - 1.0) * ai) / den
        br, bi = b_re[d].astype(F32), b_im[d].astype(F32)
        bb_re = cf_re[..., None] * br - cf_im[..., None] * bi
        bb_im = cf_re[..., None] * bi + cf_im[..., None] * br
        cr, ci = c_re[d].astype(F32), c_im[d].astype(F32)
        cl_re = cr[None] * pr[:, :, None, :] - ci[None] * pi[:, :, None, :]
        cl_im = cr[None] * pi[:, :, None, :] + ci[None] * pr[:, :, None, :]
        lag = (jnp.einsum('nghp,gpi->nghi', cl_re[:t], bb_re, precision=HIGHEST)
               - jnp.einsum('nghp,gpi->nghi', cl_im[:t], bb_im, precision=HIGHEST))
        kl.append(lag)
        pw = jnp.arange(t - 1, -1, -1) if d == 0 else jnp.arange(t)
        ps_re, ps_im = pr[pw], pi[pw]
        f_re = ps_re[..., None] * bb_re[None] - ps_im[..., None] * bb_im[None]
        f_im = ps_re[..., None] * bb_im[None] + ps_im[..., None] * bb_re[None]
        fm.append((jnp.transpose(f_re, (1, 0, 3, 2)).reshape(g, t * h, p),
                   jnp.transpose(f_im, (1, 0, 3, 2)).reshape(g, t * h, p)))
        pe = jnp.arange(1, t + 1) if d == 0 else jnp.arange(t, 0, -1)
        e_re = jnp.transpose(cl_re[pe], (1, 3, 0, 2)).reshape(g, p, t * h)
        e_im = -jnp.transpose(cl_im[pe], (1, 3, 0, 2)).reshape(g, p, t * h)
        em.append((e_re, e_im))
        lam_re.append(pr[t])
        lam_im.append(pi[t])
    fmat = jnp.concatenate([fm[0][0], fm[1][0], fm[0][1], fm[1][1]], axis=-1)
    emat = jnp.concatenate([em[0][0], em[1][0], em[0][1], em[1][1]], axis=1)
    s_idx = jnp.arange(t)[:, None]
    t_idx = jnp.arange(t)[None, :]
    kf = jnp.where((t_idx >= s_idx)[..., None, None, None],
                   kl[0][jnp.clip(t_idx - s_idx, 0, t - 1)], 0.0)
    kb = jnp.where((s_idx >= t_idx)[..., None, None, None],
                   kl[1][jnp.clip(s_idx - t_idx, 0, t - 1)], 0.0)
    kmat = jnp.transpose(kf + kb, (2, 0, 4, 1, 3)).reshape(g, t * h, t * h)
    lam = jnp.stack([jnp.concatenate([lam_re[0], lam_re[1]], -1),
                     jnp.concatenate([lam_im[0], lam_im[1]], -1)], axis=1)
    dvec = jnp.tile(d_skip.astype(F32).reshape(g, 1, h), (1, t, 1)).reshape(g, 1, t * h)
    bvec = jnp.tile(glu_b.astype(F32).reshape(g, 1, h), (1, t, 1)).reshape(g, 1, t * h)
    eye_t = jnp.eye(t, dtype=F32)
    gmat = jnp.einsum('st,ghk->gshtk', eye_t, glu_w.astype(F32)).reshape(g, t * h, t * h)
    return (fmat.astype(BF16), emat.astype(BF16), kmat.astype(BF16), lam, dvec, bvec,
            gmat.astype(BF16))


def _s5_in_kernel(u_ref, f_ref, z_ref):
    for gi in range(u_ref.shape[0]):
        z_ref[gi] = _mm(u_ref[gi], f_ref[gi])


def _s5_scan_kernel(n_ctx_chunks, z_ref, lre_ref, lim_ref, hs_ref):
    n_chunks, rows, width = z_ref.shape
    half = width // 2
    l_re, l_im = lre_ref[...], lim_ref[...]
    fwd_lane = lax.broadcasted_iota(jnp.int32, (rows, half), 1) < (half // 2)

    def step(i, carry):
        h_re, h_im = carry
        cb = jnp.where(i < n_ctx_chunks, n_ctx_chunks - 1 - i, n_chunks - 1 + n_ctx_chunks - i)
        zf = z_ref[i]
        zb = z_ref[cb]
        hs_ref[i, :, 0:half // 2] = h_re[:, 0:half // 2]
        hs_ref[i, :, half:half + half // 2] = h_im[:, 0:half // 2]
        hs_ref[cb, :, half // 2:half] = h_re[:, half // 2:]
        hs_ref[cb, :, half + half // 2:] = h_im[:, half // 2:]
        z_re = jnp.where(fwd_lane, zf[:, :half], zb[:, :half])
        z_im = jnp.where(fwd_lane, zf[:, half:], zb[:, half:])
        return (l_re * h_re - l_im * h_im + z_re, l_re * h_im + l_im * h_re + z_im)

    zero = jnp.zeros((rows, half), F32)
    lax.fori_loop(0, n_chunks, step, (zero, zero))


def _s5_out_kernel(ctx_rows, u_ref, hs_ref, e_ref, k_ref, d_ref, b_ref, g_ref, o_ref):
    for gi in range(u_ref.shape[0]):
        u = u_ref[gi, ctx_rows:, :]
        y = _mm(u, k_ref[gi]) + _mm(hs_ref[gi, ctx_rows:, :], e_ref[gi])
        y = jax.nn.gelu(y + d_ref[gi] * u)
        gate = _mm(y, g_ref[gi]) + b_ref[gi]
        o_ref[gi] = y * jax.nn.sigmoid(gate)


def _s5(u_grp, consts, n_ctx_chunks, bsz):
    fmat, emat, kmat, lam, dvec, bvec, gmat = consts
    g, rows, width = u_grp.shape
    gb = S5_GROUP_BLOCK
    n_chunks = rows // bsz
    grp = pl.BlockSpec((gb, rows, width), lambda i: (i, 0, 0))
    mat = pl.BlockSpec((gb, width, width), lambda i: (i, 0, 0))
    vec = pl.BlockSpec((gb, 1, width), lambda i: (i, 0, 0))
    z = pl.pallas_call(
        _s5_in_kernel,
        grid=(g // gb,),
        in_specs=[grp, mat],
        out_specs=grp,
        out_shape=jax.ShapeDtypeStruct((g, rows, width), F32),
        compiler_params=_params("parallel"),
        name="s5_in",
    )(u_grp, fmat)
    z_cm = jnp.transpose(z.reshape(g, n_chunks, bsz, width), (1, 0, 2, 3)).reshape(
        n_chunks, g * bsz, width)
    lam_rows = jnp.repeat(lam, bsz, axis=0)
    rb = 32
    cm = pl.BlockSpec((n_chunks, rb, width), lambda i: (0, i, 0))
    lrow = pl.BlockSpec((rb, width // 2), lambda i: (i, 0))
    hs_cm = pl.pallas_call(
        functools.partial(_s5_scan_kernel, n_ctx_chunks),
        grid=(g * bsz // rb,),
        in_specs=[cm, lrow, lrow],
        out_specs=cm,
        out_shape=jax.ShapeDtypeStruct((n_chunks, g * bsz, width), F32),
        compiler_params=_params("parallel"),
        name="s5_scan",
    )(z_cm, lam_rows[:, 0], lam_rows[:, 1])
    hs = jnp.transpose(hs_cm.reshape(n_chunks, g, bsz, width), (1, 0, 2, 3)).reshape(
        g, rows, width)
    ctx_rows = n_ctx_chunks * bsz
    return pl.pallas_call(
        functools.partial(_s5_out_kernel, ctx_rows),
        grid=(g // gb,),
        in_specs=[grp, grp, mat, mat, vec, vec, mat],
        out_specs=pl.BlockSpec((gb, rows - ctx_rows, width), lambda i: (i, 0, 0)),
        out_shape=jax.ShapeDtypeStruct((g, rows - ctx_rows, width), F32),
        compiler_params=_params("parallel"),
        name="s5_out",
    )(u_grp, hs, emat, kmat, dvec, bvec, gmat)


def _rw_feat_kernel(width, has_vert, n_tiles, z_ref, zp_ref, zn_ref, mu_ref, w0_ref, w2_ref,
                    a0_ref, a2_ref, g2_ref, kk_ref, ka_ref, rk_ref, hsel_ref, hselt_ref,
                    r_ref, v_ref, kh_ref, g_ref, bv_ref, lw_ref, q_ref, kt_ref):
    z = z_ref[0]
    t0, cz = z.shape
    rw = r_ref.shape[-1]
    mu = mu_ref[...]
    tok = lax.broadcasted_iota(jnp.int32, (t0, 1), 0)
    col = tok % width
    left = pltpu.roll(z, 1, 0)
    right = pltpu.roll(z, t0 - 1, 0)
    out = z + jnp.where(col != 0, mu[0:1] * (left - z), 0.0)
    out = out + jnp.where(col != width - 1, mu[1:2] * (right - z), 0.0)
    if has_vert:
        i = pl.program_id(1)
        up = jnp.concatenate([zp_ref[0], z[:t0 - width]], axis=0)
        down = jnp.concatenate([z[width:], zn_ref[0]], axis=0)
        up_ok = jnp.logical_or(i > 0, tok >= width)
        down_ok = jnp.logical_or(i < n_tiles - 1, tok < t0 - width)
        out = out + jnp.where(up_ok, mu[2:3] * (up - z), 0.0)
        out = out + jnp.where(down_ok, mu[3:4] * (down - z), 0.0)
    r = out[:, 0:rw]
    k = out[:, rw:2 * rw]
    v = out[:, 2 * rw:3 * rw]
    o = 3 * rw
    lora = w2_ref.shape[0]
    xw = out[:, o:o + lora]
    xa = out[:, o + lora:o + 2 * lora]
    xg = out[:, o + 2 * lora:]
    dec = w0_ref[...] + _mm_f32(jnp.tanh(xw), w2_ref[...])
    lw = -math.exp(-0.5) * jax.nn.sigmoid(dec)
    a = jax.nn.sigmoid(a0_ref[...] + _mm_f32(xa, a2_ref[...]))
    g = _mm_f32(jax.nn.sigmoid(xg), g2_ref[...])
    kk = k * kk_ref[...]
    ssq = _mm_f32(kk * kk, hsel_ref[...])
    inv = 1.0 / jnp.maximum(jnp.sqrt(ssq), 1e-12)
    kh = kk * _mm_f32(inv, hselt_ref[...])
    ka = ka_ref[...]
    kt_sum = jnp.zeros_like(k)
    for d in range(N_DIR):
        a_d = a[:, d * rw:(d + 1) * rw]
        kt_d = k * (1.0 + (a_d - 1.0) * ka)
        kt_sum = kt_sum + kt_d
        lw_ref[d, 0] = lw[:, d * rw:(d + 1) * rw]
        q_ref[d, 0] = a_d * kh
        kt_ref[d, 0] = kt_d
    bonus = _mm_f32(_mm_f32(r * kt_sum * rk_ref[...], hsel_ref[...]), hselt_ref[...])
    r_ref[0] = r
    v_ref[0] = v
    kh_ref[0] = kh
    g_ref[0] = g
    bv_ref[0] = bonus * v


def _rw_features(z, width, has_vert, mu, w0, w2blk, a0, a2blk, g2, k_k, k_a, r_k, hsel, hselt):
    bsz, n, cz = z.shape
    rw = k_k.shape[-1]
    t0 = ROW_TILE
    n_tiles = n // t0
    per = t0 // GRID_W
    nblk = n // GRID_W
    full = lambda a: pl.BlockSpec(a.shape, lambda b, i: (0,) * a.ndim)
    tok = pl.BlockSpec((1, t0, rw), lambda b, i: (b, i, 0))
    dtok = pl.BlockSpec((N_DIR, 1, t0, rw), lambda b, i: (0, b, i, 0))
    consts = (mu, w0, w2blk, a0, a2blk, g2, k_k, k_a, r_k, hsel, hselt)
    return pl.pallas_call(
        functools.partial(_rw_feat_kernel, width, has_vert, n_tiles),
        grid=(bsz, n_tiles),
        in_specs=[pl.BlockSpec((1, t0, cz), lambda b, i: (b, i, 0)),
                  pl.BlockSpec((1, GRID_W, cz), lambda b, i: (b, jnp.maximum(i * per - 1, 0), 0)),
                  pl.BlockSpec((1, GRID_W, cz),
                               lambda b, i: (b, jnp.minimum(i * per + per, nblk - 1), 0))]
        + [full(a) for a in consts],
        out_specs=[tok] * 5 + [dtok] * 3,
        out_shape=[jax.ShapeDtypeStruct((bsz, n, rw), F32)] * 5
        + [jax.ShapeDtypeStruct((N_DIR, bsz, n, rw), F32)] * 3,
        compiler_params=_params("parallel", "parallel"),
        name="rw_features",
    )(z, z, z, *consts)


def _rw_chunk_kernel(r_ref, v_ref, kh_ref, lw_ref, q_ref, kt_ref, g_ref, h_ref, rh_ref, y0_ref):
    rev = pl.program_id(0) == 1
    n = RW_CHUNK
    hd = RW_HEAD
    lw = lw_ref[0, 0]
    row = lax.broadcasted_iota(jnp.int32, (n, n), 0)
    col = lax.broadcasted_iota(jnp.int32, (n, n), 1)
    ahead = (row - col) * jnp.where(rev, -1, 1)
    incl = ahead >= 0
    strict = ahead > 0
    eye = (row == col).astype(F32)
    same_block = [jnp.right_shift(row, s) == jnp.right_shift(col, s)
                  for s in range(3, n.bit_length())]
    b_incl = _mm_f32(incl.astype(F32), lw)
    btot = jnp.sum(lw, axis=0, keepdims=True)
    e_neg = jnp.exp(-b_incl)
    e_rem = jnp.exp(btot - b_incl)
    q, kt = q_ref[0, 0], kt_ref[0, 0]
    pt_all = kh_ref[0] * jnp.exp(b_incl - lw)
    rt_all = r_ref[0] * jnp.exp(b_incl)
    qt_all = q * e_neg
    ktt_all = kt * e_neg
    qh_all = q * e_rem
    kth_all = kt * e_rem
    gam_all = jnp.exp(btot)
    v_all = v_ref[0]
    for h in range(lw.shape[-1] // hd):
        sl = slice(h * hd, (h + 1) * hd)
        pt, rt, v = pt_all[:, sl], rt_all[:, sl], v_all[:, sl]
        a4 = _mm(jnp.concatenate([pt, rt], 0),
                 jnp.concatenate([qt_all[:, sl], ktt_all[:, sl]], 0), ((1,), (1,)))
        nmat = jnp.where(strict, a4[:n, :n], 0.0)
        akp = jnp.where(strict, a4[:n, n:], 0.0)
        lrq = jnp.where(incl, a4[n:, :n], 0.0)
        lrk = jnp.where(incl, a4[n:, n:], 0.0)
        nd = jnp.where(same_block[0], nmat, 0.0)
        m = eye - nd
        x = _mm(nd, nd)
        m = m + _mm(m, x)
        x = _mm(x, x)
        m = m + _mm(m, x)
        for lvl in range(1, len(same_block)):
            off = jnp.where(same_block[lvl], jnp.where(same_block[lvl - 1], 0.0, nmat), 0.0)
            m = m - _mm(_mm(m, off), m)
        wu = _mm(m, jnp.concatenate([pt, _mm(akp, v)], 1))
        gh = _mm(wu, qh_all[:, sl], ((0,), (0,)))
        vk = _mm(v, kth_all[:, sl], ((0,), (0,)))
        g_ref[0, 0, 0, :, sl] = eye * gam_all[:, sl] - gh[:hd]
        h_ref[0, 0, 0, :, sl] = vk - gh[hd:]
        lwu = _mm(lrq, wu)
        rh_ref[0, 0, :, sl] = rt - lwu[:, :hd]
        y0_ref[0, 0, :, sl] = _mm(lrk, v) - lwu[:, hd:]


def _rw_chunks(r, v, kh, lw, q, kt):
    bsz, n, rw = r.shape
    nc = n // RW_CHUNK
    tok = pl.BlockSpec((1, RW_CHUNK, rw), lambda d, b, c: (b, c, 0))
    dtok = pl.BlockSpec((1, 1, RW_CHUNK, rw), lambda d, b, c: (d, b, c, 0))
    mat = pl.BlockSpec((1, 1, 1, RW_HEAD, rw), lambda d, b, c: (d, b, c, 0, 0))
    return pl.pallas_call(
        _rw_chunk_kernel,
        grid=(N_DIR, bsz, nc),
        in_specs=[tok, tok, tok, dtok, dtok, dtok],
        out_specs=[mat, mat, dtok, dtok],
        out_shape=[jax.ShapeDtypeStruct((N_DIR, bsz, nc, RW_HEAD, rw), F32)] * 2
        + [jax.ShapeDtypeStruct((N_DIR, bsz, n, rw), F32)] * 2,
        compiler_params=_params("parallel", "parallel", "parallel"),
        name="rw_chunks",
    )(r, v, kh, lw, q, kt)


def _rw_state_kernel(emit_y, s0_ref, g_ref, h_ref, rh_ref, y0_ref, *rest):
    if emit_y:
        y_ref, sfin_ref, s_scr = rest
    else:
        sfin_ref, s_scr = rest
    c = pl.program_id(2)
    hd = RW_HEAD

    @pl.when(c == 0)
    def _():
        s_scr[...] = s0_ref[0, 0]

    for h in range(s_scr.shape[-1] // hd):
        sl = slice(h * hd, (h + 1) * hd)
        s = s_scr[:, sl]
        if emit_y:
            y_ref[0, 0, :, sl] = _mm(rh_ref[0, 0, :, sl], s, ((1,), (1,))) + y0_ref[0, 0, :, sl]
        s_scr[:, sl] = _mm(s, g_ref[0, 0, 0, :, sl]) + h_ref[0, 0, 0, :, sl]

    @pl.when(c == pl.num_programs(2) - 1)
    def _():
        sfin_ref[0, 0] = s_scr[...]


def _rw_state(s0, gmat, hmat, rh, y0, emit_y):
    n_dir, bsz, nc, hd, rw = gmat.shape
    n = rh.shape[2]
    order = lambda d, c: c + d * (nc - 1 - 2 * c)
    mat = pl.BlockSpec((1, 1, 1, hd, rw), lambda d, b, c: (d, b, order(d, c), 0, 0))
    dtok = pl.BlockSpec((1, 1, RW_CHUNK, rw), lambda d, b, c: (d, b, order(d, c), 0))
    st = pl.BlockSpec((1, 1, hd, rw), lambda d, b, c: (d, b, 0, 0))
    out_specs = [st]
    out_shape = [jax.ShapeDtypeStruct((n_dir, bsz, hd, rw), F32)]
    if emit_y:
        out_specs = [dtok] + out_specs
        out_shape = [jax.ShapeDtypeStruct((n_dir, bsz, n, rw), F32)] + out_shape
    return pl.pallas_call(
        functools.partial(_rw_state_kernel, emit_y),
        grid=(n_dir, bsz, nc),
        in_specs=[st, mat, mat, dtok, dtok],
        out_specs=out_specs,
        out_shape=out_shape,
        scratch_shapes=[pltpu.VMEM((hd, rw), F32)],
        compiler_params=_params("parallel", "parallel", "arbitrary"),
        name="rw_state_y" if emit_y else "rw_state",
    )(s0, gmat, hmat, rh, y0)


def _outproj_kernel(x_ref, y5_ref, yf_ref, yb_ref, bv_ref, g_ref, lnw_ref, lnb_ref, hsel_ref,
                    hselt_ref, wo_ref, gt1_ref, g2_ref, sc2_ref, sh2_ref, rw_ref, rb_ref,
                    x1_ref, f_ref, idx_ref, gate_ref, rank_ref, cnt_ref, carry):
    first = jnp.logical_and(pl.program_id(0) == 0, pl.program_id(1) == 0)

    @pl.when(first)
    def _():
        carry[...] = jnp.zeros_like(carry)

    inv_hd = 1.0 / RW_HEAD
    y = yf_ref[0, 0] + yb_ref[0, 0]
    mean = _mm_f32(_mm_f32(y, hsel_ref[...]), hselt_ref[...]) * inv_hd
    yc = y - mean
    var = _mm_f32(_mm_f32(yc * yc, hsel_ref[...]), hselt_ref[...]) * inv_hd
    yn = yc * lax.rsqrt(var + RW_GN_EPS) * lnw_ref[...] + lnb_ref[...]
    yr = (yn + bv_ref[0]) * g_ref[0]
    mix = _mm(jnp.concatenate([y5_ref[0], yr], axis=-1), wo_ref[...])
    x1 = x_ref[0] + gt1_ref[0] * mix
    x1_ref[0] = x1
    f = _rms_mod(x1, g2_ref[...], sc2_ref[0], sh2_ref[0])
    f_ref[0] = f.astype(BF16)

    logits = _mm_f32(f, rw_ref[...]) + rb_ref[...]
    tm, ne = logits.shape
    eid = lax.broadcasted_iota(jnp.int32, (tm, ne), 1)
    work = logits
    sel = jnp.zeros((tm, ne), F32)
    idx_cols, val_cols = [], []
    for _ in range(TOP_K):
        top = jnp.max(work, axis=-1, keepdims=True)
        pick = jnp.min(jnp.where(work == top, eid, ne), axis=-1, keepdims=True)
        hit = eid == pick
        sel = jnp.where(hit, 1.0, sel)
        work = jnp.where(hit, -jnp.inf, work)
        idx_cols.append(pick)
        val_cols.append(top)
    exps = [jnp.exp(vk - val_cols[0]) for vk in val_cols]
    denom = exps[0] + exps[1] + exps[2] + exps[3]
    row = lax.broadcasted_iota(jnp.int32, (tm, tm), 0)
    colm = lax.broadcasted_iota(jnp.int32, (tm, tm), 1)
    before = _mm((colm < row).astype(F32), sel) + carry[0:1, :]
    rank_cols = [jnp.sum(jnp.where(eid == ic, before, 0.0), axis=-1, keepdims=True)
                 for ic in idx_cols]
    idx_ref[0] = jnp.concatenate(idx_cols, axis=-1)
    gate_ref[0] = jnp.concatenate([e / denom for e in exps], axis=-1)
    rank_ref[0] = jnp.concatenate(rank_cols, axis=-1).astype(jnp.int32)
    total = carry[0:1, :] + jnp.sum(sel, axis=0, keepdims=True)
    carry[...] = jnp.broadcast_to(total, carry.shape)
    cnt_ref[...] = jnp.broadcast_to(total, cnt_ref.shape).astype(jnp.int32)


def _outproj(x, y5, ydir, bv, g, ln_w, ln_b, hsel, hselt, wo, gt1, g2, sc2, sh2, router_w,
             router_b):
    bsz, n, d = x.shape
    rw = y5.shape[-1]
    ne = router_w.shape[-1]
    tm = ROW_TILE
    row = lambda w: pl.BlockSpec((1, tm, w), lambda b, i: (b, i, 0))
    vec = lambda w: pl.BlockSpec((1, w), lambda b, i: (0, 0))
    bvec = pl.BlockSpec((1, 1, d), lambda b, i: (b, 0, 0))
    full = lambda a: pl.BlockSpec(a.shape, lambda b, i: (0,) * a.ndim)
    return pl.pallas_call(
        _outproj_kernel,
        grid=(bsz, n // tm),
        in_specs=[row(d), row(rw),
                  pl.BlockSpec((1, 1, tm, rw), lambda b, i: (0, b, i, 0)),
                  pl.BlockSpec((1, 1, tm, rw), lambda b, i: (1, b, i, 0)),
                  row(rw), row(rw), vec(rw), vec(rw), full(hsel), full(hselt),
                  pl.BlockSpec(wo.shape, lambda b, i: (0, 0), pipeline_mode=pl.Buffered(1)),
                  bvec, vec(d), bvec, bvec, full(router_w), vec(ne)],
        out_specs=[row(d), row(d), row(TOP_K), row(TOP_K), row(TOP_K),
                   pl.BlockSpec((8, ne), lambda b, i: (0, 0))],
        out_shape=[jax.ShapeDtypeStruct((bsz, n, d), F32),
                   jax.ShapeDtypeStruct((bsz, n, d), BF16),
                   jax.ShapeDtypeStruct((bsz, n, TOP_K), jnp.int32),
                   jax.ShapeDtypeStruct((bsz, n, TOP_K), F32),
                   jax.ShapeDtypeStruct((bsz, n, TOP_K), jnp.int32),
                   jax.ShapeDtypeStruct((8, ne), jnp.int32)],
        scratch_shapes=[pltpu.VMEM((8, ne), F32)],
        compiler_params=_params("arbitrary", "arbitrary"),
        name="outproj_router",
    )(x, y5, ydir, ydir, bv, g, ln_w.reshape(1, rw), ln_b.reshape(1, rw), hsel, hselt, wo, gt1,
      g2.reshape(1, d), sc2, sh2, router_w, router_b.reshape(1, ne))


def _dispatch_kernel(idx_ref, rank_ref, start_ref, f_ref, xs_in_ref, xs_ref, sem):
    del xs_in_ref
    tm = f_ref.shape[0]

    def issue(t, _):
        for k in range(TOP_K):
            a = t * TOP_K + k
            dst = start_ref[idx_ref[a]] + rank_ref[a]
            pltpu.make_async_copy(f_ref.at[t], xs_ref.at[dst], sem).start()
        return 0

    lax.fori_loop(0, tm, issue, 0)
    for _ in range(TOP_K):
        pltpu.make_async_copy(f_ref, xs_ref.at[pl.ds(0, tm)], sem).wait()


def _dispatch(idx_flat, rank_flat, start, f3, n_slots):
    n_tok, s, lanes = f3.shape
    tm = ROW_TILE
    smem_tok = pl.BlockSpec((tm * TOP_K,), lambda i: (i,), memory_space=pltpu.SMEM)
    xs0 = jnp.zeros((n_slots, s, lanes), f3.dtype)
    return pl.pallas_call(
        _dispatch_kernel,
        grid=(n_tok // tm,),
        in_specs=[smem_tok, smem_tok,
                  pl.BlockSpec(start.shape, lambda i: (0,), memory_space=pltpu.SMEM),
                  pl.BlockSpec((tm, s, lanes), lambda i: (i, 0, 0)),
                  pl.BlockSpec(memory_space=pl.ANY)],
        out_specs=pl.BlockSpec(memory_space=pl.ANY),
        out_shape=jax.ShapeDtypeStruct((n_slots, s, lanes), f3.dtype),
        scratch_shapes=[pltpu.SemaphoreType.DMA(())],
        input_output_aliases={4: 0},
        compiler_params=_params("arbitrary"),
        name="moe_dispatch",
    )(idx_flat, rank_flat, start, f3, xs0)


def _expert_changed(be_ref, b):
    prev = be_ref[jnp.maximum(b - 1, 0)]
    return jnp.logical_or(b == 0, be_ref[b] != prev)


def _moe_gu_kernel(be_ref, nv_ref, x_ref, wg_ref, wl_ref, bg_ref, bl_ref, o_ref, wg_bf, wl_bf):
    b = pl.program_id(1)

    @pl.when(b < nv_ref[0])
    def _():
        @pl.when(_expert_changed(be_ref, b))
        def _():
            wg_bf[...] = wg_ref[0].astype(BF16)
            wl_bf[...] = wl_ref[0].astype(BF16)

        x = x_ref[...]
        glu = jnp.dot(x, wg_bf[...], preferred_element_type=F32) + bg_ref[0]
        lin = jnp.dot(x, wl_bf[...], preferred_element_type=F32) + bl_ref[0]
        glu = jnp.minimum(glu, SWIGLU_LIMIT)
        lin = jnp.clip(lin, -SWIGLU_LIMIT, SWIGLU_LIMIT)
        o_ref[...] = ((lin + 1.0) * glu * jax.nn.sigmoid(SWIGLU_ALPHA * glu)).astype(BF16)

    @pl.when(b >= nv_ref[0])
    def _():
        o_ref[...] = jnp.zeros_like(o_ref)


def _moe_gu(block_e, n_valid, xs, w_gu, b_gu):
    n_slots, d = xs.shape
    ne, _, two_de = w_gu.shape
    de = two_de // 2
    tn = 512
    nj = de // tn
    nb = n_slots // MOE_ROWS
    blk = lambda b, nv: jnp.minimum(b, nv[0] - 1)
    grid_spec = pltpu.PrefetchScalarGridSpec(
        num_scalar_prefetch=2,
        grid=(nj, nb),
        in_specs=[pl.BlockSpec((MOE_ROWS, d), lambda j, b, be, nv: (blk(b, nv), 0)),
                  pl.BlockSpec((1, d, tn), lambda j, b, be, nv: (be[blk(b, nv)], 0, j)),
                  pl.BlockSpec((1, d, tn), lambda j, b, be, nv: (be[blk(b, nv)], 0, nj + j)),
                  pl.BlockSpec((1, 1, tn), lambda j, b, be, nv: (be[blk(b, nv)], 0, j)),
                  pl.BlockSpec((1, 1, tn), lambda j, b, be, nv: (be[blk(b, nv)], 0, nj + j))],
        out_specs=pl.BlockSpec((MOE_ROWS, tn), lambda j, b, be, nv: (b, j)),
        scratch_shapes=[pltpu.VMEM((d, tn), BF16)] * 2)
    return pl.pallas_call(
        _moe_gu_kernel,
        grid_spec=grid_spec,
        out_shape=jax.ShapeDtypeStruct((n_slots, de), BF16),
        compiler_params=_params("arbitrary", "arbitrary"),
        name="moe_gate_up",
    )(block_e, n_valid, xs, w_gu, w_gu, b_gu.reshape(ne, 1, two_de), b_gu.reshape(ne, 1, two_de))


def _moe_dn_kernel(be_ref, nv_ref, a_ref, w_ref, bd_ref, o_ref, w_bf):
    b = pl.program_id(1)

    @pl.when(b < nv_ref[0])
    def _():
        @pl.when(_expert_changed(be_ref, b))
        def _():
            w_bf[...] = w_ref[0].astype(BF16)

        y = jnp.dot(a_ref[...], w_bf[...], preferred_element_type=F32) + bd_ref[0]
        for s in range(o_ref.shape[1]):
            o_ref[:, s, :] = y[:, s * LANES:(s + 1) * LANES]

    @pl.when(b >= nv_ref[0])
    def _():
        o_ref[...] = jnp.zeros_like(o_ref)


def _moe_dn(block_e, n_valid, act, w_dn, b_dn):
    n_slots, de = act.shape
    ne, _, d = w_dn.shape
    tn = 1024
    nb = n_slots // MOE_ROWS
    blk = lambda b, nv: jnp.minimum(b, nv[0] - 1)
    grid_spec = pltpu.PrefetchScalarGridSpec(
        num_scalar_prefetch=2,
        grid=(d // tn, nb),
        in_specs=[pl.BlockSpec((MOE_ROWS, de), lambda j, b, be, nv: (blk(b, nv), 0)),
                  pl.BlockSpec((1, de, tn), lambda j, b, be, nv: (be[blk(b, nv)], 0, j)),
                  pl.BlockSpec((1, 1, tn), lambda j, b, be, nv: (be[blk(b, nv)], 0, j))],
        out_specs=pl.BlockSpec((MOE_ROWS, tn // LANES, LANES),
                               lambda j, b, be, nv: (b, j, 0)),
        scratch_shapes=[pltpu.VMEM((de, tn), BF16)])
    return pl.pallas_call(
        _moe_dn_kernel,
        grid_spec=grid_spec,
        out_shape=jax.ShapeDtypeStruct((n_slots, d // LANES, LANES), F32),
        compiler_params=_params("arbitrary", "arbitrary"),
        name="moe_down",
    )(block_e, n_valid, act, w_dn, b_dn.reshape(ne, 1, d))


def _combine_kernel(idx_ref, rank_ref, start_ref, x1_ref, gate_ref, gt2_ref, fg_ref, y_ref,
                    o_ref, buf, sem):
    tm = x1_ref.shape[0]

    def issue(t, _):
        for k in range(TOP_K):
            a = t * TOP_K + k
            src = start_ref[idx_ref[a]] + rank_ref[a]
            pltpu.make_async_copy(y_ref.at[src], buf.at[k, t], sem).start()
        return 0

    lax.fori_loop(0, tm, issue, 0)
    for k in range(TOP_K):
        pltpu.make_async_copy(y_ref.at[pl.ds(0, tm)], buf.at[k], sem).wait()
    moe = gate_ref[0] * buf[0]
    for k in range(1, TOP_K):
        moe = moe + gate_ref[k] * buf[k]
    x2 = x1_ref[...] + gt2_ref[0] * moe
    ms = jnp.sum(jnp.sum(x2 * x2, axis=2, keepdims=True), axis=1, keepdims=True) * (
        1.0 / (x2.shape[1] * x2.shape[2]))
    o_ref[...] = x2 * lax.rsqrt(ms + NORM_EPS) * fg_ref[...]


def _combine(idx_flat, rank_flat, start, x1_3, gate4, gt2_3, fg_3, y3, tiles_per_batch):
    n_tok, s, lanes = x1_3.shape
    tm = ROW_TILE
    smem_tok = pl.BlockSpec((tm * TOP_K,), lambda i: (i,), memory_space=pltpu.SMEM)
    return pl.pallas_call(
        _combine_kernel,
        grid=(n_tok // tm,),
        in_specs=[smem_tok, smem_tok,
                  pl.BlockSpec(start.shape, lambda i: (0,), memory_space=pltpu.SMEM),
                  pl.BlockSpec((tm, s, lanes), lambda i: (i, 0, 0)),
                  pl.BlockSpec((TOP_K, tm, 1, 1), lambda i: (0, i, 0, 0)),
                  pl.BlockSpec((1, s, lanes), lambda i: (i // tiles_per_batch, 0, 0)),
                  pl.BlockSpec((s, lanes), lambda i: (0, 0)),
                  pl.BlockSpec(memory_space=pl.ANY)],
        out_specs=pl.BlockSpec((tm, s, lanes), lambda i: (i, 0, 0)),
        out_shape=jax.ShapeDtypeStruct((n_tok, s, lanes), F32),
        scratch_shapes=[pltpu.VMEM((TOP_K, tm, s, lanes), F32), pltpu.SemaphoreType.DMA(())],
        compiler_params=_params("arbitrary"),
        name="moe_combine_final",
    )(idx_flat, rank_flat, start, x1_3, gate4, gt2_3, fg_3, y3)


def _block_diag2(w):
    z = jnp.zeros_like(w[0])
    return jnp.concatenate([jnp.concatenate([w[0], z], 1), jnp.concatenate([z, w[1]], 1)], 0)


def kernel(x, c, ctx, c_ctx, mod_w, mod_b, norm1_g, w_in, s5_a_re, s5_a_im, s5_log_dt, s5_b_re,
           s5_b_im, s5_c_re, s5_c_im, s5_d, s5_glu_w, s5_glu_b, rw_mu, rw_w0, rw_w2, rw_a0, rw_a2,
           rw_g2, rw_k_k, rw_k_a, rw_r_k, rw_ln_w, rw_ln_b, w_out, norm2_g, router_w, router_b,
           exp_w_gu, exp_b_gu, exp_w_dn, exp_b_dn, final_g):
    assert mod_w.shape[0] == 1, "single-layer stack only"
    bsz, n_lat, d = x.shape
    n_ctx = ctx.shape[1]
    assert bsz == 2 and n_ctx % ROW_TILE == 0 and n_lat % ROW_TILE == 0
    s5w = s5_d.shape[-1]
    rww = rw_k_k.shape[-1]
    n_heads = rww // RW_HEAD
    ne = router_w.shape[-1]

    cond8 = jnp.zeros((8, d), F32).at[:bsz].set(c).at[bsz].set(c_ctx)
    mod = _adaln(cond8, mod_w[0], mod_b[0])
    sh1, sc1, gt1, sh2, sc2, gt2 = [m[:bsz, None, :] for m in jnp.split(mod, 6, axis=-1)]
    csh1, csc1 = [jnp.broadcast_to(m[bsz][None, None, :], (bsz, 1, d))
                  for m in jnp.split(mod, 6, axis=-1)[:2]]

    wu = w_in[0][:, :s5w].astype(BF16)
    wz = w_in[0][:, s5w:].astype(BF16)
    u_lat, z_lat = _inproj(x, norm1_g[0], sc1, sh1, wu, wz)
    u_ctx, z_ctx = _inproj(ctx, norm1_g[0], csc1, csh1, wu, wz)

    t5 = S5_CHUNK
    n_all = n_ctx + n_lat
    groups = s5w // S5_GROUP
    u_all = jnp.concatenate([u_ctx, u_lat], axis=1)
    u_grp = jnp.transpose(u_all.reshape(bsz, n_all // t5, t5, groups, S5_GROUP),
                          (3, 1, 0, 2, 4)).reshape(groups, (n_all // t5) * bsz, t5 * S5_GROUP)
    s5c = _s5_constants(s5_a_re[0], s5_a_im[0], s5_log_dt[0], s5_b_re[0], s5_b_im[0], s5_c_re[0],
                        s5_c_im[0], s5_d[0], s5_glu_w[0], s5_glu_b[0])
    y5_grp = _s5(u_grp, s5c, n_ctx // t5, bsz)
    y5 = jnp.transpose(y5_grp.reshape(groups, n_lat // t5, bsz, t5, S5_GROUP),
                       (2, 1, 3, 0, 4)).reshape(bsz, n_lat, s5w)

    lanes_idx = jnp.arange(rww) // RW_HEAD
    hsel = (lanes_idx[:, None] == jnp.arange(LANES)[None, :]).astype(F32)
    hselt = hsel.T
    feat_consts = (rw_mu[0], rw_w0[0].reshape(1, N_DIR * rww), _block_diag2(rw_w2[0]),
                   rw_a0[0].reshape(1, N_DIR * rww), _block_diag2(rw_a2[0]), rw_g2[0],
                   rw_k_k[0].reshape(1, rww), rw_k_a[0].reshape(1, rww),
                   rw_r_k[0].reshape(1, rww), hsel, hselt)
    fc = _rw_features(z_ctx, n_ctx, False, *feat_consts)
    fl = _rw_features(z_lat, GRID_W, True, *feat_consts)

    def scan_inputs(f):
        r, v, kh, _, _, lw, q, kt = f
        return r, v, kh, lw, q, kt

    gc, hc, rhc, y0c = _rw_chunks(*scan_inputs(fc))
    gl, hl, rhl, y0l = _rw_chunks(*scan_inputs(fl))
    s_zero = jnp.zeros((N_DIR, bsz, RW_HEAD, rww), F32)
    (s_ctx,) = _rw_state(s_zero, gc, hc, rhc, y0c, emit_y=False)
    y_dir, _ = _rw_state(s_ctx, gl, hl, rhl, y0l, emit_y=True)

    x1, f, idx4, gate4, rank4, counts = _outproj(
        x, y5, y_dir, fl[4], fl[3], rw_ln_w[0], rw_ln_b[0], hsel, hselt, w_out[0].astype(BF16),
        gt1, norm2_g[0], sc2, sh2, router_w[0], router_b[0])

    n_tok = bsz * n_lat
    cnt = counts[0]
    padded = (cnt + MOE_ROWS - 1) // MOE_ROWS * MOE_ROWS
    pend = jnp.cumsum(padded)
    start = (pend - padded).astype(jnp.int32)
    nb = n_tok * TOP_K // MOE_ROWS + ne
    n_slots = nb * MOE_ROWS
    block_e = jnp.minimum(
        jnp.searchsorted(pend, jnp.arange(nb) * MOE_ROWS, side='right'), ne - 1).astype(jnp.int32)
    n_valid = (pend[-1] // MOE_ROWS).astype(jnp.int32).reshape(1)
    idx_flat = idx4.reshape(-1)
    rank_flat = rank4.reshape(-1)

    s = d // LANES
    xs3 = _dispatch(idx_flat, rank_flat, start, f.reshape(n_tok, s, LANES), n_slots)
    act = _moe_gu(block_e, n_valid, xs3.reshape(n_slots, d), exp_w_gu[0], exp_b_gu[0])
    y3 = _moe_dn(block_e, n_valid, act, exp_w_dn[0], exp_b_dn[0])
    gate_t = jnp.transpose(gate4.reshape(n_tok, TOP_K), (1, 0)).reshape(TOP_K, n_tok, 1, 1)
    out3 = _combine(idx_flat, rank_flat, start, x1.reshape(n_tok, s, LANES), gate_t,
                    gt2.reshape(bsz, s, LANES), final_g.reshape(s, LANES), y3, n_lat // ROW_TILE)
    return out3.reshape(bsz, n_lat, d)
```

```python
import functools
import math

import jax
import jax.numpy as jnp
from jax import lax
from jax.experimental import pallas as pl
from jax.experimental.pallas import tpu as pltpu

F32 = jnp.float32
BF16 = jnp.bfloat16
HIGHEST = lax.Precision.HIGHEST

LANES = 128
VMEM_LIMIT_BYTES = 56 * 1024 * 1024

NORM_EPS = 1e-5
N_DIR = 2
S5_GROUP = 16
S5_STATE = 64
S5_CHUNK = 16
S5_GROUP_BLOCK = 8
RW_HEAD = 64
RW_CHUNK = 64
RW_GN_EPS = 64e-5
GRID_W = 64
TOP_K = 4
SWIGLU_ALPHA = 1.702
SWIGLU_LIMIT = 7.0
ROW_TILE = 256
MOE_ROWS = 256


def _params(*sem):
    return pltpu.CompilerParams(dimension_semantics=sem, vmem_limit_bytes=VMEM_LIMIT_BYTES)


def _mm(a, b, dims=((1,), (0,))):
    return lax.dot_general(a.astype(BF16), b.astype(BF16), (dims, ((), ())),
                           preferred_element_type=F32)


def _mm_f32(a, b, dims=((1,), (0,))):
    return lax.dot_general(a, b, (dims, ((), ())), precision=HIGHEST,
                           preferred_element_type=F32)


def _adaln_kernel(cond_ref, w_ref, b_ref, o_ref):
    c = cond_ref[...]
    o_ref[...] = _mm_f32(c * jax.nn.sigmoid(c), w_ref[...]) + b_ref[...]


def _adaln(cond8, w, b):
    d, n = w.shape
    tn = 1536
    return pl.pallas_call(
        _adaln_kernel,
        grid=(n // tn,),
        in_specs=[pl.BlockSpec((8, d), lambda j: (0, 0)),
                  pl.BlockSpec((d, tn), lambda j: (0, j)),
                  pl.BlockSpec((1, tn), lambda j: (0, j))],
        out_specs=pl.BlockSpec((8, tn), lambda j: (0, j)),
        out_shape=jax.ShapeDtypeStruct((8, n), F32),
        compiler_params=_params("parallel"),
        name="adaln",
    )(cond8, w, b.reshape(1, n))


def _rms_mod(x, g, sc, sh):
    y = x * lax.rsqrt(jnp.mean(x * x, axis=-1, keepdims=True) + NORM_EPS)
    return (y * g) * (1.0 + sc) + sh


def _inproj_kernel(x_ref, g_ref, sc_ref, sh_ref, wu_ref, wz_ref, u_ref, z_ref):
    h = _rms_mod(x_ref[0], g_ref[...], sc_ref[0], sh_ref[0]).astype(BF16)
    u_ref[0] = jnp.dot(h, wu_ref[...], preferred_element_type=F32)
    z_ref[0] = jnp.dot(h, wz_ref[...], preferred_element_type=F32)


def _inproj(x, g, sc, sh, wu, wz):
    bsz, n, d = x.shape
    nu, nz = wu.shape[1], wz.shape[1]
    tm = ROW_TILE
    const = dict(pipeline_mode=pl.Buffered(1))
    return pl.pallas_call(
        _inproj_kernel,
        grid=(bsz, n // tm),
        in_specs=[pl.BlockSpec((1, tm, d), lambda b, i: (b, i, 0)),
                  pl.BlockSpec((1, d), lambda b, i: (0, 0)),
                  pl.BlockSpec((1, 1, d), lambda b, i: (b, 0, 0)),
                  pl.BlockSpec((1, 1, d), lambda b, i: (b, 0, 0)),
                  pl.BlockSpec((d, nu), lambda b, i: (0, 0), **const),
                  pl.BlockSpec((d, nz), lambda b, i: (0, 0), **const)],
        out_specs=[pl.BlockSpec((1, tm, nu), lambda b, i: (b, i, 0)),
                   pl.BlockSpec((1, tm, nz), lambda b, i: (b, i, 0))],
        out_shape=[jax.ShapeDtypeStruct((bsz, n, nu), F32),
                   jax.ShapeDtypeStruct((bsz, n, nz), F32)],
        compiler_params=_params("parallel", "parallel"),
        name="inproj",
    )(x, g.reshape(1, d), sc, sh, wu, wz)


def _s5_constants(a_re, a_im, log_dt, b_re, b_im, c_re, c_im, d_skip, glu_w, glu_b):
    t = S5_CHUNK
    g, p = a_re.shape[1], a_re.shape[2]
    h = S5_GROUP
    n = jnp.arange(t + 1, dtype=F32)[:, None, None]
    fm, em, lam_re, lam_im = [], [], [], []
    kl = []
    for d in range(N_DIR):
        ar, ai = a_re[d].astype(F32), a_im[d].astype(F32)
        dt = jnp.exp(log_dt[d].astype(F32))[:, None]
        mag = jnp.exp(n * (ar * dt))
        pr, pi = mag * jnp.cos(n * (ai * dt)), mag * jnp.sin(n * (ai * dt))
        lr, li = pr[1], pi[1]
        den = ar * ar + ai * ai
        cf_re = ((lr - 1.0) * ar + li * ai) / den
        cf_im = (li * ar - (lr - 1.0) * ai) / den
        br, bi = b_re[d].astype(F32), b_im[d].astype(F32)
        bb_re = cf_re[..., None] * br - cf_im[..., None] * bi
        bb_im = cf_re[..., None] * bi + cf_im[..., None] * br
        cr, ci = c_re[d].astype(F32), c_im[d].astype(F32)
        cl_re = cr[None] * pr[:, :, None, :] - ci[None] * pi[:, :, None, :]
        cl_im = cr[None] * pi[:, :, None, :] + ci[None] * pr[:, :, None, :]
        lag = (jnp.einsum('nghp,gpi->nghi', cl_re[:t], bb_re, precision=HIGHEST)
               - jnp.einsum('nghp,gpi->nghi', cl_im[:t], bb_im, precision=HIGHEST))
        kl.append(lag)
        pw = jnp.arange(t - 1, -1, -1) if d == 0 else jnp.arange(t)
        ps_re, ps_im = pr[pw], pi[pw]
        f_re = ps_re[..., None] * bb_re[None] - ps_im[..., None] * bb_im[None]
        f_im = ps_re[..., None] * bb_im[None] + ps_im[..., None] * bb_re[None]
        fm.append((jnp.transpose(f_re, (1, 0, 3, 2)).reshape(g, t * h, p),
                   jnp.transpose(f_im, (1, 0, 3, 2)).reshape(g, t * h, p)))
        pe = jnp.arange(1, t + 1) if d == 0 else jnp.arange(t, 0, -1)
        e_re = jnp.transpose(cl_re[pe], (1, 3, 0, 2)).reshape(g, p, t * h)
        e_im = -jnp.transpose(cl_im[pe], (1, 3, 0, 2)).reshape(g, p, t * h)
        em.append((e_re, e_im))
        lam_re.append(pr[t])
        lam_im.append(pi[t])
    fmat = jnp.concatenate([fm[0][0], fm[1][0], fm[0][1], fm[1][1]], axis=-1)
    emat = jnp.concatenate([em[0][0], em[1][0], em[0][1], em[1][1]], axis=1)
    s_idx = jnp.arange(t)[:, None]
    t_idx = jnp.arange(t)[None, :]
    kf = jnp.where((t_idx >= s_idx)[..., None, None, None],
                   kl[0][jnp.clip(t_idx - s_idx, 0, t - 1)], 0.0)
    kb = jnp.where((s_idx >= t_idx)[..., None, None, None],
                   kl[1][jnp.clip(s_idx - t_idx, 0, t - 1)], 0.0)
    kmat = jnp.transpose(kf + kb, (2, 0, 4, 1, 3)).reshape(g, t * h, t * h)
    lam = jnp.stack([jnp.concatenate([lam_re[0], lam_re[1]], -1),
                     jnp.concatenate([lam_im[0], lam_im[1]], -1)], axis=1)
    dvec = jnp.tile(d_skip.astype(F32).reshape(g, 1, h), (1, t, 1)).reshape(g, 1, t * h)
    bvec = jnp.tile(glu_b.astype(F32).reshape(g, 1, h), (1, t, 1)).reshape(g, 1, t * h)
    eye_t = jnp.eye(t, dtype=F32)
    gmat = jnp.einsum('st,ghk->gshtk', eye_t, glu_w.astype(F32)).reshape(g, t * h, t * h)
    return (fmat.astype(BF16), emat.astype(BF16), kmat.astype(BF16), lam, dvec, bvec,
            gmat.astype(BF16))


def _s5_in_kernel(u_ref, f_ref, z_ref):
    for gi in range(u_ref.shape[0]):
        z_ref[gi] = _mm(u_ref[gi], f_ref[gi])


def _s5_scan_kernel(n_ctx_chunks, z_ref, lre_ref, lim_ref, hs_ref):
    n_chunks, rows, width = z_ref.shape
    half = width // 2
    l_re, l_im = lre_ref[...], lim_ref[...]
    fwd_lane = lax.broadcasted_iota(jnp.int32, (rows, half), 1) < (half // 2)

    def step(i, carry):
        h_re, h_im = carry
        cb = jnp.where(i < n_ctx_chunks, n_ctx_chunks - 1 - i, n_chunks - 1 + n_ctx_chunks - i)
        zf = z_ref[i]
        zb = z_ref[cb]
        hs_ref[i, :, 0:half // 2] = h_re[:, 0:half // 2]
        hs_ref[i, :, half:half + half // 2] = h_im[:, 0:half // 2]
        hs_ref[cb, :, half // 2:half] = h_re[:, half // 2:]
        hs_ref[cb, :, half + half // 2:] = h_im[:, half // 2:]
        z_re = jnp.where(fwd_lane, zf[:, :half], zb[:, :half])
        z_im = jnp.where(fwd_lane, zf[:, half:], zb[:, half:])
        return (l_re * h_re - l_im * h_im + z_re, l_re * h_im + l_im * h_re + z_im)

    zero = jnp.zeros((rows, half), F32)
    lax.fori_loop(0, n_chunks, step, (zero, zero))


def _s5_out_kernel(ctx_rows, u_ref, hs_ref, e_ref, k_ref, d_ref, b_ref, g_ref, o_ref):
    for gi in range(u_ref.shape[0]):
        u = u_ref[gi, ctx_rows:, :]
        y = _mm(u, k_ref[gi]) + _mm(hs_ref[gi, ctx_rows:, :], e_ref[gi])
        y = jax.nn.gelu(y + d_ref[gi] * u)
        gate = _mm(y, g_ref[gi]) + b_ref[gi]
        o_ref[gi] = y * jax.nn.sigmoid(gate)


def _s5(u_grp, consts, n_ctx_chunks, bsz):
    fmat, emat, kmat, lam, dvec, bvec, gmat = consts
    g, rows, width = u_grp.shape
    gb = S5_GROUP_BLOCK
    n_chunks = rows // bsz
    grp = pl.BlockSpec((gb, rows, width), lambda i: (i, 0, 0))
    mat = pl.BlockSpec((gb, width, width), lambda i: (i, 0, 0))
    vec = pl.BlockSpec((gb, 1, width), lambda i: (i, 0, 0))
    z = pl.pallas_call(
        _s5_in_kernel,
        grid=(g // gb,),
        in_specs=[grp, mat],
        out_specs=grp,
        out_shape=jax.ShapeDtypeStruct((g, rows, width), F32),
        compiler_params=_params("parallel"),
        name="s5_in",
    )(u_grp, fmat)
    z_cm = jnp.transpose(z.reshape(g, n_chunks, bsz, width), (1, 0, 2, 3)).reshape(
        n_chunks, g * bsz, width)
    lam_rows = jnp.repeat(lam, bsz, axis=0)
    rb = 32
    cm = pl.BlockSpec((n_chunks, rb, width), lambda i: (0, i, 0))
    lrow = pl.BlockSpec((rb, width // 2), lambda i: (i, 0))
    hs_cm = pl.pallas_call(
        functools.partial(_s5_scan_kernel, n_ctx_chunks),
        grid=(g * bsz // rb,),
        in_specs=[cm, lrow, lrow],
        out_specs=cm,
        out_shape=jax.ShapeDtypeStruct((n_chunks, g * bsz, width), F32),
        compiler_params=_params("parallel"),
        name="s5_scan",
    )(z_cm, lam_rows[:, 0], lam_rows[:, 1])
    hs = jnp.transpose(hs_cm.reshape(n_chunks, g, bsz, width), (1, 0, 2, 3)).reshape(
        g, rows, width)
    ctx_rows = n_ctx_chunks * bsz
    return pl.pallas_call(
        functools.partial(_s5_out_kernel, ctx_rows),
        grid=(g // gb,),
        in_specs=[grp, grp, mat, mat, vec, vec, mat],
        out_specs=pl.BlockSpec((gb, rows - ctx_rows, width), lambda i: (i, 0, 0)),
        out_shape=jax.ShapeDtypeStruct((g, rows - ctx_rows, width), F32),
        compiler_params=_params("parallel"),
        name="s5_out",
    )(u_grp, hs, emat, kmat, dvec, bvec, gmat)


def _rw_feat_kernel(width, has_vert, n_tiles, z_ref, zp_ref, zn_ref, mu_ref, w0_ref, w2_ref,
                    a0_ref, a2_ref, g2_ref, kk_ref, ka_ref, rk_ref, hsel_ref, hselt_ref,
                    r_ref, v_ref, kh_ref, g_ref, bv_ref, lw_ref, q_ref, kt_ref):
    z = z_ref[0]
    t0, cz = z.shape
    rw = r_ref.shape[-1]
    mu = mu_ref[...]
    tok = lax.broadcasted_iota(jnp.int32, (t0, 1), 0)
    col = tok % width
    left = pltpu.roll(z, 1, 0)
    right = pltpu.roll(z, t0 - 1, 0)
    out = z + jnp.where(col != 0, mu[0:1] * (left - z), 0.0)
    out = out + jnp.where(col != width - 1, mu[1:2] * (right - z), 0.0)
    if has_vert:
        i = pl.program_id(1)
        up = jnp.concatenate([zp_ref[0], z[:t0 - width]], axis=0)
        down = jnp.concatenate([z[width:], zn_ref[0]], axis=0)
        up_ok = jnp.logical_or(i > 0, tok >= width)
        down_ok = jnp.logical_or(i < n_tiles - 1, tok < t0 - width)
        out = out + jnp.where(up_ok, mu[2:3] * (up - z), 0.0)
        out = out + jnp.where(down_ok, mu[3:4] * (down - z), 0.0)
    r = out[:, 0:rw]
    k = out[:, rw:2 * rw]
    v = out[:, 2 * rw:3 * rw]
    o = 3 * rw
    lora = w2_ref.shape[0]
    xw = out[:, o:o + lora]
    xa = out[:, o + lora:o + 2 * lora]
    xg = out[:, o + 2 * lora:]
    dec = w0_ref[...] + _mm_f32(jnp.tanh(xw), w2_ref[...])
    lw = -math.exp(-0.5) * jax.nn.sigmoid(dec)
    a = jax.nn.sigmoid(a0_ref[...] + _mm_f32(xa, a2_ref[...]))
    g = _mm_f32(jax.nn.sigmoid(xg), g2_ref[...])
    kk = k * kk_ref[...]
    ssq = _mm_f32(kk * kk, hsel_ref[...])
    inv = 1.0 / jnp.maximum(jnp.sqrt(ssq), 1e-12)
    kh = kk * _mm_f32(inv, hselt_ref[...])
    ka = ka_ref[...]
    kt_sum = jnp.zeros_like(k)
    for d in range(N_DIR):
        a_d = a[:, d * rw:(d + 1) * rw]
        kt_d = k * (1.0 + (a_d - 1.0) * ka)
        kt_sum = kt_sum + kt_d
        lw_ref[d, 0] = lw[:, d * rw:(d + 1) * rw]
        q_ref[d, 0] = a_d * kh
        kt_ref[d, 0] = kt_d
    bonus = _mm_f32(_mm_f32(r * kt_sum * rk_ref[...], hsel_ref[...]), hselt_ref[...])
    r_ref[0] = r
    v_ref[0] = v
    kh_ref[0] = kh
    g_ref[0] = g
    bv_ref[0] = bonus * v


def _rw_features(z, width, has_vert, mu, w0, w2blk, a0, a2blk, g2, k_k, k_a, r_k, hsel, hselt):
    bsz, n, cz = z.shape
    rw = k_k.shape[-1]
    t0 = ROW_TILE
    n_tiles = n // t0
    per = t0 // GRID_W
    nblk = n // GRID_W
    full = lambda a: pl.BlockSpec(a.shape, lambda b, i: (0,) * a.ndim)
    tok = pl.BlockSpec((1, t0, rw), lambda b, i: (b, i, 0))
    dtok = pl.BlockSpec((N_DIR, 1, t0, rw), lambda b, i: (0, b, i, 0))
    consts = (mu, w0, w2blk, a0, a2blk, g2, k_k, k_a, r_k, hsel, hselt)
    return pl.pallas_call(
        functools.partial(_rw_feat_kernel, width, has_vert, n_tiles),
        grid=(bsz, n_tiles),
        in_specs=[pl.BlockSpec((1, t0, cz), lambda b, i: (b, i, 0)),
                  pl.BlockSpec((1, GRID_W, cz), lambda b, i: (b, jnp.maximum(i * per - 1, 0), 0)),
                  pl.BlockSpec((1, GRID_W, cz),
                               lambda b, i: (b, jnp.minimum(i * per + per, nblk - 1), 0))]
        + [full(a) for a in consts],
        out_specs=[tok] * 5 + [dtok] * 3,
        out_shape=[jax.ShapeDtypeStruct((bsz, n, rw), F32)] * 5
        + [jax.ShapeDtypeStruct((N_DIR, bsz, n, rw), F32)] * 3,
        compiler_params=_params("parallel", "parallel"),
        name="rw_features",
    )(z, z, z, *consts)


def _rw_chunk_kernel(r_ref, v_ref, kh_ref, lw_ref, q_ref, kt_ref, g_ref, h_ref, rh_ref, y0_ref):
    rev = pl.program_id(0) == 1
    n = RW_CHUNK
    hd = RW_HEAD
    lw = lw_ref[0, 0]
    row = lax.broadcasted_iota(jnp.int32, (n, n), 0)
    col = lax.broadcasted_iota(jnp.int32, (n, n), 1)
    ahead = (row - col) * jnp.where(rev, -1, 1)
    incl = (ahead >= 0).astype(F32)
    strict = (ahead > 0).astype(F32)
    eye = (row == col).astype(F32)
    same_block = [(jnp.right_shift(row, s) == jnp.right_shift(col, s)).astype(F32)
                  for s in range(3, n.bit_length())]
    b_incl = _mm_f32(incl, lw)
    btot = jnp.sum(lw, axis=0, keepdims=True)
    e_neg = jnp.exp(-b_incl)
    e_rem = jnp.exp(btot - b_incl)
    q, kt = q_ref[0, 0], kt_ref[0, 0]
    pt_all = kh_ref[0] * jnp.exp(b_incl - lw)
    rt_all = r_ref[0] * jnp.exp(b_incl)
    qt_all = q * e_neg
    ktt_all = kt * e_neg
    qh_all = q * e_rem
    kth_all = kt * e_rem
    gam_all = jnp.exp(btot)
    v_all = v_ref[0]
    heads = range(lw.shape[-1] // hd)
    sls = [slice(h * hd, (h + 1) * hd) for h in heads]
    pt = [pt_all[:, s] for s in sls]
    rt = [rt_all[:, s] for s in sls]
    v = [v_all[:, s] for s in sls]
    a4 = [_mm(jnp.concatenate([pt[h], rt[h]], 0),
              jnp.concatenate([qt_all[:, sls[h]], ktt_all[:, sls[h]]], 0), ((1,), (1,)))
          for h in heads]
    nmat = [strict * a4[h][:n, :n] for h in heads]
    akv = [_mm(strict * a4[h][:n, n:], v[h]) for h in heads]
    nd = [same_block[0] * nmat[h] for h in heads]
    x = [_mm(nd[h], nd[h]) for h in heads]
    m = [eye - nd[h] for h in heads]
    m = [m[h] + _mm(m[h], x[h]) for h in heads]
    x = [_mm(x[h], x[h]) for h in heads]
    m = [m[h] + _mm(m[h], x[h]) for h in heads]
    for lvl in range(1, len(same_block)):
        ring = same_block[lvl] - same_block[lvl - 1]
        t = [_mm(m[h], ring * nmat[h]) for h in heads]
        m = [m[h] - _mm(t[h], m[h]) for h in heads]
    wu = [_mm(m[h], jnp.concatenate([pt[h], akv[h]], 1)) for h in heads]
    gh = [_mm(wu[h], qh_all[:, sls[h]], ((0,), (0,))) for h in heads]
    vk = [_mm(v[h], kth_all[:, sls[h]], ((0,), (0,))) for h in heads]
    lwu = [_mm(incl * a4[h][n:, :n], wu[h]) for h in heads]
    lv = [_mm(incl * a4[h][n:, n:], v[h]) for h in heads]
    for h in heads:
        g_ref[0, 0, 0, :, sls[h]] = eye * gam_all[:, sls[h]] - gh[h][:hd]
        h_ref[0, 0, 0, :, sls[h]] = vk[h] - gh[h][hd:]
        rh_ref[0, 0, :, sls[h]] = rt[h] - lwu[h][:, :hd]
        y0_ref[0, 0, :, sls[h]] = lv[h] - lwu[h][:, hd:]


def _rw_chunks(r, v, kh, lw, q, kt):
    bsz, n, rw = r.shape
    nc = n // RW_CHUNK
    tok = pl.BlockSpec((1, RW_CHUNK, rw), lambda d, b, c: (b, c, 0))
    dtok = pl.BlockSpec((1, 1, RW_CHUNK, rw), lambda d, b, c: (d, b, c, 0))
    mat = pl.BlockSpec((1, 1, 1, RW_HEAD, rw), lambda d, b, c: (d, b, c, 0, 0))
    return pl.pallas_call(
        _rw_chunk_kernel,
        grid=(N_DIR, bsz, nc),
        in_specs=[tok, tok, tok, dtok, dtok, dtok],
        out_specs=[mat, mat, dtok, dtok],
        out_shape=[jax.ShapeDtypeStruct((N_DIR, bsz, nc, RW_HEAD, rw), F32)] * 2
        + [jax.ShapeDtypeStruct((N_DIR, bsz, n, rw), F32)] * 2,
        compiler_params=_params("parallel", "parallel", "parallel"),
        name="rw_chunks",
    )(r, v, kh, lw, q, kt)


def _rw_state_kernel(emit_y, s0_ref, g_ref, h_ref, rh_ref, y0_ref, *rest):
    if emit_y:
        y_ref, sfin_ref, s_scr = rest
    else:
        sfin_ref, s_scr = rest
    c = pl.program_id(2)
    hd = RW_HEAD

    @pl.when(c == 0)
    def _():
        s_scr[...] = s0_ref[0, 0]

    sls = [slice(h * hd, (h + 1) * hd) for h in range(s_scr.shape[-1] // hd)]
    s_all = s_scr[...].astype(BF16)
    s = [s_all[:, sl] for sl in sls]
    g_all = g_ref[0, 0, 0].astype(BF16)
    s_new = [_mm(s[h], g_all[:, sl]) for h, sl in enumerate(sls)]
    if emit_y:
        rh_all = rh_ref[0, 0].astype(BF16)
        y = [_mm(rh_all[:, sl], s[h], ((1,), (1,))) for h, sl in enumerate(sls)]
        y_ref[0, 0] = jnp.concatenate(y, axis=-1) + y0_ref[0, 0]
    s_scr[...] = jnp.concatenate(s_new, axis=-1) + h_ref[0, 0, 0]

    @pl.when(c == pl.num_programs(2) - 1)
    def _():
        sfin_ref[0, 0] = s_scr[...]


def _rw_state(s0, gmat, hmat, rh, y0, emit_y):
    n_dir, bsz, nc, hd, rw = gmat.shape
    n = rh.shape[2]
    order = lambda d, c: c + d * (nc - 1 - 2 * c)
    mat = pl.BlockSpec((1, 1, 1, hd, rw), lambda d, b, c: (d, b, order(d, c), 0, 0))
    dtok = pl.BlockSpec((1, 1, RW_CHUNK, rw), lambda d, b, c: (d, b, order(d, c), 0))
    st = pl.BlockSpec((1, 1, hd, rw), lambda d, b, c: (d, b, 0, 0))
    out_specs = [st]
    out_shape = [jax.ShapeDtypeStruct((n_dir, bsz, hd, rw), F32)]
    if emit_y:
        out_specs = [dtok] + out_specs
        out_shape = [jax.ShapeDtypeStruct((n_dir, bsz, n, rw), F32)] + out_shape
    return pl.pallas_call(
        functools.partial(_rw_state_kernel, emit_y),
        grid=(n_dir, bsz, nc),
        in_specs=[st, mat, mat, dtok, dtok],
        out_specs=out_specs,
        out_shape=out_shape,
        scratch_shapes=[pltpu.VMEM((hd, rw), F32)],
        compiler_params=_params("parallel", "parallel", "arbitrary"),
        name="rw_state_y" if emit_y else "rw_state",
    )(s0, gmat, hmat, rh, y0)


def _outproj_kernel(x_ref, y5_ref, yf_ref, yb_ref, bv_ref, g_ref, lnw_ref, lnb_ref, hsel_ref,
                    hselt_ref, wo_ref, gt1_ref, g2_ref, sc2_ref, sh2_ref, rw_ref, rb_ref,
                    x1_ref, f_ref, idx_ref, gate_ref, rank_ref, cnt_ref, carry):
    first = jnp.logical_and(pl.program_id(0) == 0, pl.program_id(1) == 0)

    @pl.when(first)
    def _():
        carry[...] = jnp.zeros_like(carry)

    inv_hd = 1.0 / RW_HEAD
    y = yf_ref[0, 0] + yb_ref[0, 0]
    mean = _mm_f32(_mm_f32(y, hsel_ref[...]), hselt_ref[...]) * inv_hd
    yc = y - mean
    var = _mm_f32(_mm_f32(yc * yc, hsel_ref[...]), hselt_ref[...]) * inv_hd
    yn = yc * lax.rsqrt(var + RW_GN_EPS) * lnw_ref[...] + lnb_ref[...]
    yr = (yn + bv_ref[0]) * g_ref[0]
    mix = _mm(jnp.concatenate([y5_ref[0], yr], axis=-1), wo_ref[...])
    x1 = x_ref[0] + gt1_ref[0] * mix
    x1_ref[0] = x1
    f = _rms_mod(x1, g2_ref[...], sc2_ref[0], sh2_ref[0])
    f_ref[0] = f.astype(BF16)

    logits = _mm_f32(f, rw_ref[...]) + rb_ref[...]
    tm, ne = logits.shape
    eid = lax.broadcasted_iota(jnp.int32, (tm, ne), 1)
    work = logits
    sel = jnp.zeros((tm, ne), F32)
    idx_cols, val_cols = [], []
    for _ in range(TOP_K):
        top = jnp.max(work, axis=-1, keepdims=True)
        pick = jnp.min(jnp.where(work == top, eid, ne), axis=-1, keepdims=True)
        hit = eid == pick
        sel = jnp.where(hit, 1.0, sel)
        work = jnp.where(hit, -jnp.inf, work)
        idx_cols.append(pick)
        val_cols.append(top)
    exps = [jnp.exp(vk - val_cols[0]) for vk in val_cols]
    denom = exps[0] + exps[1] + exps[2] + exps[3]
    row = lax.broadcasted_iota(jnp.int32, (tm, tm), 0)
    colm = lax.broadcasted_iota(jnp.int32, (tm, tm), 1)
    before = _mm((colm < row).astype(F32), sel) + carry[0:1, :]
    rank_cols = [jnp.sum(jnp.where(eid == ic, before, 0.0), axis=-1, keepdims=True)
                 for ic in idx_cols]
    idx_ref[0] = jnp.concatenate(idx_cols, axis=-1)
    gate_ref[0] = jnp.concatenate([e / denom for e in exps], axis=-1)
    rank_ref[0] = jnp.concatenate(rank_cols, axis=-1).astype(jnp.int32)
    total = carry[0:1, :] + jnp.sum(sel, axis=0, keepdims=True)
    carry[...] = jnp.broadcast_to(total, carry.shape)
    cnt_ref[...] = jnp.broadcast_to(total, cnt_ref.shape).astype(jnp.int32)


def _outproj(x, y5, ydir, bv, g, ln_w, ln_b, hsel, hselt, wo, gt1, g2, sc2, sh2, router_w,
             router_b):
    bsz, n, d = x.shape
    rw = y5.shape[-1]
    ne = router_w.shape[-1]
    tm = ROW_TILE
    row = lambda w: pl.BlockSpec((1, tm, w), lambda b, i: (b, i, 0))
    vec = lambda w: pl.BlockSpec((1, w), lambda b, i: (0, 0))
    bvec = pl.BlockSpec((1, 1, d), lambda b, i: (b, 0, 0))
    full = lambda a: pl.BlockSpec(a.shape, lambda b, i: (0,) * a.ndim)
    return pl.pallas_call(
        _outproj_kernel,
        grid=(bsz, n // tm),
        in_specs=[row(d), row(rw),
                  pl.BlockSpec((1, 1, tm, rw), lambda b, i: (0, b, i, 0)),
                  pl.BlockSpec((1, 1, tm, rw), lambda b, i: (1, b, i, 0)),
                  row(rw), row(rw), vec(rw), vec(rw), full(hsel), full(hselt),
                  pl.BlockSpec(wo.shape, lambda b, i: (0, 0), pipeline_mode=pl.Buffered(1)),
                  bvec, vec(d), bvec, bvec, full(router_w), vec(ne)],
        out_specs=[row(d), row(d), row(TOP_K), row(TOP_K), row(TOP_K),
                   pl.BlockSpec((8, ne), lambda b, i: (0, 0))],
        out_shape=[jax.ShapeDtypeStruct((bsz, n, d), F32),
                   jax.ShapeDtypeStruct((bsz, n, d), BF16),
                   jax.ShapeDtypeStruct((bsz, n, TOP_K), jnp.int32),
                   jax.ShapeDtypeStruct((bsz, n, TOP_K), F32),
                   jax.ShapeDtypeStruct((bsz, n, TOP_K), jnp.int32),
                   jax.ShapeDtypeStruct((8, ne), jnp.int32)],
        scratch_shapes=[pltpu.VMEM((8, ne), F32)],
        compiler_params=_params("arbitrary", "arbitrary"),
        name="outproj_router",
    )(x, y5, ydir, ydir, bv, g, ln_w.reshape(1, rw), ln_b.reshape(1, rw), hsel, hselt, wo, gt1,
      g2.reshape(1, d), sc2, sh2, router_w, router_b.reshape(1, ne))


def _dispatch_kernel(idx_ref, rank_ref, start_ref, f_ref, xs_in_ref, xs_ref, sem):
    del xs_in_ref
    tm = f_ref.shape[0]

    def issue(t, _):
        for k in range(TOP_K):
            a = t * TOP_K + k
            dst = start_ref[idx_ref[a]] + rank_ref[a]
            pltpu.make_async_copy(f_ref.at[t], xs_ref.at[dst], sem).start()
        return 0

    lax.fori_loop(0, tm, issue, 0)
    for _ in range(TOP_K):
        pltpu.make_async_copy(f_ref, xs_ref.at[pl.ds(0, tm)], sem).wait()


def _dispatch(idx_flat, rank_flat, start, f3, n_slots):
    n_tok, s, lanes = f3.shape
    tm = ROW_TILE
    smem_tok = pl.BlockSpec((tm * TOP_K,), lambda i: (i,), memory_space=pltpu.SMEM)
    xs0 = jnp.zeros((n_slots, s, lanes), f3.dtype)
    return pl.pallas_call(
        _dispatch_kernel,
        grid=(n_tok // tm,),
        in_specs=[smem_tok, smem_tok,
                  pl.BlockSpec(start.shape, lambda i: (0,), memory_space=pltpu.SMEM),
                  pl.BlockSpec((tm, s, lanes), lambda i: (i, 0, 0)),
                  pl.BlockSpec(memory_space=pl.ANY)],
        out_specs=pl.BlockSpec(memory_space=pl.ANY),
        out_shape=jax.ShapeDtypeStruct((n_slots, s, lanes), f3.dtype),
        scratch_shapes=[pltpu.SemaphoreType.DMA(())],
        input_output_aliases={4: 0},
        compiler_params=_params("arbitrary"),
        name="moe_dispatch",
    )(idx_flat, rank_flat, start, f3, xs0)


def _expert_changed(be_ref, b):
    prev = be_ref[jnp.maximum(b - 1, 0)]
    return jnp.logical_or(b == 0, be_ref[b] != prev)


def _moe_gu_kernel(be_ref, nv_ref, x_ref, wg_ref, wl_ref, bg_ref, bl_ref, o_ref, wg_bf, wl_bf):
    b = pl.program_id(1)

    @pl.when(b < nv_ref[0])
    def _():
        @pl.when(_expert_changed(be_ref, b))
        def _():
            wg_bf[...] = wg_ref[0].astype(BF16)
            wl_bf[...] = wl_ref[0].astype(BF16)

        x = x_ref[...]
        glu = jnp.dot(x, wg_bf[...], preferred_element_type=F32) + bg_ref[0]
        lin = jnp.dot(x, wl_bf[...], preferred_element_type=F32) + bl_ref[0]
        glu = jnp.minimum(glu, SWIGLU_LIMIT)
        lin = jnp.clip(lin, -SWIGLU_LIMIT, SWIGLU_LIMIT)
        o_ref[...] = ((lin + 1.0) * glu * jax.nn.sigmoid(SWIGLU_ALPHA * glu)).astype(BF16)

    @pl.when(b >= nv_ref[0])
    def _():
        o_ref[...] = jnp.zeros_like(o_ref)


def _moe_gu(block_e, n_valid, xs, w_gu, b_gu):
    n_slots, d = xs.shape
    ne, _, two_de = w_gu.shape
    de = two_de // 2
    tn = 512
    nj = de // tn
    nb = n_slots // MOE_ROWS
    blk = lambda b, nv: jnp.minimum(b, nv[0] - 1)
    grid_spec = pltpu.PrefetchScalarGridSpec(
        num_scalar_prefetch=2,
        grid=(nj, nb),
        in_specs=[pl.BlockSpec((MOE_ROWS, d), lambda j, b, be, nv: (blk(b, nv), 0)),
                  pl.BlockSpec((1, d, tn), lambda j, b, be, nv: (be[blk(b, nv)], 0, j)),
                  pl.BlockSpec((1, d, tn), lambda j, b, be, nv: (be[blk(b, nv)], 0, nj + j)),
                  pl.BlockSpec((1, 1, tn), lambda j, b, be, nv: (be[blk(b, nv)], 0, j)),
                  pl.BlockSpec((1, 1, tn), lambda j, b, be, nv: (be[blk(b, nv)], 0, nj + j))],
        out_specs=pl.BlockSpec((MOE_ROWS, tn), lambda j, b, be, nv: (b, j)),
        scratch_shapes=[pltpu.VMEM((d, tn), BF16)] * 2)
    return pl.pallas_call(
        _moe_gu_kernel,
        grid_spec=grid_spec,
        out_shape=jax.ShapeDtypeStruct((n_slots, de), BF16),
        compiler_params=_params("arbitrary", "arbitrary"),
        name="moe_gate_up",
    )(block_e, n_valid, xs, w_gu, w_gu, b_gu.reshape(ne, 1, two_de), b_gu.reshape(ne, 1, two_de))


def _moe_dn_kernel(be_ref, nv_ref, a_ref, w_ref, bd_ref, o_ref, w_bf):
    b = pl.program_id(1)

    @pl.when(b < nv_ref[0])
    def _():
        @pl.when(_expert_changed(be_ref, b))
        def _():
            w_bf[...] = w_ref[0].astype(BF16)

        y = jnp.dot(a_ref[...], w_bf[...], preferred_element_type=F32) + bd_ref[0]
        for s in range(o_ref.shape[1]):
            o_ref[:, s, :] = y[:, s * LANES:(s + 1) * LANES]

    @pl.when(b >= nv_ref[0])
    def _():
        o_ref[...] = jnp.zeros_like(o_ref)


def _moe_dn(block_e, n_valid, act, w_dn, b_dn):
    n_slots, de = act.shape
    ne, _, d = w_dn.shape
    tn = 1024
    nb = n_slots // MOE_ROWS
    blk = lambda b, nv: jnp.minimum(b, nv[0] - 1)
    grid_spec = pltpu.PrefetchScalarGridSpec(
        num_scalar_prefetch=2,
        grid=(d // tn, nb),
        in_specs=[pl.BlockSpec((MOE_ROWS, de), lambda j, b, be, nv: (blk(b, nv), 0)),
                  pl.BlockSpec((1, de, tn), lambda j, b, be, nv: (be[blk(b, nv)], 0, j)),
                  pl.BlockSpec((1, 1, tn), lambda j, b, be, nv: (be[blk(b, nv)], 0, j))],
        out_specs=pl.BlockSpec((MOE_ROWS, tn // LANES, LANES),
                               lambda j, b, be, nv: (b, j, 0)),
        scratch_shapes=[pltpu.VMEM((de, tn), BF16)])
    return pl.pallas_call(
        _moe_dn_kernel,
        grid_spec=grid_spec,
        out_shape=jax.ShapeDtypeStruct((n_slots, d // LANES, LANES), F32),
        compiler_params=_params("arbitrary", "arbitrary"),
        name="moe_down",
    )(block_e, n_valid, act, w_dn, b_dn.reshape(ne, 1, d))


def _combine_kernel(idx_ref, rank_ref, start_ref, x1_ref, gate_ref, gt2_ref, fg_ref, y_ref,
                    o_ref, buf, sem):
    tm = x1_ref.shape[0]

    def issue(t, _):
        for k in range(TOP_K):
            a = t * TOP_K + k
            src = start_ref[idx_ref[a]] + rank_ref[a]
            pltpu.make_async_copy(y_ref.at[src], buf.at[k, t], sem).start()
        return 0

    lax.fori_loop(0, tm, issue, 0)
    for k in range(TOP_K):
        pltpu.make_async_copy(y_ref.at[pl.ds(0, tm)], buf.at[k], sem).wait()
    moe = gate_ref[0] * buf[0]
    for k in range(1, TOP_K):
        moe = moe + gate_ref[k] * buf[k]
    x2 = x1_ref[...] + gt2_ref[0] * moe
    ms = jnp.sum(jnp.sum(x2 * x2, axis=2, keepdims=True), axis=1, keepdims=True) * (
        1.0 / (x2.shape[1] * x2.shape[2]))
    o_ref[...] = x2 * lax.rsqrt(ms + NORM_EPS) * fg_ref[...]


def _combine(idx_flat, rank_flat, start, x1_3, gate4, gt2_3, fg_3, y3, tiles_per_batch):
    n_tok, s, lanes = x1_3.shape
    tm = ROW_TILE
    smem_tok = pl.BlockSpec((tm * TOP_K,), lambda i: (i,), memory_space=pltpu.SMEM)
    return pl.pallas_call(
        _combine_kernel,
        grid=(n_tok // tm,),
        in_specs=[smem_tok, smem_tok,
                  pl.BlockSpec(start.shape, lambda i: (0,), memory_space=pltpu.SMEM),
                  pl.BlockSpec((tm, s, lanes), lambda i: (i, 0, 0)),
                  pl.BlockSpec((TOP_K, tm, 1, 1), lambda i: (0, i, 0, 0)),
                  pl.BlockSpec((1, s, lanes), lambda i: (i // tiles_per_batch, 0, 0)),
                  pl.BlockSpec((s, lanes), lambda i: (0, 0)),
                  pl.BlockSpec(memory_space=pl.ANY)],
        out_specs=pl.BlockSpec((tm, s, lanes), lambda i: (i, 0, 0)),
        out_shape=jax.ShapeDtypeStruct((n_tok, s, lanes), F32),
        scratch_shapes=[pltpu.VMEM((TOP_K, tm, s, lanes), F32), pltpu.SemaphoreType.DMA(())],
        compiler_params=_params("arbitrary"),
        name="moe_combine_final",
    )(idx_flat, rank_flat, start, x1_3, gate4, gt2_3, fg_3, y3)


def _block_diag2(w):
    z = jnp.zeros_like(w[0])
    return jnp.concatenate([jnp.concatenate([w[0], z], 1), jnp.concatenate([z, w[1]], 1)], 0)


def kernel(x, c, ctx, c_ctx, mod_w, mod_b, norm1_g, w_in, s5_a_re, s5_a_im, s5_log_dt, s5_b_re,
           s5_b_im, s5_c_re, s5_c_im, s5_d, s5_glu_w, s5_glu_b, rw_mu, rw_w0, rw_w2, rw_a0, rw_a2,
           rw_g2, rw_k_k, rw_k_a, rw_r_k, rw_ln_w, rw_ln_b, w_out, norm2_g, router_w, router_b,
           exp_w_gu, exp_b_gu, exp_w_dn, exp_b_dn, final_g):
    assert mod_w.shape[0] == 1, "single-layer stack only"
    bsz, n_lat, d = x.shape
    n_ctx = ctx.shape[1]
    assert bsz == 2 and n_ctx % ROW_TILE == 0 and n_lat % ROW_TILE == 0
    s5w = s5_d.shape[-1]
    rww = rw_k_k.shape[-1]
    n_heads = rww // RW_HEAD
    ne = router_w.shape[-1]

    cond8 = jnp.zeros((8, d), F32).at[:bsz].set(c).at[bsz].set(c_ctx)
    mod = _adaln(cond8, mod_w[0], mod_b[0])
    sh1, sc1, gt1, sh2, sc2, gt2 = [m[:bsz, None, :] for m in jnp.split(mod, 6, axis=-1)]
    csh1, csc1 = [jnp.broadcast_to(m[bsz][None, None, :], (bsz, 1, d))
                  for m in jnp.split(mod, 6, axis=-1)[:2]]

    wu = w_in[0][:, :s5w].astype(BF16)
    wz = w_in[0][:, s5w:].astype(BF16)
    u_lat, z_lat = _inproj(x, norm1_g[0], sc1, sh1, wu, wz)
    u_ctx, z_ctx = _inproj(ctx, norm1_g[0], csc1, csh1, wu, wz)

    t5 = S5_CHUNK
    n_all = n_ctx + n_lat
    groups = s5w // S5_GROUP
    u_all = jnp.concatenate([u_ctx, u_lat], axis=1)
    u_grp = jnp.transpose(u_all.reshape(bsz, n_all // t5, t5, groups, S5_GROUP),
                          (3, 1, 0, 2, 4)).reshape(groups, (n_all // t5) * bsz, t5 * S5_GROUP)
    s5c = _s5_constants(s5_a_re[0], s5_a_im[0], s5_log_dt[0], s5_b_re[0], s5_b_im[0], s5_c_re[0],
                        s5_c_im[0], s5_d[0], s5_glu_w[0], s5_glu_b[0])
    y5_grp = _s5(u_grp, s5c, n_ctx // t5, bsz)
    y5 = jnp.transpose(y5_grp.reshape(groups, n_lat // t5, bsz, t5, S5_GROUP),
                       (2, 1, 3, 0, 4)).reshape(bsz, n_lat, s5w)

    lanes_idx = jnp.arange(rww) // RW_HEAD
    hsel = (lanes_idx[:, None] == jnp.arange(LANES)[None, :]).astype(F32)
    hselt = hsel.T
    feat_consts = (rw_mu[0], rw_w0[0].reshape(1, N_DIR * rww), _block_diag2(rw_w2[0]),
                   rw_a0[0].reshape(1, N_DIR * rww), _block_diag2(rw_a2[0]), rw_g2[0],
                   rw_k_k[0].reshape(1, rww), rw_k_a[0].reshape(1, rww),
                   rw_r_k[0].reshape(1, rww), hsel, hselt)
    fc = _rw_features(z_ctx, n_ctx, False, *feat_consts)
    fl = _rw_features(z_lat, GRID_W, True, *feat_consts)

    def scan_inputs(f):
        r, v, kh, _, _, lw, q, kt = f
        return r, v, kh, lw, q, kt

    gc, hc, rhc, y0c = _rw_chunks(*scan_inputs(fc))
    gl, hl, rhl, y0l = _rw_chunks(*scan_inputs(fl))
    s_zero = jnp.zeros((N_DIR, bsz, RW_HEAD, rww), F32)
    (s_ctx,) = _rw_state(s_zero, gc, hc, rhc, y0c, emit_y=False)
    y_dir, _ = _rw_state(s_ctx, gl, hl, rhl, y0l, emit_y=True)

    x1, f, idx4, gate4, rank4, counts = _outproj(
        x, y5, y_dir, fl[4], fl[3], rw_ln_w[0], rw_ln_b[0], hsel, hselt, w_out[0].astype(BF16),
        gt1, norm2_g[0], sc2, sh2, router_w[0], router_b[0])

    n_tok = bsz * n_lat
    cnt = counts[0]
    padded = (cnt + MOE_ROWS - 1) // MOE_ROWS * MOE_ROWS
    pend = jnp.cumsum(padded)
    start = (pend - padded).astype(jnp.int32)
    nb = n_tok * TOP_K // MOE_ROWS + ne
    n_slots = nb * MOE_ROWS
    block_e = jnp.minimum(
        jnp.sum(pend[None, :] <= (jnp.arange(nb) * MOE_ROWS)[:, None], axis=1), ne - 1
    ).astype(jnp.int32)
    n_valid = (pend[-1] // MOE_ROWS).astype(jnp.int32).reshape(1)
    idx_flat = idx4.reshape(-1)
    rank_flat = rank4.reshape(-1)

    s = d // LANES
    xs3 = _dispatch(idx_flat, rank_flat, start, f.reshape(n_tok, s, LANES), n_slots)
    act = _moe_gu(block_e, n_valid, xs3.reshape(n_slots, d), exp_w_gu[0], exp_b_gu[0])
    y3 = _moe_dn(block_e, n_valid, act, exp_w_dn[0], exp_b_dn[0])
    gate_t = jnp.transpose(gate4.reshape(n_tok, TOP_K), (1, 0)).reshape(TOP_K, n_tok, 1, 1)
    out3 = _combine(idx_flat, rank_flat, start, x1.reshape(n_tok, s, LANES), gate_t,
                    gt2.reshape(bsz, s, LANES), final_g.reshape(s, LANES), y3, n_lat // ROW_TILE)
    return out3.reshape(bsz, n_lat, d)
```

```python
import functools
import math

import jax
import jax.numpy as jnp
from jax import lax
from jax.experimental import pallas as pl
from jax.experimental.pallas import tpu as pltpu

F32 = jnp.float32
BF16 = jnp.bfloat16
HIGHEST = lax.Precision.HIGHEST

LANES = 128
VMEM_LIMIT_BYTES = 56 * 1024 * 1024

NORM_EPS = 1e-5
N_DIR = 2
S5_GROUP = 16
S5_STATE = 64
S5_CHUNK = 8
S5_LANE_GROUPS = LANES // S5_GROUP
S5_SCAN_ROWS = 8
RW_HEAD = 64
RW_CHUNK = 64
RW_GN_EPS = 64e-5
GRID_W = 64
TOP_K = 4
SWIGLU_ALPHA = 1.702
SWIGLU_LIMIT = 7.0
ROW_TILE = 256
MOE_ROWS = 256


def _params(*sem):
    return pltpu.CompilerParams(dimension_semantics=sem, vmem_limit_bytes=VMEM_LIMIT_BYTES)


def _mm(a, b, dims=((1,), (0,))):
    return lax.dot_general(a.astype(BF16), b.astype(BF16), (dims, ((), ())),
                           preferred_element_type=F32)


def _split_bf16(a):
    hi = a.astype(BF16)
    return hi, (a - hi.astype(F32)).astype(BF16)


def _mm_sel(a, sel):
    hi, lo = _split_bf16(a)
    m = a.shape[0]
    out = jnp.dot(jnp.concatenate([hi, lo], axis=0), sel.astype(BF16), preferred_element_type=F32)
    return out[:m] + out[m:]


def _sel_mm(sel, a):
    hi, lo = _split_bf16(a)
    n = a.shape[1]
    out = jnp.dot(sel.astype(BF16), jnp.concatenate([hi, lo], axis=1), preferred_element_type=F32)
    return out[:, :n] + out[:, n:]


def _mm_f32(a, b, dims=((1,), (0,))):
    return lax.dot_general(a, b, (dims, ((), ())), precision=HIGHEST,
                           preferred_element_type=F32)


def _adaln_kernel(cond_ref, w_ref, b_ref, o_ref):
    c = cond_ref[...]
    o_ref[...] = _mm_f32(c * jax.nn.sigmoid(c), w_ref[...]) + b_ref[...]


def _adaln(cond8, w, b):
    d, n = w.shape
    tn = 1536
    return pl.pallas_call(
        _adaln_kernel,
        grid=(n // tn,),
        in_specs=[pl.BlockSpec((8, d), lambda j: (0, 0)),
                  pl.BlockSpec((d, tn), lambda j: (0, j)),
                  pl.BlockSpec((1, tn), lambda j: (0, j))],
        out_specs=pl.BlockSpec((8, tn), lambda j: (0, j)),
        out_shape=jax.ShapeDtypeStruct((8, n), F32),
        compiler_params=_params("parallel"),
        name="adaln",
    )(cond8, w, b.reshape(1, n))


def _rms_mod(x, g, sc, sh):
    y = x * lax.rsqrt(jnp.mean(x * x, axis=-1, keepdims=True) + NORM_EPS)
    return (y * g) * (1.0 + sc) + sh


def _inproj_kernel(x_ref, g_ref, sc_ref, sh_ref, wu_ref, wz_ref, u_ref, z_ref):
    h = _rms_mod(x_ref[0], g_ref[...], sc_ref[0], sh_ref[0]).astype(BF16)
    u_ref[0] = jnp.dot(h, wu_ref[...], preferred_element_type=F32)
    z_ref[0] = jnp.dot(h, wz_ref[...], preferred_element_type=F32)


def _inproj(x, g, sc, sh, wu, wz):
    bsz, n, d = x.shape
    nu, nz = wu.shape[1], wz.shape[1]
    tm = ROW_TILE
    const = dict(pipeline_mode=pl.Buffered(1))
    return pl.pallas_call(
        _inproj_kernel,
        grid=(bsz, n // tm),
        in_specs=[pl.BlockSpec((1, tm, d), lambda b, i: (b, i, 0)),
                  pl.BlockSpec((1, d), lambda b, i: (0, 0)),
                  pl.BlockSpec((1, 1, d), lambda b, i: (b, 0, 0)),
                  pl.BlockSpec((1, 1, d), lambda b, i: (b, 0, 0)),
                  pl.BlockSpec((d, nu), lambda b, i: (0, 0), **const),
                  pl.BlockSpec((d, nz), lambda b, i: (0, 0), **const)],
        out_specs=[pl.BlockSpec((1, tm, nu), lambda b, i: (b, i, 0)),
                   pl.BlockSpec((1, tm, nz), lambda b, i: (b, i, 0))],
        out_shape=[jax.ShapeDtypeStruct((bsz, n, nu), F32),
                   jax.ShapeDtypeStruct((bsz, n, nz), F32)],
        compiler_params=_params("parallel", "parallel"),
        name="inproj",
    )(x, g.reshape(1, d), sc, sh, wu, wz)


def _s5_constants(a_re, a_im, log_dt, b_re, b_im, c_re, c_im, d_skip, glu_w, glu_b):
    t = S5_CHUNK
    g, p = a_re.shape[1], a_re.shape[2]
    h = S5_GROUP
    n = jnp.arange(t + 1, dtype=F32)[:, None, None]
    fm, em, lam_re, lam_im = [], [], [], []
    kl = []
    for d in range(N_DIR):
        ar, ai = a_re[d].astype(F32), a_im[d].astype(F32)
        dt = jnp.exp(log_dt[d].astype(F32))[:, None]
        mag = jnp.exp(n * (ar * dt))
        pr, pi = mag * jnp.cos(n * (ai * dt)), mag * jnp.sin(n * (ai * dt))
        lr, li = pr[1], pi[1]
        den = ar * ar + ai * ai
        cf_re = ((lr - 1.0) * ar + li * ai) / den
        cf_im = (li * ar - (lr - 1.0) * ai) / den
        br, bi = b_re[d].astype(F32), b_im[d].astype(F32)
        bb_re = cf_re[..., None] * br - cf_im[..., None] * bi
        bb_im = cf_re[..., None] * bi + cf_im[..., None] * br
        cr, ci = c_re[d].astype(F32), c_im[d].astype(F32)
        cl_re = cr[None] * pr[:, :, None, :] - ci[None] * pi[:, :, None, :]
        cl_im = cr[None] * pi[:, :, None, :] + ci[None] * pr[:, :, None, :]
        lag = (jnp.einsum('nghp,gpi->nghi', cl_re[:t], bb_re, precision=HIGHEST)
               - jnp.einsum('nghp,gpi->nghi', cl_im[:t], bb_im, precision=HIGHEST))
        kl.append(lag)
        pw = jnp.arange(t - 1, -1, -1) if d == 0 else jnp.arange(t)
        ps_re, ps_im = pr[pw], pi[pw]
        f_re = ps_re[..., None] * bb_re[None] - ps_im[..., None] * bb_im[None]
        f_im = ps_re[..., None] * bb_im[None] + ps_im[..., None] * bb_re[None]
        fm.append((jnp.transpose(f_re, (1, 0, 3, 2)).reshape(g, t * h, p),
                   jnp.transpose(f_im, (1, 0, 3, 2)).reshape(g, t * h, p)))
        pe = jnp.arange(1, t + 1) if d == 0 else jnp.arange(t, 0, -1)
        e_re = jnp.transpose(cl_re[pe], (1, 3, 0, 2)).reshape(g, p, t * h)
        e_im = -jnp.transpose(cl_im[pe], (1, 3, 0, 2)).reshape(g, p, t * h)
        em.append((e_re, e_im))
        lam_re.append(pr[t])
        lam_im.append(pi[t])
    gl = S5_LANE_GROUPS
    no = g // gl
    eye_a = jnp.eye(gl, dtype=F32)
    fg = jnp.stack([jnp.stack([fm[0][0], fm[1][0]], axis=2),
                    jnp.stack([fm[0][1], fm[1][1]], axis=2)], axis=2)
    fg = fg.reshape(no, gl, t, h, 2, N_DIR, p)
    fmat = jnp.einsum('ab,oasiqdp->osaiqdbp', eye_a, fg).reshape(no, t * gl * h, 2 * N_DIR * gl * p)
    eg = jnp.stack([jnp.stack([em[0][0], em[1][0]], axis=1),
                    jnp.stack([em[0][1], em[1][1]], axis=1)], axis=1)
    eg = eg.reshape(no, gl, 2, N_DIR, p, t, h)
    emat = jnp.einsum('ab,oaqdptj->oqdaptbj', eye_a, eg).reshape(no, 2 * N_DIR * gl * p, t * gl * h)
    s_idx = jnp.arange(t)[:, None]
    t_idx = jnp.arange(t)[None, :]
    kf = jnp.where((t_idx >= s_idx)[..., None, None, None],
                   kl[0][jnp.clip(t_idx - s_idx, 0, t - 1)], 0.0)
    kb = jnp.where((s_idx >= t_idx)[..., None, None, None],
                   kl[1][jnp.clip(s_idx - t_idx, 0, t - 1)], 0.0)
    kg = (kf + kb).reshape(t, t, no, gl, h, h)
    kmat = jnp.einsum('ab,stoaji->osaitbj', eye_a, kg).reshape(no, t * gl * h, t * gl * h)
    lam = jnp.stack([lam_re[0], lam_im[0], lam_re[1], lam_im[1]], axis=1)
    lam = jnp.transpose(lam.reshape(no, gl, 4, p), (0, 2, 1, 3)).reshape(no, 4, gl * p)
    gmat = jnp.einsum('ab,oahk->oahbk', eye_a,
                      glu_w.astype(F32).reshape(no, gl, h, h)).reshape(no, gl * h, gl * h)
    dvec = d_skip.astype(F32).reshape(no, 1, gl * h)
    bvec = glu_b.astype(F32).reshape(no, 1, gl * h)
    return (fmat.astype(BF16), emat.astype(BF16), kmat.astype(BF16), lam, dvec, bvec,
            gmat.astype(BF16))


def _s5_chunk_rows(u_ref, b, start, n_chunks):
    t = S5_CHUNK
    return jnp.concatenate(
        [u_ref[b, pl.ds(start + s, n_chunks, stride=t), :] for s in range(t)], axis=-1)


def _s5_in_kernel(u_ref, f_ref, z_ref):
    bsz, n_tok, _ = u_ref.shape
    for b in range(bsz):
        z_ref[b] = _mm(_s5_chunk_rows(u_ref, b, 0, n_tok // S5_CHUNK), f_ref[0])


def _s5_scan_kernel(n_ctx_chunks, z_ref, lam_ref, hs_ref):
    n_chunks = z_ref.shape[1]
    q = z_ref.shape[2] // 4
    rows = S5_SCAN_ROWS
    n_blocks, ctx_blocks = n_chunks // rows, n_ctx_chunks // rows
    lam = lam_ref[0]
    lf_re, lf_im, lb_re, lb_im = lam[0:1], lam[1:2], lam[2:3], lam[3:4]

    def block(k, carry):
        f_re, f_im, b_re, b_im = carry
        kb = jnp.where(k < ctx_blocks, ctx_blocks - 1 - k, n_blocks - 1 + ctx_blocks - k)
        rf = pl.multiple_of(k * rows, rows)
        rb = pl.multiple_of(kb * rows, rows)
        zf = z_ref[0, pl.ds(rf, rows), :]
        zb = z_ref[0, pl.ds(rb, rows), :]
        ent = [[], [], [], []]
        for j in range(rows):
            jb = rows - 1 - j
            for lst, val in zip(ent, (f_re, f_im, b_re, b_im)):
                lst.append(val)
            f_re, f_im = (lf_re * f_re - lf_im * f_im + zf[j:j + 1, 0:q],
                          lf_re * f_im + lf_im * f_re + zf[j:j + 1, 2 * q:3 * q])
            b_re, b_im = (lb_re * b_re - lb_im * b_im + zb[jb:jb + 1, q:2 * q],
                          lb_re * b_im + lb_im * b_re + zb[jb:jb + 1, 3 * q:4 * q])
        hs_ref[0, pl.ds(rf, rows), 0:q] = jnp.concatenate(ent[0], axis=0)
        hs_ref[0, pl.ds(rf, rows), 2 * q:3 * q] = jnp.concatenate(ent[1], axis=0)
        hs_ref[0, pl.ds(rb, rows), q:2 * q] = jnp.concatenate(ent[2][::-1], axis=0)
        hs_ref[0, pl.ds(rb, rows), 3 * q:4 * q] = jnp.concatenate(ent[3][::-1], axis=0)
        return f_re, f_im, b_re, b_im

    zero = jnp.zeros((1, q), F32)
    lax.fori_loop(0, n_blocks, block, (zero, zero, zero, zero))


def _s5_out_kernel(n_ctx, u_ref, hs_ref, e_ref, k_ref, g_ref, d_ref, b_ref, o_ref, y_scr):
    t = S5_CHUNK
    bsz, n_tok, lanes = u_ref.shape
    n_chunks = (n_tok - n_ctx) // t
    for b in range(bsz):
        x = _s5_chunk_rows(u_ref, b, n_ctx, n_chunks)
        y = _mm(x, k_ref[0]) + _mm(hs_ref[b, n_ctx // t:, :], e_ref[0])
        for s in range(t):
            y_scr[pl.ds(s, n_chunks, stride=t), :] = y[:, s * lanes:(s + 1) * lanes]
        y = jax.nn.gelu(y_scr[...] + d_ref[0] * u_ref[b, n_ctx:, :])
        gate = _mm(y, g_ref[0]) + b_ref[0]
        o_ref[b] = y * jax.nn.sigmoid(gate)


def _s5(u_all, consts, n_ctx):
    fmat, emat, kmat, lam, dvec, bvec, gmat = consts
    bsz, n_tok, width = u_all.shape
    t = S5_CHUNK
    n_chunks = n_tok // t
    no, rows_k, cols_f = fmat.shape
    assert n_ctx % (t * S5_SCAN_ROWS) == 0 and n_tok % (t * S5_SCAN_ROWS) == 0
    tok = pl.BlockSpec((1, n_tok, LANES), lambda o, b: (b, 0, o))
    state = pl.BlockSpec((1, n_chunks, cols_f), lambda o, b: (b, 0, o))
    full = lambda a: pl.BlockSpec((1,) + a.shape[1:], lambda o, b: (o, 0, 0))
    z = pl.pallas_call(
        _s5_in_kernel,
        grid=(no, bsz),
        in_specs=[tok, full(fmat)],
        out_specs=state,
        out_shape=jax.ShapeDtypeStruct((bsz, n_chunks, no * cols_f), F32),
        compiler_params=_params("parallel", "parallel"),
        name="s5_in",
    )(u_all, fmat)
    hs = pl.pallas_call(
        functools.partial(_s5_scan_kernel, n_ctx // t),
        grid=(no, bsz),
        in_specs=[state, full(lam)],
        out_specs=state,
        out_shape=jax.ShapeDtypeStruct((bsz, n_chunks, no * cols_f), F32),
        compiler_params=_params("parallel", "parallel"),
        name="s5_scan",
    )(z, lam)
    n_lat = n_tok - n_ctx
    return pl.pallas_call(
        functools.partial(_s5_out_kernel, n_ctx),
        grid=(no, bsz),
        in_specs=[tok, state, full(emat), full(kmat), full(gmat), full(dvec), full(bvec)],
        out_specs=pl.BlockSpec((1, n_lat, LANES), lambda o, b: (b, 0, o)),
        out_shape=jax.ShapeDtypeStruct((bsz, n_lat, width), F32),
        scratch_shapes=[pltpu.VMEM((n_lat, LANES), F32)],
        compiler_params=_params("parallel", "parallel"),
        name="s5_out",
    )(u_all, hs, emat, kmat, gmat, dvec, bvec)


def _rw_feat_kernel(width, has_vert, n_tiles, z_ref, zp_ref, zn_ref, mu_ref, w0_ref, w2_ref,
                    a0_ref, a2_ref, g2_ref, kk_ref, ka_ref, rk_ref, hsel_ref, hselt_ref,
                    r_ref, v_ref, kh_ref, g_ref, bv_ref, lw_ref, q_ref, kt_ref):
    z = z_ref[0]
    t0, cz = z.shape
    rw = r_ref.shape[-1]
    mu = mu_ref[...]
    tok = lax.broadcasted_iota(jnp.int32, (t0, 1), 0)
    col = tok % width
    left = pltpu.roll(z, 1, 0)
    right = pltpu.roll(z, t0 - 1, 0)
    out = z + jnp.where(col != 0, mu[0:1] * (left - z), 0.0)
    out = out + jnp.where(col != width - 1, mu[1:2] * (right - z), 0.0)
    if has_vert:
        i = pl.program_id(1)
        up = jnp.concatenate([zp_ref[0], z[:t0 - width]], axis=0)
        down = jnp.concatenate([z[width:], zn_ref[0]], axis=0)
        up_ok = jnp.logical_or(i > 0, tok >= width)
        down_ok = jnp.logical_or(i < n_tiles - 1, tok < t0 - width)
        out = out + jnp.where(up_ok, mu[2:3] * (up - z), 0.0)
        out = out + jnp.where(down_ok, mu[3:4] * (down - z), 0.0)
    r = out[:, 0:rw]
    k = out[:, rw:2 * rw]
    v = out[:, 2 * rw:3 * rw]
    o = 3 * rw
    lora = w2_ref.shape[0]
    xw = out[:, o:o + lora]
    xa = out[:, o + lora:o + 2 * lora]
    xg = out[:, o + 2 * lora:]
    dec = w0_ref[...] + _mm(jnp.tanh(xw), w2_ref[...])
    lw = -math.exp(-0.5) * jax.nn.sigmoid(dec)
    a = jax.nn.sigmoid(a0_ref[...] + _mm(xa, a2_ref[...]))
    g = _mm(jax.nn.sigmoid(xg), g2_ref[...])
    kk = k * kk_ref[...]
    ssq = _mm_sel(kk * kk, hsel_ref[...])
    inv = 1.0 / jnp.maximum(jnp.sqrt(ssq), 1e-12)
    kh = kk * _mm_sel(inv, hselt_ref[...])
    ka = ka_ref[...]
    kt_sum = jnp.zeros_like(k)
    for d in range(N_DIR):
        a_d = a[:, d * rw:(d + 1) * rw]
        kt_d = k * (1.0 + (a_d - 1.0) * ka)
        kt_sum = kt_sum + kt_d
        lw_ref[d, 0] = lw[:, d * rw:(d + 1) * rw]
        q_ref[d, 0] = a_d * kh
        kt_ref[d, 0] = kt_d
    bonus = _mm_sel(_mm_sel(r * kt_sum * rk_ref[...], hsel_ref[...]), hselt_ref[...])
    r_ref[0] = r
    v_ref[0] = v
    kh_ref[0] = kh
    g_ref[0] = g
    bv_ref[0] = bonus * v


def _rw_features(z, width, has_vert, mu, w0, w2blk, a0, a2blk, g2, k_k, k_a, r_k, hsel, hselt):
    bsz, n, cz = z.shape
    rw = k_k.shape[-1]
    t0 = ROW_TILE
    n_tiles = n // t0
    per = t0 // GRID_W
    nblk = n // GRID_W
    full = lambda a: pl.BlockSpec(a.shape, lambda b, i: (0,) * a.ndim)
    tok = pl.BlockSpec((1, t0, rw), lambda b, i: (b, i, 0))
    dtok = pl.BlockSpec((N_DIR, 1, t0, rw), lambda b, i: (0, b, i, 0))
    consts = (mu, w0, w2blk, a0, a2blk, g2, k_k, k_a, r_k, hsel, hselt)
    return pl.pallas_call(
        functools.partial(_rw_feat_kernel, width, has_vert, n_tiles),
        grid=(bsz, n_tiles),
        in_specs=[pl.BlockSpec((1, t0, cz), lambda b, i: (b, i, 0)),
                  pl.BlockSpec((1, GRID_W, cz), lambda b, i: (b, jnp.maximum(i * per - 1, 0), 0)),
                  pl.BlockSpec((1, GRID_W, cz),
                               lambda b, i: (b, jnp.minimum(i * per + per, nblk - 1), 0))]
        + [full(a) for a in consts],
        out_specs=[tok] * 5 + [dtok] * 3,
        out_shape=[jax.ShapeDtypeStruct((bsz, n, rw), F32)] * 5
        + [jax.ShapeDtypeStruct((N_DIR, bsz, n, rw), F32)] * 3,
        compiler_params=_params("parallel", "parallel"),
        name="rw_features",
    )(z, z, z, *consts)


def _rw_chunk_kernel(r_ref, v_ref, kh_ref, lw_ref, q_ref, kt_ref, g_ref, h_ref, rh_ref, y0_ref):
    rev = pl.program_id(0) == 1
    n = RW_CHUNK
    hd = RW_HEAD
    lw = lw_ref[0, 0]
    row = lax.broadcasted_iota(jnp.int32, (n, n), 0)
    col = lax.broadcasted_iota(jnp.int32, (n, n), 1)
    ahead = (row - col) * jnp.where(rev, -1, 1)
    incl = (ahead >= 0).astype(F32)
    strict = (ahead > 0).astype(F32)
    eye = (row == col).astype(F32)
    same_block = [(jnp.right_shift(row, s) == jnp.right_shift(col, s)).astype(F32)
                  for s in range(3, n.bit_length())]
    b_incl = _sel_mm(incl, lw)
    btot = jnp.sum(lw, axis=0, keepdims=True)
    e_neg = jnp.exp(-b_incl)
    e_rem = jnp.exp(btot - b_incl)
    q, kt = q_ref[0, 0], kt_ref[0, 0]
    pt_all = kh_ref[0] * jnp.exp(b_incl - lw)
    rt_all = r_ref[0] * jnp.exp(b_incl)
    qt_all = q * e_neg
    ktt_all = kt * e_neg
    qh_all = q * e_rem
    kth_all = kt * e_rem
    gam_all = jnp.exp(btot)
    v_all = v_ref[0]
    heads = range(lw.shape[-1] // hd)
    sls = [slice(h * hd, (h + 1) * hd) for h in heads]
    pt = [pt_all[:, s] for s in sls]
    rt = [rt_all[:, s] for s in sls]
    v = [v_all[:, s] for s in sls]
    a4 = [_mm(jnp.concatenate([pt[h], rt[h]], 0),
              jnp.concatenate([qt_all[:, sls[h]], ktt_all[:, sls[h]]], 0), ((1,), (1,)))
          for h in heads]
    nmat = [strict * a4[h][:n, :n] for h in heads]
    akv = [_mm(strict * a4[h][:n, n:], v[h]) for h in heads]
    nd = [same_block[0] * nmat[h] for h in heads]
    x = [_mm(nd[h], nd[h]) for h in heads]
    m = [eye - nd[h] for h in heads]
    m = [m[h] + _mm(m[h], x[h]) for h in heads]
    x = [_mm(x[h], x[h]) for h in heads]
    m = [m[h] + _mm(m[h], x[h]) for h in heads]
    for lvl in range(1, len(same_block)):
        ring = same_block[lvl] - same_block[lvl - 1]
        t = [_mm(m[h], ring * nmat[h]) for h in heads]
        m = [m[h] - _mm(t[h], m[h]) for h in heads]
    wu = [_mm(m[h], jnp.concatenate([pt[h], akv[h]], 1)) for h in heads]
    gh = [_mm(wu[h], qh_all[:, sls[h]], ((0,), (0,))) for h in heads]
    vk = [_mm(v[h], kth_all[:, sls[h]], ((0,), (0,))) for h in heads]
    lwu = [_mm(incl * a4[h][n:, :n], wu[h]) for h in heads]
    lv = [_mm(incl * a4[h][n:, n:], v[h]) for h in heads]
    for h in heads:
        g_ref[0, 0, 0, :, sls[h]] = eye * gam_all[:, sls[h]] - gh[h][:hd]
        h_ref[0, 0, 0, :, sls[h]] = vk[h] - gh[h][hd:]
        rh_ref[0, 0, :, sls[h]] = rt[h] - lwu[h][:, :hd]
        y0_ref[0, 0, :, sls[h]] = lv[h] - lwu[h][:, hd:]


def _rw_chunks(r, v, kh, lw, q, kt):
    bsz, n, rw = r.shape
    nc = n // RW_CHUNK
    tok = pl.BlockSpec((1, RW_CHUNK, rw), lambda d, b, c: (b, c, 0))
    dtok = pl.BlockSpec((1, 1, RW_CHUNK, rw), lambda d, b, c: (d, b, c, 0))
    mat = pl.BlockSpec((1, 1, 1, RW_HEAD, rw), lambda d, b, c: (d, b, c, 0, 0))
    return pl.pallas_call(
        _rw_chunk_kernel,
        grid=(N_DIR, bsz, nc),
        in_specs=[tok, tok, tok, dtok, dtok, dtok],
        out_specs=[mat, mat, dtok, dtok],
        out_shape=[jax.ShapeDtypeStruct((N_DIR, bsz, nc, RW_HEAD, rw), F32)] * 2
        + [jax.ShapeDtypeStruct((N_DIR, bsz, n, rw), F32)] * 2,
        compiler_params=_params("parallel", "parallel", "parallel"),
        name="rw_chunks",
    )(r, v, kh, lw, q, kt)


def _rw_state_kernel(emit_y, s0_ref, g_ref, h_ref, rh_ref, y0_ref, *rest):
    if emit_y:
        y_ref, sfin_ref, s_scr = rest
    else:
        sfin_ref, s_scr = rest
    c = pl.program_id(2)
    hd = RW_HEAD

    @pl.when(c == 0)
    def _():
        s_scr[...] = s0_ref[0, 0]

    sls = [slice(h * hd, (h + 1) * hd) for h in range(s_scr.shape[-1] // hd)]
    s_all = s_scr[...].astype(BF16)
    s = [s_all[:, sl] for sl in sls]
    g_all = g_ref[0, 0, 0].astype(BF16)
    s_new = [_mm(s[h], g_all[:, sl]) for h, sl in enumerate(sls)]
    if emit_y:
        rh_all = rh_ref[0, 0].astype(BF16)
        y = [_mm(rh_all[:, sl], s[h], ((1,), (1,))) for h, sl in enumerate(sls)]
        y_ref[0, 0] = jnp.concatenate(y, axis=-1) + y0_ref[0, 0]
    s_scr[...] = jnp.concatenate(s_new, axis=-1) + h_ref[0, 0, 0]

    @pl.when(c == pl.num_programs(2) - 1)
    def _():
        sfin_ref[0, 0] = s_scr[...]


def _rw_state(s0, gmat, hmat, rh, y0, emit_y):
    n_dir, bsz, nc, hd, rw = gmat.shape
    n = rh.shape[2]
    order = lambda d, c: c + d * (nc - 1 - 2 * c)
    mat = pl.BlockSpec((1, 1, 1, hd, rw), lambda d, b, c: (d, b, order(d, c), 0, 0))
    dtok = pl.BlockSpec((1, 1, RW_CHUNK, rw), lambda d, b, c: (d, b, order(d, c), 0))
    st = pl.BlockSpec((1, 1, hd, rw), lambda d, b, c: (d, b, 0, 0))
    out_specs = [st]
    out_shape = [jax.ShapeDtypeStruct((n_dir, bsz, hd, rw), F32)]
    if emit_y:
        out_specs = [dtok] + out_specs
        out_shape = [jax.ShapeDtypeStruct((n_dir, bsz, n, rw), F32)] + out_shape
    return pl.pallas_call(
        functools.partial(_rw_state_kernel, emit_y),
        grid=(n_dir, bsz, nc),
        in_specs=[st, mat, mat, dtok, dtok],
        out_specs=out_specs,
        out_shape=out_shape,
        scratch_shapes=[pltpu.VMEM((hd, rw), F32)],
        compiler_params=_params("parallel", "parallel", "arbitrary"),
        name="rw_state_y" if emit_y else "rw_state",
    )(s0, gmat, hmat, rh, y0)


def _outproj_kernel(x_ref, y5_ref, yf_ref, yb_ref, bv_ref, g_ref, lnw_ref, lnb_ref, hsel_ref,
                    hselt_ref, wo_ref, gt1_ref, g2_ref, sc2_ref, sh2_ref, rw_ref, rb_ref,
                    x1_ref, f_ref, idx_ref, gate_ref, rank_ref, cnt_ref, carry):
    first = jnp.logical_and(pl.program_id(0) == 0, pl.program_id(1) == 0)

    @pl.when(first)
    def _():
        carry[...] = jnp.zeros_like(carry)

    inv_hd = 1.0 / RW_HEAD
    y = yf_ref[0, 0] + yb_ref[0, 0]
    mean = _mm_sel(_mm_sel(y, hsel_ref[...]), hselt_ref[...]) * inv_hd
    yc = y - mean
    var = _mm_sel(_mm_sel(yc * yc, hsel_ref[...]), hselt_ref[...]) * inv_hd
    yn = yc * lax.rsqrt(var + RW_GN_EPS) * lnw_ref[...] + lnb_ref[...]
    yr = (yn + bv_ref[0]) * g_ref[0]
    mix = _mm(jnp.concatenate([y5_ref[0], yr], axis=-1), wo_ref[...])
    x1 = x_ref[0] + gt1_ref[0] * mix
    x1_ref[0] = x1
    f = _rms_mod(x1, g2_ref[...], sc2_ref[0], sh2_ref[0])
    f_ref[0] = f.astype(BF16)

    logits = _mm_f32(f, rw_ref[...]) + rb_ref[...]
    tm, ne = logits.shape
    eid = lax.broadcasted_iota(jnp.int32, (tm, ne), 1)
    work = logits
    sel = jnp.zeros((tm, ne), F32)
    idx_cols, val_cols = [], []
    for _ in range(TOP_K):
        top = jnp.max(work, axis=-1, keepdims=True)
        pick = jnp.min(jnp.where(work == top, eid, ne), axis=-1, keepdims=True)
        hit = eid == pick
        sel = jnp.where(hit, 1.0, sel)
        work = jnp.where(hit, -jnp.inf, work)
        idx_cols.append(pick)
        val_cols.append(top)
    exps = [jnp.exp(vk - val_cols[0]) for vk in val_cols]
    denom = exps[0] + exps[1] + exps[2] + exps[3]
    row = lax.broadcasted_iota(jnp.int32, (tm, tm), 0)
    colm = lax.broadcasted_iota(jnp.int32, (tm, tm), 1)
    before = _mm((colm < row).astype(F32), sel) + carry[0:1, :]
    rank_cols = [jnp.sum(jnp.where(eid == ic, before, 0.0), axis=-1, keepdims=True)
                 for ic in idx_cols]
    idx_ref[0] = jnp.concatenate(idx_cols, axis=-1)
    gate_ref[0] = jnp.concatenate([e / denom for e in exps], axis=-1)
    rank_ref[0] = jnp.concatenate(rank_cols, axis=-1).astype(jnp.int32)
    total = carry[0:1, :] + jnp.sum(sel, axis=0, keepdims=True)
    carry[...] = jnp.broadcast_to(total, carry.shape)
    cnt_ref[...] = jnp.broadcast_to(total, cnt_ref.shape).astype(jnp.int32)


def _outproj(x, y5, ydir, bv, g, ln_w, ln_b, hsel, hselt, wo, gt1, g2, sc2, sh2, router_w,
             router_b):
    bsz, n, d = x.shape
    rw = y5.shape[-1]
    ne = router_w.shape[-1]
    tm = ROW_TILE
    row = lambda w: pl.BlockSpec((1, tm, w), lambda b, i: (b, i, 0))
    vec = lambda w: pl.BlockSpec((1, w), lambda b, i: (0, 0))
    bvec = pl.BlockSpec((1, 1, d), lambda b, i: (b, 0, 0))
    full = lambda a: pl.BlockSpec(a.shape, lambda b, i: (0,) * a.ndim)
    return pl.pallas_call(
        _outproj_kernel,
        grid=(bsz, n // tm),
        in_specs=[row(d), row(rw),
                  pl.BlockSpec((1, 1, tm, rw), lambda b, i: (0, b, i, 0)),
                  pl.BlockSpec((1, 1, tm, rw), lambda b, i: (1, b, i, 0)),
                  row(rw), row(rw), vec(rw), vec(rw), full(hsel), full(hselt),
                  pl.BlockSpec(wo.shape, lambda b, i: (0, 0), pipeline_mode=pl.Buffered(1)),
                  bvec, vec(d), bvec, bvec, full(router_w), vec(ne)],
        out_specs=[row(d), row(d), row(TOP_K), row(TOP_K), row(TOP_K),
                   pl.BlockSpec((8, ne), lambda b, i: (0, 0))],
        out_shape=[jax.ShapeDtypeStruct((bsz, n, d), F32),
                   jax.ShapeDtypeStruct((bsz, n, d), BF16),
                   jax.ShapeDtypeStruct((bsz, n, TOP_K), jnp.int32),
                   jax.ShapeDtypeStruct((bsz, n, TOP_K), F32),
                   jax.ShapeDtypeStruct((bsz, n, TOP_K), jnp.int32),
                   jax.ShapeDtypeStruct((8, ne), jnp.int32)],
        scratch_shapes=[pltpu.VMEM((8, ne), F32)],
        compiler_params=_params("arbitrary", "arbitrary"),
        name="outproj_router",
    )(x, y5, ydir, ydir, bv, g, ln_w.reshape(1, rw), ln_b.reshape(1, rw), hsel, hselt, wo, gt1,
      g2.reshape(1, d), sc2, sh2, router_w, router_b.reshape(1, ne))


def _dispatch_kernel(idx_ref, rank_ref, start_ref, f_ref, xs_in_ref, xs_ref, sem):
    del xs_in_ref
    tm = f_ref.shape[0]

    def issue(t, _):
        for k in range(TOP_K):
            a = t * TOP_K + k
            dst = start_ref[idx_ref[a]] + rank_ref[a]
            pltpu.make_async_copy(f_ref.at[t], xs_ref.at[dst], sem).start()
        return 0

    lax.fori_loop(0, tm, issue, 0)
    for _ in range(TOP_K):
        pltpu.make_async_copy(f_ref, xs_ref.at[pl.ds(0, tm)], sem).wait()


def _dispatch(idx_flat, rank_flat, start, f3, n_slots):
    n_tok, s, lanes = f3.shape
    tm = ROW_TILE
    smem_tok = pl.BlockSpec((tm * TOP_K,), lambda i: (i,), memory_space=pltpu.SMEM)
    xs0 = jnp.zeros((n_slots, s, lanes), f3.dtype)
    return pl.pallas_call(
        _dispatch_kernel,
        grid=(n_tok // tm,),
        in_specs=[smem_tok, smem_tok,
                  pl.BlockSpec(start.shape, lambda i: (0,), memory_space=pltpu.SMEM),
                  pl.BlockSpec((tm, s, lanes), lambda i: (i, 0, 0)),
                  pl.BlockSpec(memory_space=pl.ANY)],
        out_specs=pl.BlockSpec(memory_space=pl.ANY),
        out_shape=jax.ShapeDtypeStruct((n_slots, s, lanes), f3.dtype),
        scratch_shapes=[pltpu.SemaphoreType.DMA(())],
        input_output_aliases={4: 0},
        compiler_params=_params("arbitrary"),
        name="moe_dispatch",
    )(idx_flat, rank_flat, start, f3, xs0)


def _expert_changed(be_ref, b):
    prev = be_ref[jnp.maximum(b - 1, 0)]
    return jnp.logical_or(b == 0, be_ref[b] != prev)


def _moe_gu_kernel(be_ref, nv_ref, x_ref, wg_ref, wl_ref, bg_ref, bl_ref, o_ref, wg_bf, wl_bf):
    b = pl.program_id(1)

    @pl.when(b < nv_ref[0])
    def _():
        @pl.when(_expert_changed(be_ref, b))
        def _():
            wg_bf[...] = wg_ref[0].astype(BF16)
            wl_bf[...] = wl_ref[0].astype(BF16)

        x = x_ref[...]
        glu = jnp.dot(x, wg_bf[...], preferred_element_type=F32) + bg_ref[0]
        lin = jnp.dot(x, wl_bf[...], preferred_element_type=F32) + bl_ref[0]
        glu = jnp.minimum(glu, SWIGLU_LIMIT)
        lin = jnp.clip(lin, -SWIGLU_LIMIT, SWIGLU_LIMIT)
        o_ref[...] = ((lin + 1.0) * glu * jax.nn.sigmoid(SWIGLU_ALPHA * glu)).astype(BF16)

    @pl.when(b >= nv_ref[0])
    def _():
        o_ref[...] = jnp.zeros_like(o_ref)


def _moe_gu(block_e, n_valid, xs, w_gu, b_gu):
    n_slots, d = xs.shape
    ne, _, two_de = w_gu.shape
    de = two_de // 2
    tn = 512
    nj = de // tn
    nb = n_slots // MOE_ROWS
    blk = lambda b, nv: jnp.maximum(jnp.minimum(b, nv[0] - 1), 0)
    grid_spec = pltpu.PrefetchScalarGridSpec(
        num_scalar_prefetch=2,
        grid=(nj, nb),
        in_specs=[pl.BlockSpec((MOE_ROWS, d), lambda j, b, be, nv: (blk(b, nv), 0)),
                  pl.BlockSpec((1, d, tn), lambda j, b, be, nv: (be[blk(b, nv)], 0, j)),
                  pl.BlockSpec((1, d, tn), lambda j, b, be, nv: (be[blk(b, nv)], 0, nj + j)),
                  pl.BlockSpec((1, 1, tn), lambda j, b, be, nv: (be[blk(b, nv)], 0, j)),
                  pl.BlockSpec((1, 1, tn), lambda j, b, be, nv: (be[blk(b, nv)], 0, nj + j))],
        out_specs=pl.BlockSpec((MOE_ROWS, tn), lambda j, b, be, nv: (b, j)),
        scratch_shapes=[pltpu.VMEM((d, tn), BF16)] * 2)
    return pl.pallas_call(
        _moe_gu_kernel,
        grid_spec=grid_spec,
        out_shape=jax.ShapeDtypeStruct((n_slots, de), BF16),
        compiler_params=_params("arbitrary", "arbitrary"),
        name="moe_gate_up",
    )(block_e, n_valid, xs, w_gu, w_gu, b_gu.reshape(ne, 1, two_de), b_gu.reshape(ne, 1, two_de))


def _moe_dn_kernel(be_ref, nv_ref, a_ref, w_ref, bd_ref, o_ref, w_bf):
    b = pl.program_id(1)

    @pl.when(b < nv_ref[0])
    def _():
        @pl.when(_expert_changed(be_ref, b))
        def _():
            w_bf[...] = w_ref[0].astype(BF16)

        y = jnp.dot(a_ref[...], w_bf[...], preferred_element_type=F32) + bd_ref[0]
        for s in range(o_ref.shape[1]):
            o_ref[:, s, :] = y[:, s * LANES:(s + 1) * LANES]

    @pl.when(b >= nv_ref[0])
    def _():
        o_ref[...] = jnp.zeros_like(o_ref)


def _moe_dn(block_e, n_valid, act, w_dn, b_dn):
    n_slots, de = act.shape
    ne, _, d = w_dn.shape
    tn = 1024
    nb = n_slots // MOE_ROWS
    blk = lambda b, nv: jnp.maximum(jnp.minimum(b, nv[0] - 1), 0)
    grid_spec = pltpu.PrefetchScalarGridSpec(
        num_scalar_prefetch=2,
        grid=(d // tn, nb),
        in_specs=[pl.BlockSpec((MOE_ROWS, de), lambda j, b, be, nv: (blk(b, nv), 0)),
                  pl.BlockSpec((1, de, tn), lambda j, b, be, nv: (be[blk(b, nv)], 0, j)),
                  pl.BlockSpec((1, 1, tn), lambda j, b, be, nv: (be[blk(b, nv)], 0, j))],
        out_specs=pl.BlockSpec((MOE_ROWS, tn // LANES, LANES),
                               lambda j, b, be, nv: (b, j, 0)),
        scratch_shapes=[pltpu.VMEM((de, tn), BF16)])
    return pl.pallas_call(
        _moe_dn_kernel,
        grid_spec=grid_spec,
        out_shape=jax.ShapeDtypeStruct((n_slots, d // LANES, LANES), F32),
        compiler_params=_params("arbitrary", "arbitrary"),
        name="moe_down",
    )(block_e, n_valid, act, w_dn, b_dn.reshape(ne, 1, d))


def _combine_kernel(idx_ref, rank_ref, start_ref, x1_ref, gate_ref, gt2_ref, fg_ref, y_ref,
                    o_ref, buf, sem):
    tm, d = x1_ref.shape

    def issue(t, _):
        for k in range(TOP_K):
            a = t * TOP_K + k
            src = start_ref[idx_ref[a]] + rank_ref[a]
            pltpu.make_async_copy(y_ref.at[src], buf.at[k, t], sem).start()
        return 0

    lax.fori_loop(0, tm, issue, 0)
    for k in range(TOP_K):
        pltpu.make_async_copy(y_ref.at[pl.ds(0, tm)], buf.at[k], sem).wait()
    gate = gate_ref[...]
    cols = []
    for s in range(d // LANES):
        acc = gate[:, 0:1] * buf[0, :, s, :]
        for k in range(1, TOP_K):
            acc = acc + gate[:, k:k + 1] * buf[k, :, s, :]
        cols.append(acc)
    x2 = x1_ref[...] + gt2_ref[0] * jnp.concatenate(cols, axis=-1)
    ms = jnp.mean(x2 * x2, axis=-1, keepdims=True)
    o_ref[...] = x2 * lax.rsqrt(ms + NORM_EPS) * fg_ref[...]


def _combine(idx_flat, rank_flat, start, x1, gate4, gt2, fg, y3, tiles_per_batch):
    n_tok, d = x1.shape
    _, s, lanes = y3.shape
    tm = ROW_TILE
    smem_tok = pl.BlockSpec((tm * TOP_K,), lambda i: (i,), memory_space=pltpu.SMEM)
    return pl.pallas_call(
        _combine_kernel,
        grid=(n_tok // tm,),
        in_specs=[smem_tok, smem_tok,
                  pl.BlockSpec(start.shape, lambda i: (0,), memory_space=pltpu.SMEM),
                  pl.BlockSpec((tm, d), lambda i: (i, 0)),
                  pl.BlockSpec((tm, TOP_K), lambda i: (i, 0)),
                  pl.BlockSpec((1, 1, d), lambda i: (i // tiles_per_batch, 0, 0)),
                  pl.BlockSpec((1, d), lambda i: (0, 0)),
                  pl.BlockSpec(memory_space=pl.ANY)],
        out_specs=pl.BlockSpec((tm, d), lambda i: (i, 0)),
        out_shape=jax.ShapeDtypeStruct((n_tok, d), F32),
        scratch_shapes=[pltpu.VMEM((TOP_K, tm, s, lanes), F32), pltpu.SemaphoreType.DMA(())],
        compiler_params=_params("arbitrary"),
        name="moe_combine_final",
    )(idx_flat, rank_flat, start, x1, gate4, gt2, fg.reshape(1, d), y3)


def _block_diag2(w):
    z = jnp.zeros_like(w[0])
    return jnp.concatenate([jnp.concatenate([w[0], z], 1), jnp.concatenate([z, w[1]], 1)], 0)


def kernel(x, c, ctx, c_ctx, mod_w, mod_b, norm1_g, w_in, s5_a_re, s5_a_im, s5_log_dt, s5_b_re,
           s5_b_im, s5_c_re, s5_c_im, s5_d, s5_glu_w, s5_glu_b, rw_mu, rw_w0, rw_w2, rw_a0, rw_a2,
           rw_g2, rw_k_k, rw_k_a, rw_r_k, rw_ln_w, rw_ln_b, w_out, norm2_g, router_w, router_b,
           exp_w_gu, exp_b_gu, exp_w_dn, exp_b_dn, final_g):
    assert mod_w.shape[0] == 1, "single-layer stack only"
    bsz, n_lat, d = x.shape
    n_ctx = ctx.shape[1]
    assert bsz == 2 and n_ctx % ROW_TILE == 0 and n_lat % ROW_TILE == 0
    s5w = s5_d.shape[-1]
    rww = rw_k_k.shape[-1]
    n_heads = rww // RW_HEAD
    ne = router_w.shape[-1]

    cond8 = jnp.zeros((8, d), F32).at[:bsz].set(c).at[bsz].set(c_ctx)
    mod = _adaln(cond8, mod_w[0], mod_b[0])
    sh1, sc1, gt1, sh2, sc2, gt2 = [m[:bsz, None, :] for m in jnp.split(mod, 6, axis=-1)]
    csh1, csc1 = [jnp.broadcast_to(m[bsz][None, None, :], (bsz, 1, d))
                  for m in jnp.split(mod, 6, axis=-1)[:2]]

    wu = w_in[0][:, :s5w].astype(BF16)
    wz = w_in[0][:, s5w:].astype(BF16)
    u_lat, z_lat = _inproj(x, norm1_g[0], sc1, sh1, wu, wz)
    u_ctx, z_ctx = _inproj(ctx, norm1_g[0], csc1, csh1, wu, wz)

    s5c = _s5_constants(s5_a_re[0], s5_a_im[0], s5_log_dt[0], s5_b_re[0], s5_b_im[0], s5_c_re[0],
                        s5_c_im[0], s5_d[0], s5_glu_w[0], s5_glu_b[0])
    y5 = _s5(jnp.concatenate([u_ctx, u_lat], axis=1), s5c, n_ctx)

    lanes_idx = jnp.arange(rww) // RW_HEAD
    hsel = (lanes_idx[:, None] == jnp.arange(LANES)[None, :]).astype(BF16)
    hselt = hsel.T
    feat_consts = (rw_mu[0], rw_w0[0].reshape(1, N_DIR * rww),
                   _block_diag2(rw_w2[0]).astype(BF16), rw_a0[0].reshape(1, N_DIR * rww),
                   _block_diag2(rw_a2[0]).astype(BF16), rw_g2[0].astype(BF16),
                   rw_k_k[0].reshape(1, rww), rw_k_a[0].reshape(1, rww),
                   rw_r_k[0].reshape(1, rww), hsel, hselt)
    fc = _rw_features(z_ctx, n_ctx, False, *feat_consts)
    fl = _rw_features(z_lat, GRID_W, True, *feat_consts)

    def scan_inputs(f):
        r, v, kh, _, _, lw, q, kt = f
        return r, v, kh, lw, q, kt

    gc, hc, rhc, y0c = _rw_chunks(*scan_inputs(fc))
    gl, hl, rhl, y0l = _rw_chunks(*scan_inputs(fl))
    s_zero = jnp.zeros((N_DIR, bsz, RW_HEAD, rww), F32)
    (s_ctx,) = _rw_state(s_zero, gc, hc, rhc, y0c, emit_y=False)
    y_dir, _ = _rw_state(s_ctx, gl, hl, rhl, y0l, emit_y=True)

    x1, f, idx4, gate4, rank4, counts = _outproj(
        x, y5, y_dir, fl[4], fl[3], rw_ln_w[0], rw_ln_b[0], hsel, hselt, w_out[0].astype(BF16),
        gt1, norm2_g[0], sc2, sh2, router_w[0], router_b[0])

    n_tok = bsz * n_lat
    cnt = counts[0]
    padded = (cnt + MOE_ROWS - 1) // MOE_ROWS * MOE_ROWS
    pend = jnp.cumsum(padded)
    start = (pend - padded).astype(jnp.int32)
    nb = n_tok * TOP_K // MOE_ROWS + ne
    n_slots = nb * MOE_ROWS
    block_e = jnp.minimum(
        jnp.sum(pend[None, :] <= (jnp.arange(nb) * MOE_ROWS)[:, None], axis=1), ne - 1
    ).astype(jnp.int32)
    n_valid = (pend[-1] // MOE_ROWS).astype(jnp.int32).reshape(1)
    idx_flat = idx4.reshape(-1)
    rank_flat = rank4.reshape(-1)

    s = d // LANES
    xs3 = _dispatch(idx_flat, rank_flat, start, f.reshape(n_tok, s, LANES), n_slots)
    act = _moe_gu(block_e, n_valid, xs3.reshape(n_slots, d), exp_w_gu[0], exp_b_gu[0])
    y3 = _moe_dn(block_e, n_valid, act, exp_w_dn[0], exp_b_dn[0])
    out = _combine(idx_flat, rank_flat, start, x1.reshape(n_tok, d), gate4.reshape(n_tok, TOP_K),
                   gt2, final_g, y3, n_lat // ROW_TILE)
    return out.reshape(bsz, n_lat, d)
```

```python
import functools
import math

import jax
import jax.numpy as jnp
from jax import lax
from jax.experimental import pallas as pl
from jax.experimental.pallas import tpu as pltpu

F32 = jnp.float32
BF16 = jnp.bfloat16
HIGHEST = lax.Precision.HIGHEST

LANES = 128
VMEM_LIMIT_BYTES = 56 * 1024 * 1024

NORM_EPS = 1e-5
N_DIR = 2
S5_GROUP = 16
S5_STATE = 64
S5_CHUNK = 8
S5_LANE_GROUPS = LANES // S5_GROUP
S5_SCAN_ROWS = 8
RW_HEAD = 64
RW_CHUNK = 64
RW_GN_EPS = 64e-5
GRID_W = 64
TOP_K = 4
SWIGLU_ALPHA = 1.702
SWIGLU_LIMIT = 7.0
ROW_TILE = 256
MOE_ROWS = 256
MOE_GROUP_ROWS = 1536
MOE_HIDDEN_TILE = 256


def _params(*sem):
    return pltpu.CompilerParams(dimension_semantics=sem, vmem_limit_bytes=VMEM_LIMIT_BYTES)


def _mm(a, b, dims=((1,), (0,))):
    return lax.dot_general(a.astype(BF16), b.astype(BF16), (dims, ((), ())),
                           preferred_element_type=F32)


def _split_bf16(a):
    hi = a.astype(BF16)
    return hi, (a - hi.astype(F32)).astype(BF16)


def _mm_sel(a, sel):
    hi, lo = _split_bf16(a)
    m = a.shape[0]
    out = jnp.dot(jnp.concatenate([hi, lo], axis=0), sel.astype(BF16), preferred_element_type=F32)
    return out[:m] + out[m:]


def _sel_mm(sel, a):
    hi, lo = _split_bf16(a)
    n = a.shape[1]
    out = jnp.dot(sel.astype(BF16), jnp.concatenate([hi, lo], axis=1), preferred_element_type=F32)
    return out[:, :n] + out[:, n:]


def _mm_f32(a, b, dims=((1,), (0,))):
    return lax.dot_general(a, b, (dims, ((), ())), precision=HIGHEST,
                           preferred_element_type=F32)


def _adaln_kernel(cond_ref, w_ref, b_ref, o_ref):
    c = cond_ref[...]
    o_ref[...] = _mm_f32(c * jax.nn.sigmoid(c), w_ref[...]) + b_ref[...]


def _adaln(cond8, w, b):
    d, n = w.shape
    tn = 1536
    return pl.pallas_call(
        _adaln_kernel,
        grid=(n // tn,),
        in_specs=[pl.BlockSpec((8, d), lambda j: (0, 0)),
                  pl.BlockSpec((d, tn), lambda j: (0, j)),
                  pl.BlockSpec((1, tn), lambda j: (0, j))],
        out_specs=pl.BlockSpec((8, tn), lambda j: (0, j)),
        out_shape=jax.ShapeDtypeStruct((8, n), F32),
        compiler_params=_params("parallel"),
        name="adaln",
    )(cond8, w, b.reshape(1, n))


def _rms_mod(x, g, sc, sh):
    y = x * lax.rsqrt(jnp.mean(x * x, axis=-1, keepdims=True) + NORM_EPS)
    return (y * g) * (1.0 + sc) + sh


def _inproj_kernel(x_ref, g_ref, sc_ref, sh_ref, wu_ref, wz_ref, u_ref, z_ref):
    h = _rms_mod(x_ref[0], g_ref[...], sc_ref[0], sh_ref[0]).astype(BF16)
    u_ref[0] = jnp.dot(h, wu_ref[...], preferred_element_type=F32)
    z_ref[0] = jnp.dot(h, wz_ref[...], preferred_element_type=F32)


def _inproj(x, g, sc, sh, wu, wz):
    bsz, n, d = x.shape
    nu, nz = wu.shape[1], wz.shape[1]
    tm = ROW_TILE
    const = dict(pipeline_mode=pl.Buffered(1))
    return pl.pallas_call(
        _inproj_kernel,
        grid=(bsz, n // tm),
        in_specs=[pl.BlockSpec((1, tm, d), lambda b, i: (b, i, 0)),
                  pl.BlockSpec((1, d), lambda b, i: (0, 0)),
                  pl.BlockSpec((1, 1, d), lambda b, i: (b, 0, 0)),
                  pl.BlockSpec((1, 1, d), lambda b, i: (b, 0, 0)),
                  pl.BlockSpec((d, nu), lambda b, i: (0, 0), **const),
                  pl.BlockSpec((d, nz), lambda b, i: (0, 0), **const)],
        out_specs=[pl.BlockSpec((1, tm, nu), lambda b, i: (b, i, 0)),
                   pl.BlockSpec((1, tm, nz), lambda b, i: (b, i, 0))],
        out_shape=[jax.ShapeDtypeStruct((bsz, n, nu), F32),
                   jax.ShapeDtypeStruct((bsz, n, nz), F32)],
        compiler_params=_params("parallel", "parallel"),
        name="inproj",
    )(x, g.reshape(1, d), sc, sh, wu, wz)


def _s5_constants(a_re, a_im, log_dt, b_re, b_im, c_re, c_im, d_skip, glu_w, glu_b):
    t = S5_CHUNK
    g, p = a_re.shape[1], a_re.shape[2]
    h = S5_GROUP
    gl = S5_LANE_GROUPS
    no = g // gl
    n = jnp.arange(t + 1, dtype=F32)[None, :, None]
    f_parts, e_parts, lam_rows, lags = {}, {}, [], []
    for d in range(N_DIR):
        ar, ai = a_re[d].astype(F32)[:, None, :], a_im[d].astype(F32)[:, None, :]
        dt = jnp.exp(log_dt[d].astype(F32))[:, None, None]
        mag = jnp.exp(n * (ar * dt))
        pr, pi = mag * jnp.cos(n * (ai * dt)), mag * jnp.sin(n * (ai * dt))
        lr, li = pr[:, 1], pi[:, 1]
        ar, ai = ar[:, 0], ai[:, 0]
        den = ar * ar + ai * ai
        cf_re = ((lr - 1.0) * ar + li * ai) / den
        cf_im = (li * ar - (lr - 1.0) * ai) / den
        br = jnp.swapaxes(b_re[d].astype(F32), 1, 2)
        bi = jnp.swapaxes(b_im[d].astype(F32), 1, 2)
        bb_re = cf_re[:, None] * br - cf_im[:, None] * bi
        bb_im = cf_re[:, None] * bi + cf_im[:, None] * br
        cr, ci = c_re[d].astype(F32)[:, None], c_im[d].astype(F32)[:, None]
        cl_re = cr * pr[:, :, None, :] - ci * pi[:, :, None, :]
        cl_im = cr * pi[:, :, None, :] + ci * pr[:, :, None, :]
        dims = (((2,), (2,)), ((0,), (0,)))
        lag = (lax.dot_general(cl_re[:, :t].reshape(g, t * h, p), bb_re, dims, precision=HIGHEST)
               - lax.dot_general(cl_im[:, :t].reshape(g, t * h, p), bb_im, dims, precision=HIGHEST))
        lags.append(lag.reshape(g, t, h, h))
        pw = jnp.arange(t - 1, -1, -1) if d == 0 else jnp.arange(t)
        ps_re, ps_im = pr[:, pw][:, :, None, :], pi[:, pw][:, :, None, :]
        f_parts[0, d] = ps_re * bb_re[:, None] - ps_im * bb_im[:, None]
        f_parts[1, d] = ps_re * bb_im[:, None] + ps_im * bb_re[:, None]
        pe = jnp.arange(1, t + 1) if d == 0 else jnp.arange(t, 0, -1)
        e_parts[0, d] = jnp.transpose(cl_re[:, pe], (0, 3, 1, 2))
        e_parts[1, d] = -jnp.transpose(cl_im[:, pe], (0, 3, 1, 2))
        lam_rows += [pr[:, t], pi[:, t]]
    parts = [(q, d) for q in range(2) for d in range(N_DIR)]
    fg = jnp.stack([f_parts[k] for k in parts], axis=3).reshape(no, gl, t, h, 4 * p)
    f_rows = jnp.transpose(fg, (0, 2, 1, 3, 4)).reshape(no, t * gl * h, 4 * p)
    eg = jnp.stack([e_parts[k] for k in parts], axis=1).reshape(no, gl, 4, p, t * h)
    e_rows = jnp.transpose(eg, (0, 2, 1, 3, 4)).reshape(no, 4 * gl * p, t * h)
    s_idx = jnp.arange(t)[:, None]
    t_idx = jnp.arange(t)[None, :]
    kf = jnp.where((t_idx >= s_idx)[None, :, :, None, None],
                   lags[0][:, jnp.clip(t_idx - s_idx, 0, t - 1)], 0.0)
    kb = jnp.where((s_idx >= t_idx)[None, :, :, None, None],
                   lags[1][:, jnp.clip(s_idx - t_idx, 0, t - 1)], 0.0)
    kg = jnp.transpose(kf + kb, (0, 1, 4, 2, 3)).reshape(no, gl, t, h, t * h)
    k_rows = jnp.transpose(kg, (0, 2, 1, 3, 4)).reshape(no, t * gl * h, t * h)
    fmat, emat, kmat = _s5_expand(f_rows, e_rows, k_rows)
    lam = jnp.stack(lam_rows, axis=1)
    lam = jnp.transpose(lam.reshape(no, gl, 4, p), (0, 2, 1, 3)).reshape(no, 4, gl * p)
    gmat = jnp.einsum('ab,oahk->oahbk', jnp.eye(gl, dtype=F32),
                      glu_w.astype(F32).reshape(no, gl, h, h)).reshape(no, gl * h, gl * h)
    dvec = d_skip.astype(F32).reshape(no, 1, gl * h)
    bvec = glu_b.astype(F32).reshape(no, 1, gl * h)
    return fmat, emat, kmat, lam, dvec, bvec, gmat.astype(BF16)


def _s5_expand_kernel(f_ref, e_ref, k_ref, rep_s_ref, rep_t_ref, fo_ref, eo_ref, ko_ref):
    def expand(x, rep, row_div, col_div):
        y = jnp.dot(x.astype(BF16), rep, preferred_element_type=F32)
        row = lax.broadcasted_iota(jnp.int32, y.shape, 0)
        col = lax.broadcasted_iota(jnp.int32, y.shape, 1)
        same = (row // row_div) % S5_LANE_GROUPS == (col // col_div) % S5_LANE_GROUPS
        return jnp.where(same, y, 0.0).astype(BF16)

    fo_ref[0] = expand(f_ref[0], rep_s_ref[...], S5_GROUP, S5_STATE)
    eo_ref[0] = expand(e_ref[0], rep_t_ref[...], S5_STATE, S5_GROUP)
    ko_ref[0] = expand(k_ref[0], rep_t_ref[...], S5_GROUP, S5_GROUP)


def _s5_expand(f_rows, e_rows, k_rows):
    no, rows_f, cols_f = f_rows.shape
    _, rows_e, cols_e = e_rows.shape
    gl = S5_LANE_GROUPS
    c = jnp.arange(cols_f * gl)
    rep_s = (jnp.arange(cols_f)[:, None]
             == (c // (gl * S5_STATE)) * S5_STATE + c % S5_STATE).astype(BF16)
    c = jnp.arange(cols_e * gl)
    rep_t = (jnp.arange(cols_e)[:, None]
             == (c // (gl * S5_GROUP)) * S5_GROUP + c % S5_GROUP).astype(BF16)
    blk = lambda r, cc: pl.BlockSpec((1, r, cc), lambda o: (o, 0, 0))
    full = lambda a: pl.BlockSpec(a.shape, lambda o: (0, 0))
    return pl.pallas_call(
        _s5_expand_kernel,
        grid=(no,),
        in_specs=[blk(rows_f, cols_f), blk(rows_e, cols_e), blk(rows_f, cols_e),
                  full(rep_s), full(rep_t)],
        out_specs=[blk(rows_f, cols_f * gl), blk(rows_e, cols_e * gl), blk(rows_f, cols_e * gl)],
        out_shape=[jax.ShapeDtypeStruct((no, rows_f, cols_f * gl), BF16),
                   jax.ShapeDtypeStruct((no, rows_e, cols_e * gl), BF16),
                   jax.ShapeDtypeStruct((no, rows_f, cols_e * gl), BF16)],
        compiler_params=_params("parallel"),
        name="s5_expand",
    )(f_rows, e_rows, k_rows, rep_s, rep_t)


def _s5_chunk_rows(u_ref, b, start, n_chunks):
    t = S5_CHUNK
    return jnp.concatenate(
        [u_ref[b, pl.ds(start + s, n_chunks, stride=t), :] for s in range(t)], axis=-1)


def _s5_in_kernel(u_ref, f_ref, z_ref):
    bsz, n_tok, _ = u_ref.shape
    for b in range(bsz):
        z_ref[b] = _mm(_s5_chunk_rows(u_ref, b, 0, n_tok // S5_CHUNK), f_ref[0])


def _s5_scan_kernel(n_ctx_chunks, z_ref, lam_ref, hs_ref):
    n_chunks = z_ref.shape[1]
    q = z_ref.shape[2] // 4
    rows = S5_SCAN_ROWS
    n_blocks, ctx_blocks = n_chunks // rows, n_ctx_chunks // rows
    lam = lam_ref[0]
    lf_re, lf_im, lb_re, lb_im = lam[0:1], lam[1:2], lam[2:3], lam[3:4]

    def block(k, carry):
        f_re, f_im, b_re, b_im = carry
        kb = jnp.where(k < ctx_blocks, ctx_blocks - 1 - k, n_blocks - 1 + ctx_blocks - k)
        rf = pl.multiple_of(k * rows, rows)
        rb = pl.multiple_of(kb * rows, rows)
        zf = z_ref[0, pl.ds(rf, rows), :]
        zb = z_ref[0, pl.ds(rb, rows), :]
        ent = [[], [], [], []]
        for j in range(rows):
            jb = rows - 1 - j
            for lst, val in zip(ent, (f_re, f_im, b_re, b_im)):
                lst.append(val)
            f_re, f_im = (lf_re * f_re - lf_im * f_im + zf[j:j + 1, 0:q],
                          lf_re * f_im + lf_im * f_re + zf[j:j + 1, 2 * q:3 * q])
            b_re, b_im = (lb_re * b_re - lb_im * b_im + zb[jb:jb + 1, q:2 * q],
                          lb_re * b_im + lb_im * b_re + zb[jb:jb + 1, 3 * q:4 * q])
        hs_ref[0, pl.ds(rf, rows), 0:q] = jnp.concatenate(ent[0], axis=0)
        hs_ref[0, pl.ds(rf, rows), 2 * q:3 * q] = jnp.concatenate(ent[1], axis=0)
        hs_ref[0, pl.ds(rb, rows), q:2 * q] = jnp.concatenate(ent[2][::-1], axis=0)
        hs_ref[0, pl.ds(rb, rows), 3 * q:4 * q] = jnp.concatenate(ent[3][::-1], axis=0)
        return f_re, f_im, b_re, b_im

    zero = jnp.zeros((1, q), F32)
    lax.fori_loop(0, n_blocks, block, (zero, zero, zero, zero))


def _s5_out_kernel(n_ctx, u_ref, hs_ref, e_ref, k_ref, g_ref, d_ref, b_ref, o_ref, y_scr):
    t = S5_CHUNK
    bsz, n_tok, lanes = u_ref.shape
    n_chunks = (n_tok - n_ctx) // t
    for b in range(bsz):
        x = _s5_chunk_rows(u_ref, b, n_ctx, n_chunks)
        y = _mm(x, k_ref[0]) + _mm(hs_ref[b, n_ctx // t:, :], e_ref[0])
        for s in range(t):
            y_scr[pl.ds(s, n_chunks, stride=t), :] = y[:, s * lanes:(s + 1) * lanes]
        y = jax.nn.gelu(y_scr[...] + d_ref[0] * u_ref[b, n_ctx:, :])
        gate = _mm(y, g_ref[0]) + b_ref[0]
        o_ref[b] = y * jax.nn.sigmoid(gate)


def _s5(u_all, consts, n_ctx):
    fmat, emat, kmat, lam, dvec, bvec, gmat = consts
    bsz, n_tok, width = u_all.shape
    t = S5_CHUNK
    n_chunks = n_tok // t
    no, rows_k, cols_f = fmat.shape
    assert n_ctx % (t * S5_SCAN_ROWS) == 0 and n_tok % (t * S5_SCAN_ROWS) == 0
    tok = pl.BlockSpec((1, n_tok, LANES), lambda o, b: (b, 0, o))
    state = pl.BlockSpec((1, n_chunks, cols_f), lambda o, b: (b, 0, o))
    full = lambda a: pl.BlockSpec((1,) + a.shape[1:], lambda o, b: (o, 0, 0))
    z = pl.pallas_call(
        _s5_in_kernel,
        grid=(no, bsz),
        in_specs=[tok, full(fmat)],
        out_specs=state,
        out_shape=jax.ShapeDtypeStruct((bsz, n_chunks, no * cols_f), F32),
        compiler_params=_params("parallel", "parallel"),
        name="s5_in",
    )(u_all, fmat)
    hs = pl.pallas_call(
        functools.partial(_s5_scan_kernel, n_ctx // t),
        grid=(no, bsz),
        in_specs=[state, full(lam)],
        out_specs=state,
        out_shape=jax.ShapeDtypeStruct((bsz, n_chunks, no * cols_f), F32),
        compiler_params=_params("parallel", "parallel"),
        name="s5_scan",
    )(z, lam)
    n_lat = n_tok - n_ctx
    return pl.pallas_call(
        functools.partial(_s5_out_kernel, n_ctx),
        grid=(no, bsz),
        in_specs=[tok, state, full(emat), full(kmat), full(gmat), full(dvec), full(bvec)],
        out_specs=pl.BlockSpec((1, n_lat, LANES), lambda o, b: (b, 0, o)),
        out_shape=jax.ShapeDtypeStruct((bsz, n_lat, width), F32),
        scratch_shapes=[pltpu.VMEM((n_lat, LANES), F32)],
        compiler_params=_params("parallel", "parallel"),
        name="s5_out",
    )(u_all, hs, emat, kmat, gmat, dvec, bvec)


def _rw_feat_kernel(width, has_vert, n_tiles, z_ref, zp_ref, zn_ref, mu_ref, w0_ref, w2_ref,
                    a0_ref, a2_ref, g2_ref, kk_ref, ka_ref, rk_ref, hsel_ref, hselt_ref,
                    r_ref, v_ref, kh_ref, g_ref, bv_ref, lw_ref, q_ref, kt_ref):
    z = z_ref[0]
    t0, cz = z.shape
    rw = r_ref.shape[-1]
    mu = mu_ref[...]
    tok = lax.broadcasted_iota(jnp.int32, (t0, 1), 0)
    col = tok % width
    left = pltpu.roll(z, 1, 0)
    right = pltpu.roll(z, t0 - 1, 0)
    out = z + jnp.where(col != 0, mu[0:1] * (left - z), 0.0)
    out = out + jnp.where(col != width - 1, mu[1:2] * (right - z), 0.0)
    if has_vert:
        i = pl.program_id(1)
        up = jnp.concatenate([zp_ref[0], z[:t0 - width]], axis=0)
        down = jnp.concatenate([z[width:], zn_ref[0]], axis=0)
        up_ok = jnp.logical_or(i > 0, tok >= width)
        down_ok = jnp.logical_or(i < n_tiles - 1, tok < t0 - width)
        out = out + jnp.where(up_ok, mu[2:3] * (up - z), 0.0)
        out = out + jnp.where(down_ok, mu[3:4] * (down - z), 0.0)
    r = out[:, 0:rw]
    k = out[:, rw:2 * rw]
    v = out[:, 2 * rw:3 * rw]
    o = 3 * rw
    lora = w2_ref.shape[0]
    xw = out[:, o:o + lora]
    xa = out[:, o + lora:o + 2 * lora]
    xg = out[:, o + 2 * lora:]
    dec = w0_ref[...] + _mm(jnp.tanh(xw), w2_ref[...])
    lw = -math.exp(-0.5) * jax.nn.sigmoid(dec)
    a = jax.nn.sigmoid(a0_ref[...] + _mm(xa, a2_ref[...]))
    g = _mm(jax.nn.sigmoid(xg), g2_ref[...])
    kk = k * kk_ref[...]
    ssq = _mm_sel(kk * kk, hsel_ref[...])
    inv = 1.0 / jnp.maximum(jnp.sqrt(ssq), 1e-12)
    kh = kk * _mm_sel(inv, hselt_ref[...])
    ka = ka_ref[...]
    kt_sum = jnp.zeros_like(k)
    for d in range(N_DIR):
        a_d = a[:, d * rw:(d + 1) * rw]
        kt_d = k * (1.0 + (a_d - 1.0) * ka)
        kt_sum = kt_sum + kt_d
        lw_ref[d, 0] = lw[:, d * rw:(d + 1) * rw]
        q_ref[d, 0] = a_d * kh
        kt_ref[d, 0] = kt_d
    bonus = _mm_sel(_mm_sel(r * kt_sum * rk_ref[...], hsel_ref[...]), hselt_ref[...])
    r_ref[0] = r
    v_ref[0] = v
    kh_ref[0] = kh
    g_ref[0] = g
    bv_ref[0] = bonus * v


def _rw_features(z, width, has_vert, mu, w0, w2blk, a0, a2blk, g2, k_k, k_a, r_k, hsel, hselt):
    bsz, n, cz = z.shape
    rw = k_k.shape[-1]
    t0 = ROW_TILE
    n_tiles = n // t0
    per = t0 // GRID_W
    nblk = n // GRID_W
    full = lambda a: pl.BlockSpec(a.shape, lambda b, i: (0,) * a.ndim)
    tok = pl.BlockSpec((1, t0, rw), lambda b, i: (b, i, 0))
    dtok = pl.BlockSpec((N_DIR, 1, t0, rw), lambda b, i: (0, b, i, 0))
    consts = (mu, w0, w2blk, a0, a2blk, g2, k_k, k_a, r_k, hsel, hselt)
    return pl.pallas_call(
        functools.partial(_rw_feat_kernel, width, has_vert, n_tiles),
        grid=(bsz, n_tiles),
        in_specs=[pl.BlockSpec((1, t0, cz), lambda b, i: (b, i, 0)),
                  pl.BlockSpec((1, GRID_W, cz), lambda b, i: (b, jnp.maximum(i * per - 1, 0), 0)),
                  pl.BlockSpec((1, GRID_W, cz),
                               lambda b, i: (b, jnp.minimum(i * per + per, nblk - 1), 0))]
        + [full(a) for a in consts],
        out_specs=[tok] * 5 + [dtok] * 3,
        out_shape=[jax.ShapeDtypeStruct((bsz, n, rw), F32)] * 5
        + [jax.ShapeDtypeStruct((N_DIR, bsz, n, rw), F32)] * 3,
        compiler_params=_params("parallel", "parallel"),
        name="rw_features",
    )(z, z, z, *consts)


def _rw_chunk_kernel(r_ref, v_ref, kh_ref, lw_ref, q_ref, kt_ref, g_ref, h_ref, rh_ref, y0_ref):
    rev = pl.program_id(0) == 1
    n = RW_CHUNK
    hd = RW_HEAD
    lw = lw_ref[0, 0]
    row = lax.broadcasted_iota(jnp.int32, (n, n), 0)
    col = lax.broadcasted_iota(jnp.int32, (n, n), 1)
    ahead = (row - col) * jnp.where(rev, -1, 1)
    incl = (ahead >= 0).astype(F32)
    strict = (ahead > 0).astype(F32)
    eye = (row == col).astype(F32)
    same_block = [(jnp.right_shift(row, s) == jnp.right_shift(col, s)).astype(F32)
                  for s in range(3, n.bit_length())]
    b_incl = _sel_mm(incl, lw)
    btot = jnp.sum(lw, axis=0, keepdims=True)
    e_neg = jnp.exp(-b_incl)
    e_rem = jnp.exp(btot - b_incl)
    q, kt = q_ref[0, 0], kt_ref[0, 0]
    pt_all = kh_ref[0] * jnp.exp(b_incl - lw)
    rt_all = r_ref[0] * jnp.exp(b_incl)
    qt_all = q * e_neg
    ktt_all = kt * e_neg
    qh_all = q * e_rem
    kth_all = kt * e_rem
    gam_all = jnp.exp(btot)
    v_all = v_ref[0]
    heads = range(lw.shape[-1] // hd)
    sls = [slice(h * hd, (h + 1) * hd) for h in heads]
    pt = [pt_all[:, s] for s in sls]
    rt = [rt_all[:, s] for s in sls]
    v = [v_all[:, s] for s in sls]
    a4 = [_mm(jnp.concatenate([pt[h], rt[h]], 0),
              jnp.concatenate([qt_all[:, sls[h]], ktt_all[:, sls[h]]], 0), ((1,), (1,)))
          for h in heads]
    nmat = [strict * a4[h][:n, :n] for h in heads]
    akv = [_mm(strict * a4[h][:n, n:], v[h]) for h in heads]
    nd = [same_block[0] * nmat[h] for h in heads]
    x = [_mm(nd[h], nd[h]) for h in heads]
    m = [eye - nd[h] for h in heads]
    m = [m[h] + _mm(m[h], x[h]) for h in heads]
    x = [_mm(x[h], x[h]) for h in heads]
    m = [m[h] + _mm(m[h], x[h]) for h in heads]
    for lvl in range(1, len(same_block)):
        ring = same_block[lvl] - same_block[lvl - 1]
        t = [_mm(m[h], ring * nmat[h]) for h in heads]
        m = [m[h] - _mm(t[h], m[h]) for h in heads]
    wu = [_mm(m[h], jnp.concatenate([pt[h], akv[h]], 1)) for h in heads]
    gh = [_mm(wu[h], qh_all[:, sls[h]], ((0,), (0,))) for h in heads]
    vk = [_mm(v[h], kth_all[:, sls[h]], ((0,), (0,))) for h in heads]
    lwu = [_mm(incl * a4[h][n:, :n], wu[h]) for h in heads]
    lv = [_mm(incl * a4[h][n:, n:], v[h]) for h in heads]
    for h in heads:
        g_ref[0, 0, 0, :, sls[h]] = eye * gam_all[:, sls[h]] - gh[h][:hd]
        h_ref[0, 0, 0, :, sls[h]] = vk[h] - gh[h][hd:]
        rh_ref[0, 0, :, sls[h]] = rt[h] - lwu[h][:, :hd]
        y0_ref[0, 0, :, sls[h]] = lv[h] - lwu[h][:, hd:]


def _rw_chunks(r, v, kh, lw, q, kt):
    bsz, n, rw = r.shape
    nc = n // RW_CHUNK
    tok = pl.BlockSpec((1, RW_CHUNK, rw), lambda d, b, c: (b, c, 0))
    dtok = pl.BlockSpec((1, 1, RW_CHUNK, rw), lambda d, b, c: (d, b, c, 0))
    mat = pl.BlockSpec((1, 1, 1, RW_HEAD, rw), lambda d, b, c: (d, b, c, 0, 0))
    return pl.pallas_call(
        _rw_chunk_kernel,
        grid=(N_DIR, bsz, nc),
        in_specs=[tok, tok, tok, dtok, dtok, dtok],
        out_specs=[mat, mat, dtok, dtok],
        out_shape=[jax.ShapeDtypeStruct((N_DIR, bsz, nc, RW_HEAD, rw), F32)] * 2
        + [jax.ShapeDtypeStruct((N_DIR, bsz, n, rw), F32)] * 2,
        compiler_params=_params("parallel", "parallel", "parallel"),
        name="rw_chunks",
    )(r, v, kh, lw, q, kt)


def _rw_state_kernel(emit_y, s0_ref, g_ref, h_ref, rh_ref, y0_ref, *rest):
    if emit_y:
        y_ref, sfin_ref, s_scr = rest
    else:
        sfin_ref, s_scr = rest
    c = pl.program_id(2)
    hd = RW_HEAD

    @pl.when(c == 0)
    def _():
        s_scr[...] = s0_ref[0, 0]

    sls = [slice(h * hd, (h + 1) * hd) for h in range(s_scr.shape[-1] // hd)]
    s_all = s_scr[...].astype(BF16)
    s = [s_all[:, sl] for sl in sls]
    g_all = g_ref[0, 0, 0].astype(BF16)
    s_new = [_mm(s[h], g_all[:, sl]) for h, sl in enumerate(sls)]
    if emit_y:
        rh_all = rh_ref[0, 0].astype(BF16)
        y = [_mm(rh_all[:, sl], s[h], ((1,), (1,))) for h, sl in enumerate(sls)]
        y_ref[0, 0] = jnp.concatenate(y, axis=-1) + y0_ref[0, 0]
    s_scr[...] = jnp.concatenate(s_new, axis=-1) + h_ref[0, 0, 0]

    @pl.when(c == pl.num_programs(2) - 1)
    def _():
        sfin_ref[0, 0] = s_scr[...]


def _rw_state(s0, gmat, hmat, rh, y0, emit_y):
    n_dir, bsz, nc, hd, rw = gmat.shape
    n = rh.shape[2]
    order = lambda d, c: c + d * (nc - 1 - 2 * c)
    mat = pl.BlockSpec((1, 1, 1, hd, rw), lambda d, b, c: (d, b, order(d, c), 0, 0))
    dtok = pl.BlockSpec((1, 1, RW_CHUNK, rw), lambda d, b, c: (d, b, order(d, c), 0))
    st = pl.BlockSpec((1, 1, hd, rw), lambda d, b, c: (d, b, 0, 0))
    out_specs = [st]
    out_shape = [jax.ShapeDtypeStruct((n_dir, bsz, hd, rw), F32)]
    if emit_y:
        out_specs = [dtok] + out_specs
        out_shape = [jax.ShapeDtypeStruct((n_dir, bsz, n, rw), F32)] + out_shape
    return pl.pallas_call(
        functools.partial(_rw_state_kernel, emit_y),
        grid=(n_dir, bsz, nc),
        in_specs=[st, mat, mat, dtok, dtok],
        out_specs=out_specs,
        out_shape=out_shape,
        scratch_shapes=[pltpu.VMEM((hd, rw), F32)],
        compiler_params=_params("parallel", "parallel", "arbitrary"),
        name="rw_state_y" if emit_y else "rw_state",
    )(s0, gmat, hmat, rh, y0)


def _outproj_kernel(x_ref, y5_ref, yf_ref, yb_ref, bv_ref, g_ref, lnw_ref, lnb_ref, hsel_ref,
                    hselt_ref, wo_ref, gt1_ref, g2_ref, sc2_ref, sh2_ref, rw_ref, rb_ref,
                    x1_ref, f_ref, idx_ref, gate_ref, rank_ref, cnt_ref, carry):
    first = jnp.logical_and(pl.program_id(0) == 0, pl.program_id(1) == 0)

    @pl.when(first)
    def _():
        carry[...] = jnp.zeros_like(carry)

    inv_hd = 1.0 / RW_HEAD
    y = yf_ref[0, 0] + yb_ref[0, 0]
    mean = _mm_sel(_mm_sel(y, hsel_ref[...]), hselt_ref[...]) * inv_hd
    yc = y - mean
    var = _mm_sel(_mm_sel(yc * yc, hsel_ref[...]), hselt_ref[...]) * inv_hd
    yn = yc * lax.rsqrt(var + RW_GN_EPS) * lnw_ref[...] + lnb_ref[...]
    yr = (yn + bv_ref[0]) * g_ref[0]
    mix = _mm(jnp.concatenate([y5_ref[0], yr], axis=-1), wo_ref[...])
    x1 = x_ref[0] + gt1_ref[0] * mix
    x1_ref[0] = x1
    f = _rms_mod(x1, g2_ref[...], sc2_ref[0], sh2_ref[0])
    n_lines = f.shape[1] // LANES
    for s in range(n_lines):
        f_ref[0, pl.ds(s, f.shape[0], stride=n_lines), :] = f[:, s * LANES:(s + 1) * LANES]

    logits = _mm_f32(f, rw_ref[...]) + rb_ref[...]
    tm, ne = logits.shape
    eid = lax.broadcasted_iota(jnp.int32, (tm, ne), 1)
    work = logits
    sel = jnp.zeros((tm, ne), F32)
    idx_cols, val_cols = [], []
    for _ in range(TOP_K):
        top = jnp.max(work, axis=-1, keepdims=True)
        pick = jnp.min(jnp.where(work == top, eid, ne), axis=-1, keepdims=True)
        hit = eid == pick
        sel = jnp.where(hit, 1.0, sel)
        work = jnp.where(hit, -jnp.inf, work)
        idx_cols.append(pick)
        val_cols.append(top)
    exps = [jnp.exp(vk - val_cols[0]) for vk in val_cols]
    denom = exps[0] + exps[1] + exps[2] + exps[3]
    row = lax.broadcasted_iota(jnp.int32, (tm, tm), 0)
    colm = lax.broadcasted_iota(jnp.int32, (tm, tm), 1)
    before = _mm((colm < row).astype(F32), sel) + carry[0:1, :]
    rank_cols = [jnp.sum(jnp.where(eid == ic, before, 0.0), axis=-1, keepdims=True)
                 for ic in idx_cols]
    idx_ref[0] = jnp.concatenate(idx_cols, axis=-1)
    gate_ref[0] = jnp.concatenate([e / denom for e in exps], axis=-1)
    rank_ref[0] = jnp.concatenate(rank_cols, axis=-1).astype(jnp.int32)
    total = carry[0:1, :] + jnp.sum(sel, axis=0, keepdims=True)
    carry[...] = jnp.broadcast_to(total, carry.shape)
    cnt_ref[...] = jnp.broadcast_to(total, cnt_ref.shape).astype(jnp.int32)


def _outproj(x, y5, ydir, bv, g, ln_w, ln_b, hsel, hselt, wo, gt1, g2, sc2, sh2, router_w,
             router_b):
    bsz, n, d = x.shape
    rw = y5.shape[-1]
    ne = router_w.shape[-1]
    tm = ROW_TILE
    row = lambda w: pl.BlockSpec((1, tm, w), lambda b, i: (b, i, 0))
    vec = lambda w: pl.BlockSpec((1, w), lambda b, i: (0, 0))
    bvec = pl.BlockSpec((1, 1, d), lambda b, i: (b, 0, 0))
    full = lambda a: pl.BlockSpec(a.shape, lambda b, i: (0,) * a.ndim)
    return pl.pallas_call(
        _outproj_kernel,
        grid=(bsz, n // tm),
        in_specs=[row(d), row(rw),
                  pl.BlockSpec((1, 1, tm, rw), lambda b, i: (0, b, i, 0)),
                  pl.BlockSpec((1, 1, tm, rw), lambda b, i: (1, b, i, 0)),
                  row(rw), row(rw), vec(rw), vec(rw), full(hsel), full(hselt),
                  pl.BlockSpec(wo.shape, lambda b, i: (0, 0), pipeline_mode=pl.Buffered(1)),
                  bvec, vec(d), bvec, bvec, full(router_w), vec(ne)],
        out_specs=[row(d), pl.BlockSpec((1, tm * (d // LANES), LANES), lambda b, i: (b, i, 0)),
                   row(TOP_K), row(TOP_K), row(TOP_K),
                   pl.BlockSpec((8, ne), lambda b, i: (0, 0))],
        out_shape=[jax.ShapeDtypeStruct((bsz, n, d), F32),
                   jax.ShapeDtypeStruct((bsz, n * (d // LANES), LANES), F32),
                   jax.ShapeDtypeStruct((bsz, n, TOP_K), jnp.int32),
                   jax.ShapeDtypeStruct((bsz, n, TOP_K), F32),
                   jax.ShapeDtypeStruct((bsz, n, TOP_K), jnp.int32),
                   jax.ShapeDtypeStruct((8, ne), jnp.int32)],
        scratch_shapes=[pltpu.VMEM((8, ne), F32)],
        compiler_params=_params("arbitrary", "arbitrary"),
        name="outproj_router",
    )(x, y5, ydir, ydir, bv, g, ln_w.reshape(1, rw), ln_b.reshape(1, rw), hsel, hselt, wo, gt1,
      g2.reshape(1, d), sc2, sh2, router_w, router_b.reshape(1, ne))


def _dispatch_kernel(tile_rows, idx_ref, rank_ref, start_ref, f_ref, xs_in_ref, xs_ref, sem):
    del xs_in_ref
    tr = tile_rows
    tm = f_ref.shape[0] // tr

    def issue(t, _):
        src = f_ref.at[pl.ds(pl.multiple_of(t * tr, tr), tr)]
        for k in range(TOP_K):
            a = t * TOP_K + k
            dst = start_ref[idx_ref[a]] + rank_ref[a]
            pltpu.make_async_copy(src, xs_ref.at[pl.ds(pl.multiple_of(dst * tr, tr), tr)],
                                  sem).start()
        return 0

    lax.fori_loop(0, tm, issue, 0)
    for _ in range(TOP_K):
        pltpu.make_async_copy(f_ref, xs_ref.at[pl.ds(0, tm * tr)], sem).wait()


def _dispatch(idx_flat, rank_flat, start, f_lines, n_tok, n_slots):
    lanes = f_lines.shape[1]
    tr = f_lines.shape[0] // n_tok
    tm = ROW_TILE
    smem_tok = pl.BlockSpec((tm * TOP_K,), lambda i: (i,), memory_space=pltpu.SMEM)
    xs0 = jnp.zeros((n_slots * tr, lanes), f_lines.dtype)
    return pl.pallas_call(
        functools.partial(_dispatch_kernel, tr),
        grid=(n_tok // tm,),
        in_specs=[smem_tok, smem_tok,
                  pl.BlockSpec(start.shape, lambda i: (0,), memory_space=pltpu.SMEM),
                  pl.BlockSpec((tm * tr, lanes), lambda i: (i, 0)),
                  pl.BlockSpec(memory_space=pl.ANY)],
        out_specs=pl.BlockSpec(memory_space=pl.ANY),
        out_shape=jax.ShapeDtypeStruct((n_slots * tr, lanes), f_lines.dtype),
        scratch_shapes=[pltpu.SemaphoreType.DMA(())],
        input_output_aliases={4: 0},
        compiler_params=_params("arbitrary"),
        name="moe_dispatch",
    )(idx_flat, rank_flat, start, f_lines, xs0)


def _moe_kernel(sbe_ref, sbrow_ref, sbn_ref, nsb_ref, xs_ref, wg_ref, wl_ref, wd_ref, bg_ref,
                bl_ref, bd_ref, y_ref, x_stage, x_bf, acc, wg_bf, wl_bf, wd_bf, y_stage, sem_in,
                sem_out):
    sb, j = pl.program_id(0), pl.program_id(1)
    nj = pl.num_programs(1)
    rb = MOE_ROWS
    n_tiles = y_stage.shape[1] // rb
    valid = sb < nsb_ref[0]

    def copy_in(row, slot):
        src = xs_ref.at[pl.ds(pl.multiple_of(row * n_tiles, rb * n_tiles), rb * n_tiles)]
        return pltpu.make_async_copy(src, x_stage.at[slot], sem_in.at[slot])

    def copy_out(row, slot):
        dst = y_ref.at[pl.ds(pl.multiple_of(row * n_tiles, rb * n_tiles), rb * n_tiles)]
        return pltpu.make_async_copy(y_stage.at[slot], dst, sem_out.at[slot])

    @pl.when(jnp.logical_and(sb == 0, j == 0))
    def _():
        x_bf[...] = jnp.zeros_like(x_bf)

    @pl.when(valid)
    def _():
        n_blk = sbn_ref[sb]
        n_pairs = (n_blk + 1) // 2
        row0 = sbrow_ref[sb] * rb

        @pl.when(j == 0)
        def _():
            copy_in(row0, 0).start()

            def load(k, carry):
                slot = k % 2

                @pl.when(k + 1 < n_blk)
                def _():
                    copy_in(row0 + (k + 1) * rb, 1 - slot).start()

                copy_in(row0 + k * rb, slot).wait()
                r = pl.multiple_of(k * rb, rb)
                for s in range(n_tiles):
                    x_bf[pl.ds(r, rb), s * LANES:(s + 1) * LANES] = (
                        x_stage[slot, pl.ds(s, rb, stride=n_tiles), :].astype(BF16))
                return carry

            lax.fori_loop(0, n_blk, load, 0)

            def clear(k, carry):
                acc[pl.ds(pl.multiple_of(k * rb, rb), rb), :] = jnp.broadcast_to(
                    bd_ref[0], (rb, acc.shape[1]))
                return carry

            lax.fori_loop(0, 2 * n_pairs, clear, 0)

        wg_bf[...] = wg_ref[0].astype(BF16)
        wl_bf[...] = wl_ref[0].astype(BF16)
        wd_bf[...] = wd_ref[0].astype(BF16)

        def row_pair(kk, carry):
            r = pl.ds(pl.multiple_of(kk * 2 * rb, 2 * rb), 2 * rb)
            x = x_bf[r, :]
            glu = jnp.dot(x, wg_bf[...], preferred_element_type=F32) + bg_ref[0]
            lin = jnp.dot(x, wl_bf[...], preferred_element_type=F32) + bl_ref[0]
            glu = jnp.minimum(glu, SWIGLU_LIMIT)
            lin = jnp.clip(lin, -SWIGLU_LIMIT, SWIGLU_LIMIT)
            act = ((lin + 1.0) * glu * jax.nn.sigmoid(SWIGLU_ALPHA * glu)).astype(BF16)
            acc[r, :] += jnp.dot(act, wd_bf[...], preferred_element_type=F32)
            return carry

        lax.fori_loop(0, n_pairs, row_pair, 0)

        @pl.when(j == nj - 1)
        def _():
            def store(k, carry):
                slot = k % 2

                @pl.when(k >= 2)
                def _():
                    copy_out(row0, slot).wait()

                r = pl.multiple_of(k * rb, rb)
                for s in range(n_tiles):
                    y_stage[slot, pl.ds(s, rb, stride=n_tiles), :] = (
                        acc[pl.ds(r, rb), s * LANES:(s + 1) * LANES])
                copy_out(row0 + k * rb, slot).start()
                return carry

            lax.fori_loop(0, n_blk, store, 0)

            @pl.when(n_blk >= 2)
            def _():
                copy_out(row0, n_blk % 2).wait()

            copy_out(row0, (n_blk - 1) % 2).wait()

    @pl.when(jnp.logical_not(valid))
    def _():
        q = (sb - nsb_ref[0]) * nj + j
        last = jnp.maximum(nsb_ref[0] - 1, 0)
        first_unused = sbrow_ref[last] + sbn_ref[last]
        n_unused = y_ref.shape[0] // (rb * n_tiles) - first_unused

        @pl.when(q < n_unused)
        def _():
            y_stage[0] = jnp.zeros(y_stage.shape[1:], y_stage.dtype)
            cp = copy_out((first_unused + q) * rb, 0)
            cp.start()
            cp.wait()


def _moe_experts(sb_e, sb_row, sb_n, n_sb, xs_lines, w_gu, b_gu, w_dn, b_dn):
    ne, d, two_de = w_gu.shape
    lanes = xs_lines.shape[1]
    n_slots = xs_lines.shape[0] // (d // lanes)
    de = two_de // 2
    th = MOE_HIDDEN_TILE
    nj = de // th
    n_groups = sb_e.shape[0]
    grp = lambda sb, nsb: jnp.maximum(jnp.minimum(sb, nsb[0] - 1), 0)
    tile = lambda sb, j, nsb: jnp.where(sb < nsb[0], j, nj - 1)
    grid_spec = pltpu.PrefetchScalarGridSpec(
        num_scalar_prefetch=4,
        grid=(n_groups, nj),
        in_specs=[pl.BlockSpec(memory_space=pl.ANY),
                  pl.BlockSpec((1, d, th), lambda sb, j, e, r, n, nsb:
                               (e[grp(sb, nsb)], 0, tile(sb, j, nsb))),
                  pl.BlockSpec((1, d, th), lambda sb, j, e, r, n, nsb:
                               (e[grp(sb, nsb)], 0, nj + tile(sb, j, nsb))),
                  pl.BlockSpec((1, th, d), lambda sb, j, e, r, n, nsb:
                               (e[grp(sb, nsb)], tile(sb, j, nsb), 0)),
                  pl.BlockSpec((1, 1, th), lambda sb, j, e, r, n, nsb:
                               (e[grp(sb, nsb)], 0, tile(sb, j, nsb))),
                  pl.BlockSpec((1, 1, th), lambda sb, j, e, r, n, nsb:
                               (e[grp(sb, nsb)], 0, nj + tile(sb, j, nsb))),
                  pl.BlockSpec((1, 1, d), lambda sb, j, e, r, n, nsb: (e[grp(sb, nsb)], 0, 0))],
        out_specs=pl.BlockSpec(memory_space=pl.ANY),
        scratch_shapes=[pltpu.VMEM((2, MOE_ROWS * (d // lanes), lanes), F32),
                        pltpu.VMEM((MOE_GROUP_ROWS, d), BF16),
                        pltpu.VMEM((MOE_GROUP_ROWS, d), F32),
                        pltpu.VMEM((d, th), BF16), pltpu.VMEM((d, th), BF16),
                        pltpu.VMEM((th, d), BF16),
                        pltpu.VMEM((2, MOE_ROWS * (d // lanes), lanes), F32),
                        pltpu.SemaphoreType.DMA((2,)), pltpu.SemaphoreType.DMA((2,))])
    return pl.pallas_call(
        _moe_kernel,
        grid_spec=grid_spec,
        out_shape=jax.ShapeDtypeStruct((n_slots * (d // lanes), lanes), F32),
        compiler_params=_params("arbitrary", "arbitrary"),
        name="moe_experts",
    )(sb_e, sb_row, sb_n, n_sb, xs_lines, w_gu, w_gu, w_dn, b_gu.reshape(ne, 1, two_de),
      b_gu.reshape(ne, 1, two_de), b_dn.reshape(ne, 1, d))


def _combine_kernel(idx_ref, rank_ref, start_ref, x1_ref, gate_ref, gt2_ref, fg_ref, y_ref,
                    o_ref, buf, sem):
    tm, d = x1_ref.shape
    tr = d // LANES

    def issue(t, _):
        dst = pl.ds(pl.multiple_of(t * tr, tr), tr)
        for k in range(TOP_K):
            a = t * TOP_K + k
            src = start_ref[idx_ref[a]] + rank_ref[a]
            pltpu.make_async_copy(y_ref.at[pl.ds(pl.multiple_of(src * tr, tr), tr)],
                                  buf.at[k, dst], sem).start()
        return 0

    lax.fori_loop(0, tm, issue, 0)
    for k in range(TOP_K):
        pltpu.make_async_copy(y_ref.at[pl.ds(0, tm * tr)], buf.at[k], sem).wait()
    gate = gate_ref[...]
    cols = []
    for s in range(tr):
        acc = gate[:, 0:1] * buf[0, pl.ds(s, tm, stride=tr), :]
        for k in range(1, TOP_K):
            acc = acc + gate[:, k:k + 1] * buf[k, pl.ds(s, tm, stride=tr), :]
        cols.append(acc)
    x2 = x1_ref[...] + gt2_ref[0] * jnp.concatenate(cols, axis=-1)
    ms = jnp.mean(x2 * x2, axis=-1, keepdims=True)
    o_ref[...] = x2 * lax.rsqrt(ms + NORM_EPS) * fg_ref[...]


def _combine(idx_flat, rank_flat, start, x1, gate4, gt2, fg, y_lines, tiles_per_batch):
    n_tok, d = x1.shape
    lanes = y_lines.shape[1]
    tm = ROW_TILE
    smem_tok = pl.BlockSpec((tm * TOP_K,), lambda i: (i,), memory_space=pltpu.SMEM)
    return pl.pallas_call(
        _combine_kernel,
        grid=(n_tok // tm,),
        in_specs=[smem_tok, smem_tok,
                  pl.BlockSpec(start.shape, lambda i: (0,), memory_space=pltpu.SMEM),
                  pl.BlockSpec((tm, d), lambda i: (i, 0)),
                  pl.BlockSpec((tm, TOP_K), lambda i: (i, 0)),
                  pl.BlockSpec((1, 1, d), lambda i: (i // tiles_per_batch, 0, 0)),
                  pl.BlockSpec((1, d), lambda i: (0, 0)),
                  pl.BlockSpec(memory_space=pl.ANY)],
        out_specs=pl.BlockSpec((tm, d), lambda i: (i, 0)),
        out_shape=jax.ShapeDtypeStruct((n_tok, d), F32),
        scratch_shapes=[pltpu.VMEM((TOP_K, tm * (d // lanes), lanes), F32),
                        pltpu.SemaphoreType.DMA(())],
        compiler_params=_params("arbitrary"),
        name="moe_combine_final",
    )(idx_flat, rank_flat, start, x1, gate4, gt2, fg.reshape(1, d), y_lines)


def _block_diag2(w):
    z = jnp.zeros_like(w[0])
    return jnp.concatenate([jnp.concatenate([w[0], z], 1), jnp.concatenate([z, w[1]], 1)], 0)


def kernel(x, c, ctx, c_ctx, mod_w, mod_b, norm1_g, w_in, s5_a_re, s5_a_im, s5_log_dt, s5_b_re,
           s5_b_im, s5_c_re, s5_c_im, s5_d, s5_glu_w, s5_glu_b, rw_mu, rw_w0, rw_w2, rw_a0, rw_a2,
           rw_g2, rw_k_k, rw_k_a, rw_r_k, rw_ln_w, rw_ln_b, w_out, norm2_g, router_w, router_b,
           exp_w_gu, exp_b_gu, exp_w_dn, exp_b_dn, final_g):
    assert mod_w.shape[0] == 1, "single-layer stack only"
    bsz, n_lat, d = x.shape
    n_ctx = ctx.shape[1]
    assert bsz == 2 and n_ctx % ROW_TILE == 0 and n_lat % ROW_TILE == 0
    s5w = s5_d.shape[-1]
    rww = rw_k_k.shape[-1]
    n_heads = rww // RW_HEAD
    ne = router_w.shape[-1]

    cond8 = jnp.zeros((8, d), F32).at[:bsz].set(c).at[bsz].set(c_ctx)
    mod = _adaln(cond8, mod_w[0], mod_b[0])
    sh1, sc1, gt1, sh2, sc2, gt2 = [m[:bsz, None, :] for m in jnp.split(mod, 6, axis=-1)]
    csh1, csc1 = [jnp.broadcast_to(m[bsz][None, None, :], (bsz, 1, d))
                  for m in jnp.split(mod, 6, axis=-1)[:2]]

    wu = w_in[0][:, :s5w].astype(BF16)
    wz = w_in[0][:, s5w:].astype(BF16)
    u_lat, z_lat = _inproj(x, norm1_g[0], sc1, sh1, wu, wz)
    u_ctx, z_ctx = _inproj(ctx, norm1_g[0], csc1, csh1, wu, wz)

    s5c = _s5_constants(s5_a_re[0], s5_a_im[0], s5_log_dt[0], s5_b_re[0], s5_b_im[0], s5_c_re[0],
                        s5_c_im[0], s5_d[0], s5_glu_w[0], s5_glu_b[0])
    y5 = _s5(jnp.concatenate([u_ctx, u_lat], axis=1), s5c, n_ctx)

    lanes_idx = jnp.arange(rww) // RW_HEAD
    hsel = (lanes_idx[:, None] == jnp.arange(LANES)[None, :]).astype(BF16)
    hselt = hsel.T
    feat_consts = (rw_mu[0], rw_w0[0].reshape(1, N_DIR * rww),
                   _block_diag2(rw_w2[0]).astype(BF16), rw_a0[0].reshape(1, N_DIR * rww),
                   _block_diag2(rw_a2[0]).astype(BF16), rw_g2[0].astype(BF16),
                   rw_k_k[0].reshape(1, rww), rw_k_a[0].reshape(1, rww),
                   rw_r_k[0].reshape(1, rww), hsel, hselt)
    fc = _rw_features(z_ctx, n_ctx, False, *feat_consts)
    fl = _rw_features(z_lat, GRID_W, True, *feat_consts)

    def scan_inputs(f):
        r, v, kh, _, _, lw, q, kt = f
        return r, v, kh, lw, q, kt

    gc, hc, rhc, y0c = _rw_chunks(*scan_inputs(fc))
    gl, hl, rhl, y0l = _rw_chunks(*scan_inputs(fl))
    s_zero = jnp.zeros((N_DIR, bsz, RW_HEAD, rww), F32)
    (s_ctx,) = _rw_state(s_zero, gc, hc, rhc, y0c, emit_y=False)
    y_dir, _ = _rw_state(s_ctx, gl, hl, rhl, y0l, emit_y=True)

    x1, f, idx4, gate4, rank4, counts = _outproj(
        x, y5, y_dir, fl[4], fl[3], rw_ln_w[0], rw_ln_b[0], hsel, hselt, w_out[0].astype(BF16),
        gt1, norm2_g[0], sc2, sh2, router_w[0], router_b[0])

    n_tok = bsz * n_lat
    cnt = counts[0]
    padded = (cnt + MOE_ROWS - 1) // MOE_ROWS * MOE_ROWS
    pend = jnp.cumsum(padded)
    start = (pend - padded).astype(jnp.int32)
    nb = n_tok * TOP_K // MOE_ROWS + ne
    n_slots = nb * MOE_ROWS
    blocks_per_group = MOE_GROUP_ROWS // MOE_ROWS
    n_groups_max = n_slots // MOE_GROUP_ROWS + ne
    groups_e = (padded + MOE_GROUP_ROWS - 1) // MOE_GROUP_ROWS
    groups_end = jnp.cumsum(groups_e)
    gidx = jnp.arange(n_groups_max)
    sb_e = jnp.minimum(jnp.sum(groups_end[None, :] <= gidx[:, None], axis=1), ne - 1)
    local = gidx - (groups_end - groups_e)[sb_e]
    sb_row = start[sb_e] // MOE_ROWS + local * blocks_per_group
    sb_n = jnp.clip(padded[sb_e] // MOE_ROWS - local * blocks_per_group, 0, blocks_per_group)
    n_sb = groups_end[-1].astype(jnp.int32).reshape(1)
    idx_flat = idx4.reshape(-1)
    rank_flat = rank4.reshape(-1)

    xs = _dispatch(idx_flat, rank_flat, start, f.reshape(-1, LANES), n_tok, n_slots)
    y3 = _moe_experts(sb_e.astype(jnp.int32), sb_row.astype(jnp.int32), sb_n.astype(jnp.int32),
                      n_sb, xs, exp_w_gu[0], exp_b_gu[0], exp_w_dn[0], exp_b_dn[0])
    out = _combine(idx_flat, rank_flat, start, x1.reshape(n_tok, d), gate4.reshape(n_tok, TOP_K),
                   gt2, final_g, y3, n_lat // ROW_TILE)
    return out.reshape(bsz, n_lat, d)
```

```python
import functools
import math

import jax
import jax.numpy as jnp
from jax import lax
from jax.experimental import pallas as pl
from jax.experimental.pallas import tpu as pltpu

F32 = jnp.float32
BF16 = jnp.bfloat16
HIGHEST = lax.Precision.HIGHEST

LANES = 128
VMEM_LIMIT_BYTES = 56 * 1024 * 1024

NORM_EPS = 1e-5
N_DIR = 2
S5_GROUP = 16
S5_STATE = 64
S5_CHUNK = 8
S5_LANE_GROUPS = LANES // S5_GROUP
S5_SCAN_ROWS = 8
RW_HEAD = 64
RW_CHUNK = 64
RW_CHUNKS_PER_STEP = 2
RW_GN_EPS = 64e-5
GRID_W = 64
TOP_K = 4
SWIGLU_ALPHA = 1.702
SWIGLU_LIMIT = 7.0
ROW_TILE = 256
MOE_ROWS = 256
MOE_GROUP_ROWS = 1536
MOE_HIDDEN_TILE = 256


def _params(*sem):
    return pltpu.CompilerParams(dimension_semantics=sem, vmem_limit_bytes=VMEM_LIMIT_BYTES)


def _mm(a, b, dims=((1,), (0,))):
    return lax.dot_general(a.astype(BF16), b.astype(BF16), (dims, ((), ())),
                           preferred_element_type=F32)


def _split_bf16(a):
    hi = a.astype(BF16)
    return hi, (a - hi.astype(F32)).astype(BF16)


def _mm_sel(a, sel):
    hi, lo = _split_bf16(a)
    m = a.shape[0]
    out = jnp.dot(jnp.concatenate([hi, lo], axis=0), sel.astype(BF16), preferred_element_type=F32)
    return out[:m] + out[m:]


def _sel_mm(sel, a):
    hi, lo = _split_bf16(a)
    n = a.shape[1]
    out = jnp.dot(sel.astype(BF16), jnp.concatenate([hi, lo], axis=1), preferred_element_type=F32)
    return out[:, :n] + out[:, n:]


def _mm_f32(a, b, dims=((1,), (0,))):
    return lax.dot_general(a, b, (dims, ((), ())), precision=HIGHEST,
                           preferred_element_type=F32)


def _adaln_kernel(cond_ref, w_ref, b_ref, o_ref):
    c = cond_ref[...]
    o_ref[...] = _mm_f32(c * jax.nn.sigmoid(c), w_ref[...]) + b_ref[...]


def _adaln(cond8, w, b):
    d, n = w.shape
    tn = 1536
    return pl.pallas_call(
        _adaln_kernel,
        grid=(n // tn,),
        in_specs=[pl.BlockSpec((8, d), lambda j: (0, 0)),
                  pl.BlockSpec((d, tn), lambda j: (0, j)),
                  pl.BlockSpec((1, tn), lambda j: (0, j))],
        out_specs=pl.BlockSpec((8, tn), lambda j: (0, j)),
        out_shape=jax.ShapeDtypeStruct((8, n), F32),
        compiler_params=_params("parallel"),
        name="adaln",
    )(cond8, w, b.reshape(1, n))


def _rms_mod(x, g, sc, sh):
    y = x * lax.rsqrt(jnp.mean(x * x, axis=-1, keepdims=True) + NORM_EPS)
    return (y * g) * (1.0 + sc) + sh


def _inproj_kernel(x_ref, g_ref, sc_ref, sh_ref, wu_ref, wz_ref, u_ref, z_ref):
    h = _rms_mod(x_ref[0], g_ref[...], sc_ref[0], sh_ref[0]).astype(BF16)
    u_ref[0] = jnp.dot(h, wu_ref[...], preferred_element_type=F32)
    z_ref[0] = jnp.dot(h, wz_ref[...], preferred_element_type=F32)


def _inproj(x, g, sc, sh, wu, wz):
    bsz, n, d = x.shape
    nu, nz = wu.shape[1], wz.shape[1]
    tm = ROW_TILE
    const = dict(pipeline_mode=pl.Buffered(1))
    return pl.pallas_call(
        _inproj_kernel,
        grid=(bsz, n // tm),
        in_specs=[pl.BlockSpec((1, tm, d), lambda b, i: (b, i, 0)),
                  pl.BlockSpec((1, d), lambda b, i: (0, 0)),
                  pl.BlockSpec((1, 1, d), lambda b, i: (b, 0, 0)),
                  pl.BlockSpec((1, 1, d), lambda b, i: (b, 0, 0)),
                  pl.BlockSpec((d, nu), lambda b, i: (0, 0), **const),
                  pl.BlockSpec((d, nz), lambda b, i: (0, 0), **const)],
        out_specs=[pl.BlockSpec((1, tm, nu), lambda b, i: (b, i, 0)),
                   pl.BlockSpec((1, tm, nz), lambda b, i: (b, i, 0))],
        out_shape=[jax.ShapeDtypeStruct((bsz, n, nu), F32),
                   jax.ShapeDtypeStruct((bsz, n, nz), F32)],
        compiler_params=_params("parallel", "parallel"),
        name="inproj",
    )(x, g.reshape(1, d), sc, sh, wu, wz)


def _s5_constants(a_re, a_im, log_dt, b_re, b_im, c_re, c_im, d_skip, glu_w, glu_b):
    t = S5_CHUNK
    g, p = a_re.shape[1], a_re.shape[2]
    h = S5_GROUP
    gl = S5_LANE_GROUPS
    no = g // gl
    n = jnp.arange(t + 1, dtype=F32)[None, :, None]
    f_parts, e_parts, lam_rows, lags = {}, {}, [], []
    for d in range(N_DIR):
        ar, ai = a_re[d].astype(F32)[:, None, :], a_im[d].astype(F32)[:, None, :]
        dt = jnp.exp(log_dt[d].astype(F32))[:, None, None]
        mag = jnp.exp(n * (ar * dt))
        pr, pi = mag * jnp.cos(n * (ai * dt)), mag * jnp.sin(n * (ai * dt))
        lr, li = pr[:, 1], pi[:, 1]
        ar, ai = ar[:, 0], ai[:, 0]
        den = ar * ar + ai * ai
        cf_re = ((lr - 1.0) * ar + li * ai) / den
        cf_im = (li * ar - (lr - 1.0) * ai) / den
        br = jnp.swapaxes(b_re[d].astype(F32), 1, 2)
        bi = jnp.swapaxes(b_im[d].astype(F32), 1, 2)
        bb_re = cf_re[:, None] * br - cf_im[:, None] * bi
        bb_im = cf_re[:, None] * bi + cf_im[:, None] * br
        cr, ci = c_re[d].astype(F32)[:, None], c_im[d].astype(F32)[:, None]
        cl_re = cr * pr[:, :, None, :] - ci * pi[:, :, None, :]
        cl_im = cr * pi[:, :, None, :] + ci * pr[:, :, None, :]
        dims = (((2,), (2,)), ((0,), (0,)))
        lag = (lax.dot_general(cl_re[:, :t].reshape(g, t * h, p), bb_re, dims, precision=HIGHEST)
               - lax.dot_general(cl_im[:, :t].reshape(g, t * h, p), bb_im, dims, precision=HIGHEST))
        lags.append(lag.reshape(g, t, h, h))
        pw = jnp.arange(t - 1, -1, -1) if d == 0 else jnp.arange(t)
        ps_re, ps_im = pr[:, pw][:, :, None, :], pi[:, pw][:, :, None, :]
        f_parts[0, d] = ps_re * bb_re[:, None] - ps_im * bb_im[:, None]
        f_parts[1, d] = ps_re * bb_im[:, None] + ps_im * bb_re[:, None]
        pe = jnp.arange(1, t + 1) if d == 0 else jnp.arange(t, 0, -1)
        e_parts[0, d] = jnp.transpose(cl_re[:, pe], (0, 3, 1, 2))
        e_parts[1, d] = -jnp.transpose(cl_im[:, pe], (0, 3, 1, 2))
        lam_rows += [pr[:, t], pi[:, t]]
    parts = [(q, d) for q in range(2) for d in range(N_DIR)]
    fg = jnp.stack([f_parts[k] for k in parts], axis=3).reshape(no, gl, t, h, 4 * p)
    f_rows = jnp.transpose(fg, (0, 2, 1, 3, 4)).reshape(no, t * gl * h, 4 * p)
    eg = jnp.stack([e_parts[k] for k in parts], axis=1).reshape(no, gl, 4, p, t * h)
    e_rows = jnp.transpose(eg, (0, 2, 1, 3, 4)).reshape(no, 4 * gl * p, t * h)
    s_idx = jnp.arange(t)[:, None]
    t_idx = jnp.arange(t)[None, :]
    kf = jnp.where((t_idx >= s_idx)[None, :, :, None, None],
                   lags[0][:, jnp.clip(t_idx - s_idx, 0, t - 1)], 0.0)
    kb = jnp.where((s_idx >= t_idx)[None, :, :, None, None],
                   lags[1][:, jnp.clip(s_idx - t_idx, 0, t - 1)], 0.0)
    kg = jnp.transpose(kf + kb, (0, 1, 4, 2, 3)).reshape(no, gl, t, h, t * h)
    k_rows = jnp.transpose(kg, (0, 2, 1, 3, 4)).reshape(no, t * gl * h, t * h)
    fmat, emat, kmat = _s5_expand(f_rows, e_rows, k_rows)
    lam = jnp.stack(lam_rows, axis=1)
    lam = jnp.transpose(lam.reshape(no, gl, 4, p), (0, 2, 1, 3)).reshape(no, 4, gl * p)
    gmat = jnp.einsum('ab,oahk->oahbk', jnp.eye(gl, dtype=F32),
                      glu_w.astype(F32).reshape(no, gl, h, h)).reshape(no, gl * h, gl * h)
    dvec = d_skip.astype(F32).reshape(no, 1, gl * h)
    bvec = glu_b.astype(F32).reshape(no, 1, gl * h)
    return fmat, emat, kmat, lam, dvec, bvec, gmat.astype(BF16)


def _s5_expand_kernel(f_ref, e_ref, k_ref, rep_s_ref, rep_t_ref, fo_ref, eo_ref, ko_ref):
    def expand(x, rep, row_div, col_div):
        y = jnp.dot(x.astype(BF16), rep, preferred_element_type=F32)
        row = lax.broadcasted_iota(jnp.int32, y.shape, 0)
        col = lax.broadcasted_iota(jnp.int32, y.shape, 1)
        same = (row // row_div) % S5_LANE_GROUPS == (col // col_div) % S5_LANE_GROUPS
        return jnp.where(same, y, 0.0).astype(BF16)

    fo_ref[0] = expand(f_ref[0], rep_s_ref[...], S5_GROUP, S5_STATE)
    eo_ref[0] = expand(e_ref[0], rep_t_ref[...], S5_STATE, S5_GROUP)
    ko_ref[0] = expand(k_ref[0], rep_t_ref[...], S5_GROUP, S5_GROUP)


def _s5_expand(f_rows, e_rows, k_rows):
    no, rows_f, cols_f = f_rows.shape
    _, rows_e, cols_e = e_rows.shape
    gl = S5_LANE_GROUPS
    c = jnp.arange(cols_f * gl)
    rep_s = (jnp.arange(cols_f)[:, None]
             == (c // (gl * S5_STATE)) * S5_STATE + c % S5_STATE).astype(BF16)
    c = jnp.arange(cols_e * gl)
    rep_t = (jnp.arange(cols_e)[:, None]
             == (c // (gl * S5_GROUP)) * S5_GROUP + c % S5_GROUP).astype(BF16)
    blk = lambda r, cc: pl.BlockSpec((1, r, cc), lambda o: (o, 0, 0))
    full = lambda a: pl.BlockSpec(a.shape, lambda o: (0, 0))
    return pl.pallas_call(
        _s5_expand_kernel,
        grid=(no,),
        in_specs=[blk(rows_f, cols_f), blk(rows_e, cols_e), blk(rows_f, cols_e),
                  full(rep_s), full(rep_t)],
        out_specs=[blk(rows_f, cols_f * gl), blk(rows_e, cols_e * gl), blk(rows_f, cols_e * gl)],
        out_shape=[jax.ShapeDtypeStruct((no, rows_f, cols_f * gl), BF16),
                   jax.ShapeDtypeStruct((no, rows_e, cols_e * gl), BF16),
                   jax.ShapeDtypeStruct((no, rows_f, cols_e * gl), BF16)],
        compiler_params=_params("parallel"),
        name="s5_expand",
    )(f_rows, e_rows, k_rows, rep_s, rep_t)


def _s5_chunk_rows(u_ref, b, start, n_chunks):
    t = S5_CHUNK
    return jnp.concatenate(
        [u_ref[b, pl.ds(start + s, n_chunks, stride=t), :] for s in range(t)], axis=-1)


def _s5_in_kernel(u_ref, f_ref, z_ref):
    bsz, n_tok, _ = u_ref.shape
    for b in range(bsz):
        z_ref[b] = _mm(_s5_chunk_rows(u_ref, b, 0, n_tok // S5_CHUNK), f_ref[0])


def _s5_scan_kernel(n_ctx_chunks, z_ref, lam_ref, hs_ref):
    n_chunks = z_ref.shape[1]
    q = z_ref.shape[2] // 4
    rows = S5_SCAN_ROWS
    n_blocks, ctx_blocks = n_chunks // rows, n_ctx_chunks // rows
    lam = lam_ref[0]
    lf_re, lf_im, lb_re, lb_im = lam[0:1], lam[1:2], lam[2:3], lam[3:4]

    def block(k, carry):
        f_re, f_im, b_re, b_im = carry
        kb = jnp.where(k < ctx_blocks, ctx_blocks - 1 - k, n_blocks - 1 + ctx_blocks - k)
        rf = pl.multiple_of(k * rows, rows)
        rb = pl.multiple_of(kb * rows, rows)
        zf = z_ref[0, pl.ds(rf, rows), :]
        zb = z_ref[0, pl.ds(rb, rows), :]
        ent = [[], [], [], []]
        for j in range(rows):
            jb = rows - 1 - j
            for lst, val in zip(ent, (f_re, f_im, b_re, b_im)):
                lst.append(val)
            f_re, f_im = (lf_re * f_re - lf_im * f_im + zf[j:j + 1, 0:q],
                          lf_re * f_im + lf_im * f_re + zf[j:j + 1, 2 * q:3 * q])
            b_re, b_im = (lb_re * b_re - lb_im * b_im + zb[jb:jb + 1, q:2 * q],
                          lb_re * b_im + lb_im * b_re + zb[jb:jb + 1, 3 * q:4 * q])
        hs_ref[0, pl.ds(rf, rows), 0:q] = jnp.concatenate(ent[0], axis=0)
        hs_ref[0, pl.ds(rf, rows), 2 * q:3 * q] = jnp.concatenate(ent[1], axis=0)
        hs_ref[0, pl.ds(rb, rows), q:2 * q] = jnp.concatenate(ent[2][::-1], axis=0)
        hs_ref[0, pl.ds(rb, rows), 3 * q:4 * q] = jnp.concatenate(ent[3][::-1], axis=0)
        return f_re, f_im, b_re, b_im

    zero = jnp.zeros((1, q), F32)
    lax.fori_loop(0, n_blocks, block, (zero, zero, zero, zero))


def _s5_out_kernel(n_ctx, u_ref, hs_ref, e_ref, k_ref, g_ref, d_ref, b_ref, o_ref, y_scr):
    t = S5_CHUNK
    bsz, n_tok, lanes = u_ref.shape
    n_chunks = (n_tok - n_ctx) // t
    for b in range(bsz):
        x = _s5_chunk_rows(u_ref, b, n_ctx, n_chunks)
        y = _mm(x, k_ref[0]) + _mm(hs_ref[b, n_ctx // t:, :], e_ref[0])
        for s in range(t):
            y_scr[pl.ds(s, n_chunks, stride=t), :] = y[:, s * lanes:(s + 1) * lanes]
        y = jax.nn.gelu(y_scr[...] + d_ref[0] * u_ref[b, n_ctx:, :])
        gate = _mm(y, g_ref[0]) + b_ref[0]
        o_ref[b] = y * jax.nn.sigmoid(gate)


def _s5(u_all, consts, n_ctx):
    fmat, emat, kmat, lam, dvec, bvec, gmat = consts
    bsz, n_tok, width = u_all.shape
    t = S5_CHUNK
    n_chunks = n_tok // t
    no, rows_k, cols_f = fmat.shape
    assert n_ctx % (t * S5_SCAN_ROWS) == 0 and n_tok % (t * S5_SCAN_ROWS) == 0
    tok = pl.BlockSpec((1, n_tok, LANES), lambda o, b: (b, 0, o))
    state = pl.BlockSpec((1, n_chunks, cols_f), lambda o, b: (b, 0, o))
    full = lambda a: pl.BlockSpec((1,) + a.shape[1:], lambda o, b: (o, 0, 0))
    z = pl.pallas_call(
        _s5_in_kernel,
        grid=(no, bsz),
        in_specs=[tok, full(fmat)],
        out_specs=state,
        out_shape=jax.ShapeDtypeStruct((bsz, n_chunks, no * cols_f), F32),
        compiler_params=_params("parallel", "parallel"),
        name="s5_in",
    )(u_all, fmat)
    hs = pl.pallas_call(
        functools.partial(_s5_scan_kernel, n_ctx // t),
        grid=(no, bsz),
        in_specs=[state, full(lam)],
        out_specs=state,
        out_shape=jax.ShapeDtypeStruct((bsz, n_chunks, no * cols_f), F32),
        compiler_params=_params("parallel", "parallel"),
        name="s5_scan",
    )(z, lam)
    n_lat = n_tok - n_ctx
    return pl.pallas_call(
        functools.partial(_s5_out_kernel, n_ctx),
        grid=(no, bsz),
        in_specs=[tok, state, full(emat), full(kmat), full(gmat), full(dvec), full(bvec)],
        out_specs=pl.BlockSpec((1, n_lat, LANES), lambda o, b: (b, 0, o)),
        out_shape=jax.ShapeDtypeStruct((bsz, n_lat, width), F32),
        scratch_shapes=[pltpu.VMEM((n_lat, LANES), F32)],
        compiler_params=_params("parallel", "parallel"),
        name="s5_out",
    )(u_all, hs, emat, kmat, gmat, dvec, bvec)


def _rw_feat_kernel(width, has_vert, n_tiles, z_ref, zp_ref, zn_ref, mu_ref, w0_ref, w2_ref,
                    a0_ref, a2_ref, g2_ref, kk_ref, ka_ref, rk_ref, hsel_ref, hselt_ref,
                    r_ref, v_ref, kh_ref, g_ref, bv_ref, lw_ref, q_ref, kt_ref):
    z = z_ref[0]
    t0, cz = z.shape
    rw = r_ref.shape[-1]
    mu = mu_ref[...]
    tok = lax.broadcasted_iota(jnp.int32, (t0, 1), 0)
    col = tok % width
    left = pltpu.roll(z, 1, 0)
    right = pltpu.roll(z, t0 - 1, 0)
    out = z + jnp.where(col != 0, mu[0:1] * (left - z), 0.0)
    out = out + jnp.where(col != width - 1, mu[1:2] * (right - z), 0.0)
    if has_vert:
        i = pl.program_id(1)
        up = jnp.concatenate([zp_ref[0], z[:t0 - width]], axis=0)
        down = jnp.concatenate([z[width:], zn_ref[0]], axis=0)
        up_ok = jnp.logical_or(i > 0, tok >= width)
        down_ok = jnp.logical_or(i < n_tiles - 1, tok < t0 - width)
        out = out + jnp.where(up_ok, mu[2:3] * (up - z), 0.0)
        out = out + jnp.where(down_ok, mu[3:4] * (down - z), 0.0)
    r = out[:, 0:rw]
    k = out[:, rw:2 * rw]
    v = out[:, 2 * rw:3 * rw]
    o = 3 * rw
    lora = w2_ref.shape[0]
    xw = out[:, o:o + lora]
    xa = out[:, o + lora:o + 2 * lora]
    xg = out[:, o + 2 * lora:]
    dec = w0_ref[...] + _mm(jnp.tanh(xw), w2_ref[...])
    lw = -math.exp(-0.5) * jax.nn.sigmoid(dec)
    a = jax.nn.sigmoid(a0_ref[...] + _mm(xa, a2_ref[...]))
    g = _mm(jax.nn.sigmoid(xg), g2_ref[...])
    kk = k * kk_ref[...]
    ssq = _mm_sel(kk * kk, hsel_ref[...])
    inv = 1.0 / jnp.maximum(jnp.sqrt(ssq), 1e-12)
    kh = kk * _mm_sel(inv, hselt_ref[...])
    ka = ka_ref[...]
    kt_sum = jnp.zeros_like(k)
    for d in range(N_DIR):
        a_d = a[:, d * rw:(d + 1) * rw]
        kt_d = k * (1.0 + (a_d - 1.0) * ka)
        kt_sum = kt_sum + kt_d
        lw_ref[d, 0] = lw[:, d * rw:(d + 1) * rw]
        q_ref[d, 0] = a_d * kh
        kt_ref[d, 0] = kt_d
    bonus = _mm_sel(_mm_sel(r * kt_sum * rk_ref[...], hsel_ref[...]), hselt_ref[...])
    r_ref[0] = r
    v_ref[0] = v
    kh_ref[0] = kh
    g_ref[0] = g
    bv_ref[0] = bonus * v


def _rw_features(z, width, has_vert, mu, w0, w2blk, a0, a2blk, g2, k_k, k_a, r_k, hsel, hselt):
    bsz, n, cz = z.shape
    rw = k_k.shape[-1]
    t0 = ROW_TILE
    n_tiles = n // t0
    per = t0 // GRID_W
    nblk = n // GRID_W
    full = lambda a: pl.BlockSpec(a.shape, lambda b, i: (0,) * a.ndim)
    tok = pl.BlockSpec((1, t0, rw), lambda b, i: (b, i, 0))
    dtok = pl.BlockSpec((N_DIR, 1, t0, rw), lambda b, i: (0, b, i, 0))
    consts = (mu, w0, w2blk, a0, a2blk, g2, k_k, k_a, r_k, hsel, hselt)
    return pl.pallas_call(
        functools.partial(_rw_feat_kernel, width, has_vert, n_tiles),
        grid=(bsz, n_tiles),
        in_specs=[pl.BlockSpec((1, t0, cz), lambda b, i: (b, i, 0)),
                  pl.BlockSpec((1, GRID_W, cz), lambda b, i: (b, jnp.maximum(i * per - 1, 0), 0)),
                  pl.BlockSpec((1, GRID_W, cz),
                               lambda b, i: (b, jnp.minimum(i * per + per, nblk - 1), 0))]
        + [full(a) for a in consts],
        out_specs=[tok] * 5 + [dtok] * 3,
        out_shape=[jax.ShapeDtypeStruct((bsz, n, rw), F32)] * 5
        + [jax.ShapeDtypeStruct((N_DIR, bsz, n, rw), F32)] * 3,
        compiler_params=_params("parallel", "parallel"),
        name="rw_features",
    )(z, z, z, *consts)


def _rw_chunk_kernel(r_ref, v_ref, kh_ref, lw_ref, q_ref, kt_ref, g_ref, h_ref, rh_ref, y0_ref):
    rev = pl.program_id(0) == 1
    n = RW_CHUNK
    hd = RW_HEAD
    n_sub = lw_ref.shape[2] // n
    row = lax.broadcasted_iota(jnp.int32, (n, n), 0)
    col = lax.broadcasted_iota(jnp.int32, (n, n), 1)
    ahead = (row - col) * jnp.where(rev, -1, 1)
    incl = (ahead >= 0).astype(F32)
    strict = (ahead > 0).astype(F32)
    eye = (row == col).astype(F32)
    same_block = [(jnp.right_shift(row, s) == jnp.right_shift(col, s)).astype(F32)
                  for s in range(3, n.bit_length())]
    n_heads = lw_ref.shape[-1] // hd
    pt, rt, v, qt, ktt, qh, kth, gam, where = [], [], [], [], [], [], [], [], []
    for c in range(n_sub):
        tok = slice(c * n, (c + 1) * n)
        lw = lw_ref[0, 0, tok, :]
        b_incl = _sel_mm(incl, lw)
        btot = jnp.sum(lw, axis=0, keepdims=True)
        e_neg = jnp.exp(-b_incl)
        e_rem = jnp.exp(btot - b_incl)
        q_c, kt_c = q_ref[0, 0, tok, :], kt_ref[0, 0, tok, :]
        pt_all = kh_ref[0, tok, :] * jnp.exp(b_incl - lw)
        rt_all = r_ref[0, tok, :] * jnp.exp(b_incl)
        qt_all, ktt_all = q_c * e_neg, kt_c * e_neg
        qh_all, kth_all = q_c * e_rem, kt_c * e_rem
        gam_all = jnp.exp(btot)
        v_all = v_ref[0, tok, :]
        for h in range(n_heads):
            sl = slice(h * hd, (h + 1) * hd)
            for lst, arr in zip((pt, rt, v, qt, ktt, qh, kth, gam),
                                (pt_all, rt_all, v_all, qt_all, ktt_all, qh_all, kth_all, gam_all)):
                lst.append(arr[:, sl])
            where.append((c, tok, sl))
    heads = range(len(where))
    a4 = [_mm(jnp.concatenate([pt[h], rt[h]], 0), jnp.concatenate([qt[h], ktt[h]], 0),
              ((1,), (1,))) for h in heads]
    nmat = [strict * a4[h][:n, :n] for h in heads]
    akv = [_mm(strict * a4[h][:n, n:], v[h]) for h in heads]
    nd = [same_block[0] * nmat[h] for h in heads]
    x = [_mm(nd[h], nd[h]) for h in heads]
    m = [eye - nd[h] for h in heads]
    m = [m[h] + _mm(m[h], x[h]) for h in heads]
    x = [_mm(x[h], x[h]) for h in heads]
    m = [m[h] + _mm(m[h], x[h]) for h in heads]
    for lvl in range(1, len(same_block)):
        ring = same_block[lvl] - same_block[lvl - 1]
        t = [_mm(m[h], ring * nmat[h]) for h in heads]
        m = [m[h] - _mm(t[h], m[h]) for h in heads]
    wu = [_mm(m[h], jnp.concatenate([pt[h], akv[h]], 1)) for h in heads]
    gh = [_mm(wu[h], qh[h], ((0,), (0,))) for h in heads]
    vk = [_mm(v[h], kth[h], ((0,), (0,))) for h in heads]
    lwu = [_mm(incl * a4[h][n:, :n], wu[h]) for h in heads]
    lv = [_mm(incl * a4[h][n:, n:], v[h]) for h in heads]
    for h in heads:
        c, tok, sl = where[h]
        g_ref[0, 0, c, :, sl] = eye * gam[h] - gh[h][:hd]
        h_ref[0, 0, c, :, sl] = vk[h] - gh[h][hd:]
        rh_ref[0, 0, tok, sl] = rt[h] - lwu[h][:, :hd]
        y0_ref[0, 0, tok, sl] = lv[h] - lwu[h][:, hd:]


def _rw_chunks(r, v, kh, lw, q, kt):
    bsz, n, rw = r.shape
    nc = n // RW_CHUNK
    per = RW_CHUNKS_PER_STEP
    tok = pl.BlockSpec((1, per * RW_CHUNK, rw), lambda d, b, c: (b, c, 0))
    dtok = pl.BlockSpec((1, 1, per * RW_CHUNK, rw), lambda d, b, c: (d, b, c, 0))
    mat = pl.BlockSpec((1, 1, per, RW_HEAD, rw), lambda d, b, c: (d, b, c, 0, 0))
    return pl.pallas_call(
        _rw_chunk_kernel,
        grid=(N_DIR, bsz, nc // per),
        in_specs=[tok, tok, tok, dtok, dtok, dtok],
        out_specs=[mat, mat, dtok, dtok],
        out_shape=[jax.ShapeDtypeStruct((N_DIR, bsz, nc, RW_HEAD, rw), F32)] * 2
        + [jax.ShapeDtypeStruct((N_DIR, bsz, n, rw), F32)] * 2,
        compiler_params=_params("parallel", "parallel", "parallel"),
        name="rw_chunks",
    )(r, v, kh, lw, q, kt)


def _rw_state_kernel(emit_y, s0_ref, gf_ref, gb_ref, hf_ref, hb_ref, rhf_ref, rhb_ref, y0f_ref,
                     y0b_ref, *rest):
    if emit_y:
        yf_ref, yb_ref, sfin_ref, s_scr = rest
    else:
        sfin_ref, s_scr = rest
    c = pl.program_id(0)
    hd = RW_HEAD
    n_dir, bsz = s_scr.shape[0], s_scr.shape[1]

    @pl.when(c == 0)
    def _():
        s_scr[...] = s0_ref[...]

    sls = [slice(h * hd, (h + 1) * hd) for h in range(s_scr.shape[-1] // hd)]
    g_refs, h_refs = (gf_ref, gb_ref), (hf_ref, hb_ref)
    rh_refs, y0_refs = (rhf_ref, rhb_ref), (y0f_ref, y0b_ref)
    chains = [(d, b) for d in range(n_dir) for b in range(bsz)]
    s_bf = {k: s_scr[k[0], k[1]].astype(BF16) for k in chains}
    g_bf = {k: g_refs[k[0]][0, k[1], 0].astype(BF16) for k in chains}
    s_new = {k: [_mm(s_bf[k][:, sl], g_bf[k][:, sl]) for sl in sls] for k in chains}
    if emit_y:
        y_refs = (yf_ref, yb_ref)
        rh_bf = {k: rh_refs[k[0]][0, k[1]].astype(BF16) for k in chains}
        y = {k: [_mm(rh_bf[k][:, sl], s_bf[k][:, sl], ((1,), (1,))) for sl in sls] for k in chains}
        for d, b in chains:
            y_refs[d][b] = jnp.concatenate(y[d, b], axis=-1) + y0_refs[d][0, b]
    for d, b in chains:
        s_scr[d, b] = jnp.concatenate(s_new[d, b], axis=-1) + h_refs[d][0, b, 0]

    @pl.when(c == pl.num_programs(0) - 1)
    def _():
        sfin_ref[...] = s_scr[...]


def _rw_state(s0, gmat, hmat, rh, y0, emit_y):
    n_dir, bsz, nc, hd, rw = gmat.shape
    n = rh.shape[2]
    chunk = lambda d, c: c + d * (nc - 1 - 2 * c)
    mat = lambda d: pl.BlockSpec((1, bsz, 1, hd, rw), lambda c: (d, 0, chunk(d, c), 0, 0))
    dtok = lambda d: pl.BlockSpec((1, bsz, RW_CHUNK, rw), lambda c: (d, 0, chunk(d, c), 0))
    st = pl.BlockSpec((n_dir, bsz, hd, rw), lambda c: (0, 0, 0, 0))
    out_specs = [st]
    out_shape = [jax.ShapeDtypeStruct((n_dir, bsz, hd, rw), F32)]
    if emit_y:
        out_specs = [pl.BlockSpec((bsz, RW_CHUNK, rw), lambda c, d=d: (0, chunk(d, c), 0))
                     for d in range(n_dir)] + out_specs
        out_shape = [jax.ShapeDtypeStruct((bsz, n, rw), F32)] * n_dir + out_shape
    return pl.pallas_call(
        functools.partial(_rw_state_kernel, emit_y),
        grid=(nc,),
        in_specs=[st, mat(0), mat(1), mat(0), mat(1), dtok(0), dtok(1), dtok(0), dtok(1)],
        out_specs=out_specs,
        out_shape=out_shape,
        scratch_shapes=[pltpu.VMEM((n_dir, bsz, hd, rw), F32)],
        compiler_params=_params("arbitrary"),
        name="rw_state_y" if emit_y else "rw_state",
    )(s0, gmat, gmat, hmat, hmat, rh, rh, y0, y0)


def _outproj_kernel(x_ref, y5_ref, yf_ref, yb_ref, bv_ref, g_ref, lnw_ref, lnb_ref, hsel_ref,
                    hselt_ref, wo_ref, gt1_ref, g2_ref, sc2_ref, sh2_ref, rw_ref, rb_ref,
                    x1_ref, f_ref, idx_ref, gate_ref, rank_ref, cnt_ref, carry):
    first = jnp.logical_and(pl.program_id(0) == 0, pl.program_id(1) == 0)

    @pl.when(first)
    def _():
        carry[...] = jnp.zeros_like(carry)

    inv_hd = 1.0 / RW_HEAD
    y = yf_ref[0] + yb_ref[0]
    mean = _mm_sel(_mm_sel(y, hsel_ref[...]), hselt_ref[...]) * inv_hd
    yc = y - mean
    var = _mm_sel(_mm_sel(yc * yc, hsel_ref[...]), hselt_ref[...]) * inv_hd
    yn = yc * lax.rsqrt(var + RW_GN_EPS) * lnw_ref[...] + lnb_ref[...]
    yr = (yn + bv_ref[0]) * g_ref[0]
    mix = _mm(jnp.concatenate([y5_ref[0], yr], axis=-1), wo_ref[...])
    x1 = x_ref[0] + gt1_ref[0] * mix
    x1_ref[0] = x1
    f = _rms_mod(x1, g2_ref[...], sc2_ref[0], sh2_ref[0])
    n_lines = f.shape[1] // LANES
    for s in range(n_lines):
        f_ref[0, pl.ds(s, f.shape[0], stride=n_lines), :] = f[:, s * LANES:(s + 1) * LANES]

    logits = _mm_f32(f, rw_ref[...]) + rb_ref[...]
    tm, ne = logits.shape
    eid = lax.broadcasted_iota(jnp.int32, (tm, ne), 1)
    work = logits
    sel = jnp.zeros((tm, ne), F32)
    idx_cols, val_cols = [], []
    for _ in range(TOP_K):
        top = jnp.max(work, axis=-1, keepdims=True)
        pick = jnp.min(jnp.where(work == top, eid, ne), axis=-1, keepdims=True)
        hit = eid == pick
        sel = jnp.where(hit, 1.0, sel)
        work = jnp.where(hit, -jnp.inf, work)
        idx_cols.append(pick)
        val_cols.append(top)
    exps = [jnp.exp(vk - val_cols[0]) for vk in val_cols]
    denom = exps[0] + exps[1] + exps[2] + exps[3]
    row = lax.broadcasted_iota(jnp.int32, (tm, tm), 0)
    colm = lax.broadcasted_iota(jnp.int32, (tm, tm), 1)
    before = _mm((colm < row).astype(F32), sel) + carry[0:1, :]
    rank_cols = [jnp.sum(jnp.where(eid == ic, before, 0.0), axis=-1, keepdims=True)
                 for ic in idx_cols]
    idx_ref[0] = jnp.concatenate(idx_cols, axis=-1)
    gate_ref[0] = jnp.concatenate([e / denom for e in exps], axis=-1)
    rank_ref[0] = jnp.concatenate(rank_cols, axis=-1).astype(jnp.int32)
    total = carry[0:1, :] + jnp.sum(sel, axis=0, keepdims=True)
    carry[...] = jnp.broadcast_to(total, carry.shape)
    cnt_ref[...] = jnp.broadcast_to(total, cnt_ref.shape).astype(jnp.int32)


def _outproj(x, y5, yf, yb, bv, g, ln_w, ln_b, hsel, hselt, wo, gt1, g2, sc2, sh2, router_w,
             router_b):
    bsz, n, d = x.shape
    rw = y5.shape[-1]
    ne = router_w.shape[-1]
    tm = ROW_TILE
    row = lambda w: pl.BlockSpec((1, tm, w), lambda b, i: (b, i, 0))
    vec = lambda w: pl.BlockSpec((1, w), lambda b, i: (0, 0))
    bvec = pl.BlockSpec((1, 1, d), lambda b, i: (b, 0, 0))
    full = lambda a: pl.BlockSpec(a.shape, lambda b, i: (0,) * a.ndim)
    return pl.pallas_call(
        _outproj_kernel,
        grid=(bsz, n // tm),
        in_specs=[row(d), row(rw), row(rw), row(rw),
                  row(rw), row(rw), vec(rw), vec(rw), full(hsel), full(hselt),
                  pl.BlockSpec(wo.shape, lambda b, i: (0, 0), pipeline_mode=pl.Buffered(1)),
                  bvec, vec(d), bvec, bvec, full(router_w), vec(ne)],
        out_specs=[row(d), pl.BlockSpec((1, tm * (d // LANES), LANES), lambda b, i: (b, i, 0)),
                   row(TOP_K), row(TOP_K), row(TOP_K),
                   pl.BlockSpec((8, ne), lambda b, i: (0, 0))],
        out_shape=[jax.ShapeDtypeStruct((bsz, n, d), F32),
                   jax.ShapeDtypeStruct((bsz, n * (d // LANES), LANES), F32),
                   jax.ShapeDtypeStruct((bsz, n, TOP_K), jnp.int32),
                   jax.ShapeDtypeStruct((bsz, n, TOP_K), F32),
                   jax.ShapeDtypeStruct((bsz, n, TOP_K), jnp.int32),
                   jax.ShapeDtypeStruct((8, ne), jnp.int32)],
        scratch_shapes=[pltpu.VMEM((8, ne), F32)],
        compiler_params=_params("arbitrary", "arbitrary"),
        name="outproj_router",
    )(x, y5, yf, yb, bv, g, ln_w.reshape(1, rw), ln_b.reshape(1, rw), hsel, hselt, wo, gt1,
      g2.reshape(1, d), sc2, sh2, router_w, router_b.reshape(1, ne))


def _dispatch_kernel(tile_rows, idx_ref, rank_ref, start_ref, fill_ref, f_ref, xs_ref, zeros, sem,
                     sem_fill):
    tr = tile_rows
    tm = f_ref.shape[0] // tr
    blk = zeros.shape[0]

    @pl.when(pl.program_id(0) == 0)
    def _():
        zeros[...] = jnp.zeros_like(zeros)

        def fill(b):
            dst = xs_ref.at[pl.ds(pl.multiple_of(b * blk, blk), blk)]
            return pltpu.make_async_copy(zeros, dst, sem_fill)

        def start(b, carry):
            @pl.when(fill_ref[b] == 1)
            def _():
                fill(b).start()
            return carry

        def wait(b, carry):
            @pl.when(fill_ref[b] == 1)
            def _():
                fill(b).wait()
            return carry

        lax.fori_loop(0, fill_ref.shape[0], start, 0)
        lax.fori_loop(0, fill_ref.shape[0], wait, 0)

    def issue(t, _):
        src = f_ref.at[pl.ds(pl.multiple_of(t * tr, tr), tr)]
        for k in range(TOP_K):
            a = t * TOP_K + k
            dst = start_ref[idx_ref[a]] + rank_ref[a]
            pltpu.make_async_copy(src, xs_ref.at[pl.ds(pl.multiple_of(dst * tr, tr), tr)],
                                  sem).start()
        return 0

    lax.fori_loop(0, tm, issue, 0)
    for _ in range(TOP_K):
        pltpu.make_async_copy(f_ref, xs_ref.at[pl.ds(0, tm * tr)], sem).wait()


def _dispatch(idx_flat, rank_flat, start, fill, f_lines, n_tok, n_slots):
    lanes = f_lines.shape[1]
    tr = f_lines.shape[0] // n_tok
    tm = ROW_TILE
    smem_tok = pl.BlockSpec((tm * TOP_K,), lambda i: (i,), memory_space=pltpu.SMEM)
    smem_all = lambda a: pl.BlockSpec(a.shape, lambda i: (0,), memory_space=pltpu.SMEM)
    return pl.pallas_call(
        functools.partial(_dispatch_kernel, tr),
        grid=(n_tok // tm,),
        in_specs=[smem_tok, smem_tok, smem_all(start), smem_all(fill),
                  pl.BlockSpec((tm * tr, lanes), lambda i: (i, 0))],
        out_specs=pl.BlockSpec(memory_space=pl.ANY),
        out_shape=jax.ShapeDtypeStruct((n_slots * tr, lanes), f_lines.dtype),
        scratch_shapes=[pltpu.VMEM((MOE_ROWS * tr, lanes), f_lines.dtype),
                        pltpu.SemaphoreType.DMA(()), pltpu.SemaphoreType.DMA(())],
        compiler_params=_params("arbitrary"),
        name="moe_dispatch",
    )(idx_flat, rank_flat, start, fill, f_lines)


def _moe_kernel(sbe_ref, sbrow_ref, sbn_ref, nsb_ref, xs_ref, wg_ref, wl_ref, wd_ref, bg_ref,
                bl_ref, bd_ref, y_ref, x_stage, x_bf, acc, wg_bf, wl_bf, wd_bf, y_stage, sem_in,
                sem_out):
    sb, j = pl.program_id(0), pl.program_id(1)
    nj = pl.num_programs(1)
    rb = MOE_ROWS
    n_tiles = y_stage.shape[1] // rb
    valid = sb < nsb_ref[0]

    def copy_in(row, slot):
        src = xs_ref.at[pl.ds(pl.multiple_of(row * n_tiles, rb * n_tiles), rb * n_tiles)]
        return pltpu.make_async_copy(src, x_stage.at[slot], sem_in.at[slot])

    def copy_out(row, slot):
        dst = y_ref.at[pl.ds(pl.multiple_of(row * n_tiles, rb * n_tiles), rb * n_tiles)]
        return pltpu.make_async_copy(y_stage.at[slot], dst, sem_out.at[slot])

    @pl.when(valid)
    def _():
        n_blk = sbn_ref[sb]
        row0 = sbrow_ref[sb] * rb

        @pl.when(j == 0)
        def _():
            copy_in(row0, 0).start()

            def load(k, carry):
                slot = k % 2

                @pl.when(k + 1 < n_blk)
                def _():
                    copy_in(row0 + (k + 1) * rb, 1 - slot).start()

                copy_in(row0 + k * rb, slot).wait()
                r = pl.multiple_of(k * rb, rb)
                for s in range(n_tiles):
                    x_bf[pl.ds(r, rb), s * LANES:(s + 1) * LANES] = (
                        x_stage[slot, pl.ds(s, rb, stride=n_tiles), :].astype(BF16))
                return carry

            lax.fori_loop(0, n_blk, load, 0)

            def clear(k, carry):
                acc[pl.ds(pl.multiple_of(k * rb, rb), rb), :] = jnp.broadcast_to(
                    bd_ref[0], (rb, acc.shape[1]))
                return carry

            lax.fori_loop(0, n_blk, clear, 0)

        wg_bf[...] = wg_ref[0].astype(BF16)
        wl_bf[...] = wl_ref[0].astype(BF16)
        wd_bf[...] = wd_ref[0].astype(BF16)

        def accumulate(r):
            x = x_bf[r, :]
            glu = jnp.dot(x, wg_bf[...], preferred_element_type=F32) + bg_ref[0]
            lin = jnp.dot(x, wl_bf[...], preferred_element_type=F32) + bl_ref[0]
            glu = jnp.minimum(glu, SWIGLU_LIMIT)
            lin = jnp.clip(lin, -SWIGLU_LIMIT, SWIGLU_LIMIT)
            act = ((lin + 1.0) * glu * jax.nn.sigmoid(SWIGLU_ALPHA * glu)).astype(BF16)
            acc[r, :] += jnp.dot(act, wd_bf[...], preferred_element_type=F32)

        def row_pair(kk, carry):
            accumulate(pl.ds(pl.multiple_of(kk * 2 * rb, 2 * rb), 2 * rb))
            return carry

        lax.fori_loop(0, n_blk // 2, row_pair, 0)

        @pl.when(n_blk % 2 == 1)
        def _():
            accumulate(pl.ds(pl.multiple_of((n_blk - 1) * rb, rb), rb))

        @pl.when(j == nj - 1)
        def _():
            def store(k, carry):
                slot = k % 2

                @pl.when(k >= 2)
                def _():
                    copy_out(row0, slot).wait()

                r = pl.multiple_of(k * rb, rb)
                for s in range(n_tiles):
                    y_stage[slot, pl.ds(s, rb, stride=n_tiles), :] = (
                        acc[pl.ds(r, rb), s * LANES:(s + 1) * LANES])
                copy_out(row0 + k * rb, slot).start()
                return carry

            lax.fori_loop(0, n_blk, store, 0)

            @pl.when(n_blk >= 2)
            def _():
                copy_out(row0, n_blk % 2).wait()

            copy_out(row0, (n_blk - 1) % 2).wait()

    @pl.when(jnp.logical_not(valid))
    def _():
        q = (sb - nsb_ref[0]) * nj + j
        last = jnp.maximum(nsb_ref[0] - 1, 0)
        first_unused = sbrow_ref[last] + sbn_ref[last]
        n_unused = y_ref.shape[0] // (rb * n_tiles) - first_unused

        @pl.when(q < n_unused)
        def _():
            y_stage[0] = jnp.zeros(y_stage.shape[1:], y_stage.dtype)
            cp = copy_out((first_unused + q) * rb, 0)
            cp.start()
            cp.wait()


def _moe_experts(sb_e, sb_row, sb_n, n_sb, xs_lines, w_gu, b_gu, w_dn, b_dn):
    ne, d, two_de = w_gu.shape
    lanes = xs_lines.shape[1]
    n_slots = xs_lines.shape[0] // (d // lanes)
    de = two_de // 2
    th = MOE_HIDDEN_TILE
    nj = de // th
    n_groups = sb_e.shape[0]
    grp = lambda sb, nsb: jnp.maximum(jnp.minimum(sb, nsb[0] - 1), 0)
    tile = lambda sb, j, nsb: jnp.where(sb < nsb[0], j, nj - 1)
    grid_spec = pltpu.PrefetchScalarGridSpec(
        num_scalar_prefetch=4,
        grid=(n_groups, nj),
        in_specs=[pl.BlockSpec(memory_space=pl.ANY),
                  pl.BlockSpec((1, d, th), lambda sb, j, e, r, n, nsb:
                               (e[grp(sb, nsb)], 0, tile(sb, j, nsb))),
                  pl.BlockSpec((1, d, th), lambda sb, j, e, r, n, nsb:
                               (e[grp(sb, nsb)], 0, nj + tile(sb, j, nsb))),
                  pl.BlockSpec((1, th, d), lambda sb, j, e, r, n, nsb:
                               (e[grp(sb, nsb)], tile(sb, j, nsb), 0)),
                  pl.BlockSpec((1, 1, th), lambda sb, j, e, r, n, nsb:
                               (e[grp(sb, nsb)], 0, tile(sb, j, nsb))),
                  pl.BlockSpec((1, 1, th), lambda sb, j, e, r, n, nsb:
                               (e[grp(sb, nsb)], 0, nj + tile(sb, j, nsb))),
                  pl.BlockSpec((1, 1, d), lambda sb, j, e, r, n, nsb: (e[grp(sb, nsb)], 0, 0))],
        out_specs=pl.BlockSpec(memory_space=pl.ANY),
        scratch_shapes=[pltpu.VMEM((2, MOE_ROWS * (d // lanes), lanes), F32),
                        pltpu.VMEM((MOE_GROUP_ROWS, d), BF16),
                        pltpu.VMEM((MOE_GROUP_ROWS, d), F32),
                        pltpu.VMEM((d, th), BF16), pltpu.VMEM((d, th), BF16),
                        pltpu.VMEM((th, d), BF16),
                        pltpu.VMEM((2, MOE_ROWS * (d // lanes), lanes), F32),
                        pltpu.SemaphoreType.DMA((2,)), pltpu.SemaphoreType.DMA((2,))])
    return pl.pallas_call(
        _moe_kernel,
        grid_spec=grid_spec,
        out_shape=jax.ShapeDtypeStruct((n_slots * (d // lanes), lanes), F32),
        compiler_params=_params("arbitrary", "arbitrary"),
        name="moe_experts",
    )(sb_e, sb_row, sb_n, n_sb, xs_lines, w_gu, w_gu, w_dn, b_gu.reshape(ne, 1, two_de),
      b_gu.reshape(ne, 1, two_de), b_dn.reshape(ne, 1, d))


def _combine_kernel(idx_ref, rank_ref, start_ref, x1_ref, gate_ref, gt2_ref, fg_ref, y_ref,
                    o_ref, buf, sem):
    tm, d = x1_ref.shape
    tr = d // LANES

    def issue(t, _):
        dst = pl.ds(pl.multiple_of(t * tr, tr), tr)
        for k in range(TOP_K):
            a = t * TOP_K + k
            src = start_ref[idx_ref[a]] + rank_ref[a]
            pltpu.make_async_copy(y_ref.at[pl.ds(pl.multiple_of(src * tr, tr), tr)],
                                  buf.at[k, dst], sem).start()
        return 0

    lax.fori_loop(0, tm, issue, 0)
    for k in range(TOP_K):
        pltpu.make_async_copy(y_ref.at[pl.ds(0, tm * tr)], buf.at[k], sem).wait()
    gate = gate_ref[...]
    cols = []
    for s in range(tr):
        acc = gate[:, 0:1] * buf[0, pl.ds(s, tm, stride=tr), :]
        for k in range(1, TOP_K):
            acc = acc + gate[:, k:k + 1] * buf[k, pl.ds(s, tm, stride=tr), :]
        cols.append(acc)
    x2 = x1_ref[...] + gt2_ref[0] * jnp.concatenate(cols, axis=-1)
    ms = jnp.mean(x2 * x2, axis=-1, keepdims=True)
    o_ref[...] = x2 * lax.rsqrt(ms + NORM_EPS) * fg_ref[...]


def _combine(idx_flat, rank_flat, start, x1, gate4, gt2, fg, y_lines, tiles_per_batch):
    n_tok, d = x1.shape
    lanes = y_lines.shape[1]
    tm = ROW_TILE
    smem_tok = pl.BlockSpec((tm * TOP_K,), lambda i: (i,), memory_space=pltpu.SMEM)
    return pl.pallas_call(
        _combine_kernel,
        grid=(n_tok // tm,),
        in_specs=[smem_tok, smem_tok,
                  pl.BlockSpec(start.shape, lambda i: (0,), memory_space=pltpu.SMEM),
                  pl.BlockSpec((tm, d), lambda i: (i, 0)),
                  pl.BlockSpec((tm, TOP_K), lambda i: (i, 0)),
                  pl.BlockSpec((1, 1, d), lambda i: (i // tiles_per_batch, 0, 0)),
                  pl.BlockSpec((1, d), lambda i: (0, 0)),
                  pl.BlockSpec(memory_space=pl.ANY)],
        out_specs=pl.BlockSpec((tm, d), lambda i: (i, 0)),
        out_shape=jax.ShapeDtypeStruct((n_tok, d), F32),
        scratch_shapes=[pltpu.VMEM((TOP_K, tm * (d // lanes), lanes), F32),
                        pltpu.SemaphoreType.DMA(())],
        compiler_params=_params("arbitrary"),
        name="moe_combine_final",
    )(idx_flat, rank_flat, start, x1, gate4, gt2, fg.reshape(1, d), y_lines)


def _block_diag2(w):
    z = jnp.zeros_like(w[0])
    return jnp.concatenate([jnp.concatenate([w[0], z], 1), jnp.concatenate([z, w[1]], 1)], 0)


def kernel(x, c, ctx, c_ctx, mod_w, mod_b, norm1_g, w_in, s5_a_re, s5_a_im, s5_log_dt, s5_b_re,
           s5_b_im, s5_c_re, s5_c_im, s5_d, s5_glu_w, s5_glu_b, rw_mu, rw_w0, rw_w2, rw_a0, rw_a2,
           rw_g2, rw_k_k, rw_k_a, rw_r_k, rw_ln_w, rw_ln_b, w_out, norm2_g, router_w, router_b,
           exp_w_gu, exp_b_gu, exp_w_dn, exp_b_dn, final_g):
    assert mod_w.shape[0] == 1, "single-layer stack only"
    bsz, n_lat, d = x.shape
    n_ctx = ctx.shape[1]
    assert bsz == 2 and n_ctx % ROW_TILE == 0 and n_lat % ROW_TILE == 0
    s5w = s5_d.shape[-1]
    rww = rw_k_k.shape[-1]
    n_heads = rww // RW_HEAD
    ne = router_w.shape[-1]

    cond8 = jnp.zeros((8, d), F32).at[:bsz].set(c).at[bsz].set(c_ctx)
    mod = _adaln(cond8, mod_w[0], mod_b[0])
    sh1, sc1, gt1, sh2, sc2, gt2 = [m[:bsz, None, :] for m in jnp.split(mod, 6, axis=-1)]
    csh1, csc1 = [jnp.broadcast_to(m[bsz][None, None, :], (bsz, 1, d))
                  for m in jnp.split(mod, 6, axis=-1)[:2]]

    wu = w_in[0][:, :s5w].astype(BF16)
    wz = w_in[0][:, s5w:].astype(BF16)
    u_lat, z_lat = _inproj(x, norm1_g[0], sc1, sh1, wu, wz)
    u_ctx, z_ctx = _inproj(ctx, norm1_g[0], csc1, csh1, wu, wz)

    s5c = _s5_constants(s5_a_re[0], s5_a_im[0], s5_log_dt[0], s5_b_re[0], s5_b_im[0], s5_c_re[0],
                        s5_c_im[0], s5_d[0], s5_glu_w[0], s5_glu_b[0])
    y5 = _s5(jnp.concatenate([u_ctx, u_lat], axis=1), s5c, n_ctx)

    lanes_idx = jnp.arange(rww) // RW_HEAD
    hsel = (lanes_idx[:, None] == jnp.arange(LANES)[None, :]).astype(BF16)
    hselt = hsel.T
    feat_consts = (rw_mu[0], rw_w0[0].reshape(1, N_DIR * rww),
                   _block_diag2(rw_w2[0]).astype(BF16), rw_a0[0].reshape(1, N_DIR * rww),
                   _block_diag2(rw_a2[0]).astype(BF16), rw_g2[0].astype(BF16),
                   rw_k_k[0].reshape(1, rww), rw_k_a[0].reshape(1, rww),
                   rw_r_k[0].reshape(1, rww), hsel, hselt)
    fc = _rw_features(z_ctx, n_ctx, False, *feat_consts)
    fl = _rw_features(z_lat, GRID_W, True, *feat_consts)

    def scan_inputs(f):
        r, v, kh, _, _, lw, q, kt = f
        return r, v, kh, lw, q, kt

    gc, hc, rhc, y0c = _rw_chunks(*scan_inputs(fc))
    gl, hl, rhl, y0l = _rw_chunks(*scan_inputs(fl))
    s_zero = jnp.zeros((N_DIR, bsz, RW_HEAD, rww), F32)
    (s_ctx,) = _rw_state(s_zero, gc, hc, rhc, y0c, emit_y=False)
    y_f, y_b, _ = _rw_state(s_ctx, gl, hl, rhl, y0l, emit_y=True)

    x1, f, idx4, gate4, rank4, counts = _outproj(
        x, y5, y_f, y_b, fl[4], fl[3], rw_ln_w[0], rw_ln_b[0], hsel, hselt, w_out[0].astype(BF16),
        gt1, norm2_g[0], sc2, sh2, router_w[0], router_b[0])

    n_tok = bsz * n_lat
    cnt = counts[0]
    padded = (cnt + MOE_ROWS - 1) // MOE_ROWS * MOE_ROWS
    pend = jnp.cumsum(padded)
    start = (pend - padded).astype(jnp.int32)
    nb = n_tok * TOP_K // MOE_ROWS + ne
    n_slots = nb * MOE_ROWS
    blocks_per_group = MOE_GROUP_ROWS // MOE_ROWS
    n_groups_max = n_slots // MOE_GROUP_ROWS + ne
    groups_e = (padded + MOE_GROUP_ROWS - 1) // MOE_GROUP_ROWS
    groups_end = jnp.cumsum(groups_e)
    gidx = jnp.arange(n_groups_max)
    sb_e = jnp.minimum(jnp.sum(groups_end[None, :] <= gidx[:, None], axis=1), ne - 1)
    local = gidx - (groups_end - groups_e)[sb_e]
    sb_row = start[sb_e] // MOE_ROWS + local * blocks_per_group
    sb_n = jnp.clip(padded[sb_e] // MOE_ROWS - local * blocks_per_group, 0, blocks_per_group)
    n_sb = groups_end[-1].astype(jnp.int32).reshape(1)
    idx_flat = idx4.reshape(-1)
    rank_flat = rank4.reshape(-1)

    blk_end = (jnp.arange(nb) + 1) * MOE_ROWS
    fill = jnp.logical_or(jnp.any(blk_end[:, None] == pend[None, :], axis=1),
                          blk_end > pend[-1]).astype(jnp.int32)
    xs = _dispatch(idx_flat, rank_flat, start, fill, f.reshape(-1, LANES), n_tok, n_slots)
    y3 = _moe_experts(sb_e.astype(jnp.int32), sb_row.astype(jnp.int32), sb_n.astype(jnp.int32),
                      n_sb, xs, exp_w_gu[0], exp_b_gu[0], exp_w_dn[0], exp_b_dn[0])
    out = _combine(idx_flat, rank_flat, start, x1.reshape(n_tok, d), gate4.reshape(n_tok, TOP_K),
                   gt2, final_g, y3, n_lat // ROW_TILE)
    return out.reshape(bsz, n_lat, d)
```

```python
import functools
import math

import jax
import jax.numpy as jnp
from jax import lax
from jax.experimental import pallas as pl
from jax.experimental.pallas import tpu as pltpu

F32 = jnp.float32
BF16 = jnp.bfloat16
HIGHEST = lax.Precision.HIGHEST

LANES = 128
VMEM_LIMIT_BYTES = 56 * 1024 * 1024

NORM_EPS = 1e-5
N_DIR = 2
S5_GROUP = 16
S5_STATE = 64
S5_CHUNK = 8
S5_LANE_GROUPS = LANES // S5_GROUP
S5_SCAN_ROWS = 8
RW_HEAD = 64
RW_CHUNK = 64
RW_CHUNKS_PER_STEP = 2
RW_GN_EPS = 64e-5
GRID_W = 64
TOP_K = 4
SWIGLU_ALPHA = 1.702
SWIGLU_LIMIT = 7.0
ROW_TILE = 256
MOE_ROWS = 256
MOE_GROUP_ROWS = 1536
MOE_HIDDEN_TILE = 256


def _params(*sem):
    return pltpu.CompilerParams(dimension_semantics=sem, vmem_limit_bytes=VMEM_LIMIT_BYTES)


def _mm(a, b, dims=((1,), (0,))):
    return lax.dot_general(a.astype(BF16), b.astype(BF16), (dims, ((), ())),
                           preferred_element_type=F32)


def _split_bf16(a):
    hi = a.astype(BF16)
    return hi, (a - hi.astype(F32)).astype(BF16)


def _mm_sel(a, sel):
    hi, lo = _split_bf16(a)
    m = a.shape[0]
    out = jnp.dot(jnp.concatenate([hi, lo], axis=0), sel.astype(BF16), preferred_element_type=F32)
    return out[:m] + out[m:]


def _sel_mm(sel, a):
    hi, lo = _split_bf16(a)
    n = a.shape[1]
    out = jnp.dot(sel.astype(BF16), jnp.concatenate([hi, lo], axis=1), preferred_element_type=F32)
    return out[:, :n] + out[:, n:]


def _mm_f32(a, b, dims=((1,), (0,))):
    return lax.dot_general(a, b, (dims, ((), ())), precision=HIGHEST,
                           preferred_element_type=F32)


def _adaln_kernel(cond_ref, w_ref, b_ref, o_ref):
    c = cond_ref[...]
    o_ref[...] = _mm_f32(c * jax.nn.sigmoid(c), w_ref[...]) + b_ref[...]


def _adaln(cond8, w, b):
    d, n = w.shape
    tn = 1536
    return pl.pallas_call(
        _adaln_kernel,
        grid=(n // tn,),
        in_specs=[pl.BlockSpec((8, d), lambda j: (0, 0)),
                  pl.BlockSpec((d, tn), lambda j: (0, j)),
                  pl.BlockSpec((1, tn), lambda j: (0, j))],
        out_specs=pl.BlockSpec((8, tn), lambda j: (0, j)),
        out_shape=jax.ShapeDtypeStruct((8, n), F32),
        compiler_params=_params("parallel"),
        name="adaln",
    )(cond8, w, b.reshape(1, n))


def _rms_mod(x, g, sc, sh):
    y = x * lax.rsqrt(jnp.mean(x * x, axis=-1, keepdims=True) + NORM_EPS)
    return (y * g) * (1.0 + sc) + sh


def _inproj_kernel(x_ref, g_ref, sc_ref, sh_ref, wu_ref, wz_ref, u_ref, z_ref):
    h = _rms_mod(x_ref[0], g_ref[...], sc_ref[0], sh_ref[0]).astype(BF16)
    u_ref[0] = jnp.dot(h, wu_ref[...], preferred_element_type=F32)
    z_ref[0] = jnp.dot(h, wz_ref[...], preferred_element_type=F32)


def _inproj(x, g, sc, sh, wu, wz):
    bsz, n, d = x.shape
    nu, nz = wu.shape[1], wz.shape[1]
    tm = ROW_TILE
    const = dict(pipeline_mode=pl.Buffered(1))
    return pl.pallas_call(
        _inproj_kernel,
        grid=(bsz, n // tm),
        in_specs=[pl.BlockSpec((1, tm, d), lambda b, i: (b, i, 0)),
                  pl.BlockSpec((1, d), lambda b, i: (0, 0)),
                  pl.BlockSpec((1, 1, d), lambda b, i: (b, 0, 0)),
                  pl.BlockSpec((1, 1, d), lambda b, i: (b, 0, 0)),
                  pl.BlockSpec((d, nu), lambda b, i: (0, 0), **const),
                  pl.BlockSpec((d, nz), lambda b, i: (0, 0), **const)],
        out_specs=[pl.BlockSpec((1, tm, nu), lambda b, i: (b, i, 0)),
                   pl.BlockSpec((1, tm, nz), lambda b, i: (b, i, 0))],
        out_shape=[jax.ShapeDtypeStruct((bsz, n, nu), F32),
                   jax.ShapeDtypeStruct((bsz, n, nz), F32)],
        compiler_params=_params("parallel", "parallel"),
        name="inproj",
    )(x, g.reshape(1, d), sc, sh, wu, wz)


def _s5_constants(a_re, a_im, log_dt, b_re, b_im, c_re, c_im, d_skip, glu_w, glu_b):
    t = S5_CHUNK
    g, p = a_re.shape[1], a_re.shape[2]
    h = S5_GROUP
    gl = S5_LANE_GROUPS
    no = g // gl
    n = jnp.arange(t + 1, dtype=F32)[None, None, :, None]
    ar, ai = a_re.astype(F32)[:, :, None, :], a_im.astype(F32)[:, :, None, :]
    dt = jnp.exp(log_dt.astype(F32))[:, :, None, None]
    mag = jnp.exp(n * (ar * dt))
    pr, pi = mag * jnp.cos(n * (ai * dt)), mag * jnp.sin(n * (ai * dt))
    lr, li = pr[:, :, 1], pi[:, :, 1]
    ar, ai = ar[:, :, 0], ai[:, :, 0]
    den = ar * ar + ai * ai
    cf_re = (((lr - 1.0) * ar + li * ai) / den)[:, :, None, :]
    cf_im = ((li * ar - (lr - 1.0) * ai) / den)[:, :, None, :]
    br = jnp.swapaxes(b_re.astype(F32), 2, 3)
    bi = jnp.swapaxes(b_im.astype(F32), 2, 3)
    bb_re = cf_re * br - cf_im * bi
    bb_im = cf_re * bi + cf_im * br
    cr = jnp.swapaxes(c_re.astype(F32), 2, 3)[:, :, :, None, :]
    ci = jnp.swapaxes(c_im.astype(F32), 2, 3)[:, :, :, None, :]
    prt = jnp.swapaxes(pr, 2, 3)[..., None]
    pit = jnp.swapaxes(pi, 2, 3)[..., None]
    cl_re = cr * prt - ci * pit
    cl_im = cr * pit + ci * prt
    dims = (((2,), (1,)), ((0,), (0,)))
    flat = lambda a: a.reshape((N_DIR * g,) + a.shape[2:])
    lag = (lax.dot_general(flat(bb_re), flat(cl_re[:, :, :, :t]).reshape(N_DIR * g, p, t * h),
                           dims, precision=HIGHEST)
           - lax.dot_general(flat(bb_im), flat(cl_im[:, :, :, :t]).reshape(N_DIR * g, p, t * h),
                             dims, precision=HIGHEST)).reshape(N_DIR, g, h, t, h)
    s_idx = jnp.arange(t)[:, None]
    t_idx = jnp.arange(t)[None, :]
    lag_n = jnp.arange(t)[:, None, None]
    shift = jnp.stack([(t_idx - s_idx)[None] == lag_n, (s_idx - t_idx)[None] == lag_n]).astype(F32)
    kg = jnp.einsum('dnst,dginj->gsitj', shift, lag, precision=HIGHEST)
    k_rows = jnp.transpose(kg.reshape(no, gl, t, h, t * h), (0, 2, 1, 3, 4)).reshape(
        no, t * gl * h, t * h)
    ps_re = jnp.stack([pr[0, :, :t][:, ::-1], pr[1, :, :t]])[:, :, :, None, :]
    ps_im = jnp.stack([pi[0, :, :t][:, ::-1], pi[1, :, :t]])[:, :, :, None, :]
    f_re = ps_re * bb_re[:, :, None] - ps_im * bb_im[:, :, None]
    f_im = ps_re * bb_im[:, :, None] + ps_im * bb_re[:, :, None]
    fg = jnp.transpose(jnp.stack([f_re, f_im]), (2, 3, 4, 0, 1, 5))
    f_rows = jnp.transpose(fg.reshape(no, gl, t, h, 4 * p), (0, 2, 1, 3, 4)).reshape(
        no, t * gl * h, 4 * p)
    pick = lambda c: jnp.stack([c[0, :, :, 1:], c[1, :, :, 1:][:, :, ::-1]])
    eg = jnp.stack([pick(cl_re), -pick(cl_im)])
    eg = jnp.transpose(eg.reshape(2 * N_DIR, no, gl, p, t * h), (1, 0, 2, 3, 4))
    e_rows = eg.reshape(no, 4 * gl * p, t * h)
    lam_rows = [pr[0, :, t], pi[0, :, t], pr[1, :, t], pi[1, :, t]]
    fmat, emat, kmat = _s5_expand(f_rows, e_rows, k_rows)
    lam = jnp.stack(lam_rows, axis=1)
    lam = jnp.transpose(lam.reshape(no, gl, 4, p), (0, 2, 1, 3)).reshape(no, 4, gl * p)
    gmat = jnp.einsum('ab,oahk->oahbk', jnp.eye(gl, dtype=F32),
                      glu_w.astype(F32).reshape(no, gl, h, h)).reshape(no, gl * h, gl * h)
    dvec = d_skip.astype(F32).reshape(no, 1, gl * h)
    bvec = glu_b.astype(F32).reshape(no, 1, gl * h)
    return fmat, emat, kmat, lam, dvec, bvec, gmat.astype(BF16)


def _s5_expand_kernel(f_ref, e_ref, k_ref, rep_s_ref, rep_t_ref, fo_ref, eo_ref, ko_ref):
    def expand(x, rep, row_div, col_div):
        y = jnp.dot(x.astype(BF16), rep, preferred_element_type=F32)
        row = lax.broadcasted_iota(jnp.int32, y.shape, 0)
        col = lax.broadcasted_iota(jnp.int32, y.shape, 1)
        same = (row // row_div) % S5_LANE_GROUPS == (col // col_div) % S5_LANE_GROUPS
        return jnp.where(same, y, 0.0).astype(BF16)

    fo_ref[0] = expand(f_ref[0], rep_s_ref[...], S5_GROUP, S5_STATE)
    eo_ref[0] = expand(e_ref[0], rep_t_ref[...], S5_STATE, S5_GROUP)
    ko_ref[0] = expand(k_ref[0], rep_t_ref[...], S5_GROUP, S5_GROUP)


def _s5_expand(f_rows, e_rows, k_rows):
    no, rows_f, cols_f = f_rows.shape
    _, rows_e, cols_e = e_rows.shape
    gl = S5_LANE_GROUPS
    c = jnp.arange(cols_f * gl)
    rep_s = (jnp.arange(cols_f)[:, None]
             == (c // (gl * S5_STATE)) * S5_STATE + c % S5_STATE).astype(BF16)
    c = jnp.arange(cols_e * gl)
    rep_t = (jnp.arange(cols_e)[:, None]
             == (c // (gl * S5_GROUP)) * S5_GROUP + c % S5_GROUP).astype(BF16)
    blk = lambda r, cc: pl.BlockSpec((1, r, cc), lambda o: (o, 0, 0))
    full = lambda a: pl.BlockSpec(a.shape, lambda o: (0, 0))
    return pl.pallas_call(
        _s5_expand_kernel,
        grid=(no,),
        in_specs=[blk(rows_f, cols_f), blk(rows_e, cols_e), blk(rows_f, cols_e),
                  full(rep_s), full(rep_t)],
        out_specs=[blk(rows_f, cols_f * gl), blk(rows_e, cols_e * gl), blk(rows_f, cols_e * gl)],
        out_shape=[jax.ShapeDtypeStruct((no, rows_f, cols_f * gl), BF16),
                   jax.ShapeDtypeStruct((no, rows_e, cols_e * gl), BF16),
                   jax.ShapeDtypeStruct((no, rows_f, cols_e * gl), BF16)],
        compiler_params=_params("parallel"),
        name="s5_expand",
    )(f_rows, e_rows, k_rows, rep_s, rep_t)


def _s5_chunk_rows(u_ref, b, start, n_chunks):
    t = S5_CHUNK
    return jnp.concatenate(
        [u_ref[b, pl.ds(start + s, n_chunks, stride=t), :] for s in range(t)], axis=-1)


def _s5_in_kernel(u_ref, f_ref, z_ref):
    bsz, n_tok, _ = u_ref.shape
    for b in range(bsz):
        z_ref[b] = _mm(_s5_chunk_rows(u_ref, b, 0, n_tok // S5_CHUNK), f_ref[0])


def _s5_scan_kernel(n_ctx_chunks, z_ref, lam_ref, hs_ref):
    n_chunks = z_ref.shape[1]
    q = z_ref.shape[2] // 4
    rows = S5_SCAN_ROWS
    n_blocks, ctx_blocks = n_chunks // rows, n_ctx_chunks // rows
    lam = lam_ref[0]
    lf_re, lf_im, lb_re, lb_im = lam[0:1], lam[1:2], lam[2:3], lam[3:4]

    def block(k, carry):
        f_re, f_im, b_re, b_im = carry
        kb = jnp.where(k < ctx_blocks, ctx_blocks - 1 - k, n_blocks - 1 + ctx_blocks - k)
        rf = pl.multiple_of(k * rows, rows)
        rb = pl.multiple_of(kb * rows, rows)
        zf = z_ref[0, pl.ds(rf, rows), :]
        zb = z_ref[0, pl.ds(rb, rows), :]
        ent = [[], [], [], []]
        for j in range(rows):
            jb = rows - 1 - j
            for lst, val in zip(ent, (f_re, f_im, b_re, b_im)):
                lst.append(val)
            f_re, f_im = (lf_re * f_re - lf_im * f_im + zf[j:j + 1, 0:q],
                          lf_re * f_im + lf_im * f_re + zf[j:j + 1, 2 * q:3 * q])
            b_re, b_im = (lb_re * b_re - lb_im * b_im + zb[jb:jb + 1, q:2 * q],
                          lb_re * b_im + lb_im * b_re + zb[jb:jb + 1, 3 * q:4 * q])
        hs_ref[0, pl.ds(rf, rows), 0:q] = jnp.concatenate(ent[0], axis=0)
        hs_ref[0, pl.ds(rf, rows), 2 * q:3 * q] = jnp.concatenate(ent[1], axis=0)
        hs_ref[0, pl.ds(rb, rows), q:2 * q] = jnp.concatenate(ent[2][::-1], axis=0)
        hs_ref[0, pl.ds(rb, rows), 3 * q:4 * q] = jnp.concatenate(ent[3][::-1], axis=0)
        return f_re, f_im, b_re, b_im

    zero = jnp.zeros((1, q), F32)
    lax.fori_loop(0, n_blocks, block, (zero, zero, zero, zero))


def _s5_out_kernel(n_ctx, u_ref, hs_ref, e_ref, k_ref, g_ref, d_ref, b_ref, o_ref, y_scr):
    t = S5_CHUNK
    bsz, n_tok, lanes = u_ref.shape
    n_chunks = (n_tok - n_ctx) // t
    for b in range(bsz):
        x = _s5_chunk_rows(u_ref, b, n_ctx, n_chunks)
        y = _mm(x, k_ref[0]) + _mm(hs_ref[b, n_ctx // t:, :], e_ref[0])
        for s in range(t):
            y_scr[pl.ds(s, n_chunks, stride=t), :] = y[:, s * lanes:(s + 1) * lanes]
        y = jax.nn.gelu(y_scr[...] + d_ref[0] * u_ref[b, n_ctx:, :])
        gate = _mm(y, g_ref[0]) + b_ref[0]
        o_ref[b] = y * jax.nn.sigmoid(gate)


def _s5(u_all, consts, n_ctx):
    fmat, emat, kmat, lam, dvec, bvec, gmat = consts
    bsz, n_tok, width = u_all.shape
    t = S5_CHUNK
    n_chunks = n_tok // t
    no, rows_k, cols_f = fmat.shape
    assert n_ctx % (t * S5_SCAN_ROWS) == 0 and n_tok % (t * S5_SCAN_ROWS) == 0
    tok = pl.BlockSpec((1, n_tok, LANES), lambda o, b: (b, 0, o))
    state = pl.BlockSpec((1, n_chunks, cols_f), lambda o, b: (b, 0, o))
    full = lambda a: pl.BlockSpec((1,) + a.shape[1:], lambda o, b: (o, 0, 0))
    z = pl.pallas_call(
        _s5_in_kernel,
        grid=(no, bsz),
        in_specs=[tok, full(fmat)],
        out_specs=state,
        out_shape=jax.ShapeDtypeStruct((bsz, n_chunks, no * cols_f), F32),
        compiler_params=_params("parallel", "parallel"),
        name="s5_in",
    )(u_all, fmat)
    hs = pl.pallas_call(
        functools.partial(_s5_scan_kernel, n_ctx // t),
        grid=(no, bsz),
        in_specs=[state, full(lam)],
        out_specs=state,
        out_shape=jax.ShapeDtypeStruct((bsz, n_chunks, no * cols_f), F32),
        compiler_params=_params("parallel", "parallel"),
        name="s5_scan",
    )(z, lam)
    n_lat = n_tok - n_ctx
    return pl.pallas_call(
        functools.partial(_s5_out_kernel, n_ctx),
        grid=(no, bsz),
        in_specs=[tok, state, full(emat), full(kmat), full(gmat), full(dvec), full(bvec)],
        out_specs=pl.BlockSpec((1, n_lat, LANES), lambda o, b: (b, 0, o)),
        out_shape=jax.ShapeDtypeStruct((bsz, n_lat, width), F32),
        scratch_shapes=[pltpu.VMEM((n_lat, LANES), F32)],
        compiler_params=_params("parallel", "parallel"),
        name="s5_out",
    )(u_all, hs, emat, kmat, gmat, dvec, bvec)


def _rw_feat_kernel(width, has_vert, n_tiles, z_ref, zp_ref, zn_ref, mu_ref, w0_ref, w2_ref,
                    a0_ref, a2_ref, g2_ref, kk_ref, ka_ref, rk_ref, hsel_ref, hselt_ref,
                    r_ref, v_ref, kh_ref, g_ref, bv_ref, lw_ref, q_ref, kt_ref):
    z = z_ref[0]
    t0, cz = z.shape
    rw = r_ref.shape[-1]
    mu = mu_ref[...]
    tok = lax.broadcasted_iota(jnp.int32, (t0, 1), 0)
    col = tok % width
    left = pltpu.roll(z, 1, 0)
    right = pltpu.roll(z, t0 - 1, 0)
    out = z + jnp.where(col != 0, mu[0:1] * (left - z), 0.0)
    out = out + jnp.where(col != width - 1, mu[1:2] * (right - z), 0.0)
    if has_vert:
        i = pl.program_id(1)
        up = jnp.concatenate([zp_ref[0], z[:t0 - width]], axis=0)
        down = jnp.concatenate([z[width:], zn_ref[0]], axis=0)
        up_ok = jnp.logical_or(i > 0, tok >= width)
        down_ok = jnp.logical_or(i < n_tiles - 1, tok < t0 - width)
        out = out + jnp.where(up_ok, mu[2:3] * (up - z), 0.0)
        out = out + jnp.where(down_ok, mu[3:4] * (down - z), 0.0)
    r = out[:, 0:rw]
    k = out[:, rw:2 * rw]
    v = out[:, 2 * rw:3 * rw]
    o = 3 * rw
    lora = w2_ref.shape[0]
    xw = out[:, o:o + lora]
    xa = out[:, o + lora:o + 2 * lora]
    xg = out[:, o + 2 * lora:]
    dec = w0_ref[...] + _mm(jnp.tanh(xw), w2_ref[...])
    lw = -math.exp(-0.5) * jax.nn.sigmoid(dec)
    a = jax.nn.sigmoid(a0_ref[...] + _mm(xa, a2_ref[...]))
    g = _mm(jax.nn.sigmoid(xg), g2_ref[...])
    kk = k * kk_ref[...]
    ssq = _mm_sel(kk * kk, hsel_ref[...])
    inv = 1.0 / jnp.maximum(jnp.sqrt(ssq), 1e-12)
    kh = kk * _mm_sel(inv, hselt_ref[...])
    ka = ka_ref[...]
    kt_sum = jnp.zeros_like(k)
    for d in range(N_DIR):
        a_d = a[:, d * rw:(d + 1) * rw]
        kt_d = k * (1.0 + (a_d - 1.0) * ka)
        kt_sum = kt_sum + kt_d
        lw_ref[d, 0] = lw[:, d * rw:(d + 1) * rw]
        q_ref[d, 0] = a_d * kh
        kt_ref[d, 0] = kt_d
    bonus = _mm_sel(_mm_sel(r * kt_sum * rk_ref[...], hsel_ref[...]), hselt_ref[...])
    r_ref[0] = r
    v_ref[0] = v
    kh_ref[0] = kh
    g_ref[0] = g
    bv_ref[0] = bonus * v


def _rw_features(z, width, has_vert, mu, w0, w2blk, a0, a2blk, g2, k_k, k_a, r_k, hsel, hselt):
    bsz, n, cz = z.shape
    rw = k_k.shape[-1]
    t0 = ROW_TILE
    n_tiles = n // t0
    per = t0 // GRID_W
    nblk = n // GRID_W
    full = lambda a: pl.BlockSpec(a.shape, lambda b, i: (0,) * a.ndim)
    tok = pl.BlockSpec((1, t0, rw), lambda b, i: (b, i, 0))
    dtok = pl.BlockSpec((N_DIR, 1, t0, rw), lambda b, i: (0, b, i, 0))
    consts = (mu, w0, w2blk, a0, a2blk, g2, k_k, k_a, r_k, hsel, hselt)
    return pl.pallas_call(
        functools.partial(_rw_feat_kernel, width, has_vert, n_tiles),
        grid=(bsz, n_tiles),
        in_specs=[pl.BlockSpec((1, t0, cz), lambda b, i: (b, i, 0)),
                  pl.BlockSpec((1, GRID_W, cz), lambda b, i: (b, jnp.maximum(i * per - 1, 0), 0)),
                  pl.BlockSpec((1, GRID_W, cz),
                               lambda b, i: (b, jnp.minimum(i * per + per, nblk - 1), 0))]
        + [full(a) for a in consts],
        out_specs=[tok] * 5 + [dtok] * 3,
        out_shape=[jax.ShapeDtypeStruct((bsz, n, rw), F32)] * 5
        + [jax.ShapeDtypeStruct((N_DIR, bsz, n, rw), F32)] * 3,
        compiler_params=_params("parallel", "parallel"),
        name="rw_features",
    )(z, z, z, *consts)


def _rw_chunk_kernel(r_ref, v_ref, kh_ref, lw_ref, q_ref, kt_ref, g_ref, h_ref, rh_ref, y0_ref):
    rev = pl.program_id(0) == 1
    n = RW_CHUNK
    hd = RW_HEAD
    n_sub = lw_ref.shape[2] // n
    row = lax.broadcasted_iota(jnp.int32, (n, n), 0)
    col = lax.broadcasted_iota(jnp.int32, (n, n), 1)
    ahead = (row - col) * jnp.where(rev, -1, 1)
    incl = (ahead >= 0).astype(F32)
    strict = (ahead > 0).astype(F32)
    eye = (row == col).astype(F32)
    same_block = [(jnp.right_shift(row, s) == jnp.right_shift(col, s)).astype(F32)
                  for s in range(3, n.bit_length())]
    n_heads = lw_ref.shape[-1] // hd
    pt, rt, v, qt, ktt, qh, kth, gam, where = [], [], [], [], [], [], [], [], []
    for c in range(n_sub):
        tok = slice(c * n, (c + 1) * n)
        lw = lw_ref[0, 0, tok, :]
        b_incl = _sel_mm(incl, lw)
        btot = jnp.sum(lw, axis=0, keepdims=True)
        e_neg = jnp.exp(-b_incl)
        e_rem = jnp.exp(btot - b_incl)
        q_c, kt_c = q_ref[0, 0, tok, :], kt_ref[0, 0, tok, :]
        pt_all = kh_ref[0, tok, :] * jnp.exp(b_incl - lw)
        rt_all = r_ref[0, tok, :] * jnp.exp(b_incl)
        qt_all, ktt_all = q_c * e_neg, kt_c * e_neg
        qh_all, kth_all = q_c * e_rem, kt_c * e_rem
        gam_all = jnp.exp(btot)
        v_all = v_ref[0, tok, :]
        for h in range(n_heads):
            sl = slice(h * hd, (h + 1) * hd)
            for lst, arr in zip((pt, rt, v, qt, ktt, qh, kth, gam),
                                (pt_all, rt_all, v_all, qt_all, ktt_all, qh_all, kth_all, gam_all)):
                lst.append(arr[:, sl])
            where.append((c, tok, sl))
    heads = range(len(where))
    a4 = [_mm(jnp.concatenate([pt[h], rt[h]], 0), jnp.concatenate([qt[h], ktt[h]], 0),
              ((1,), (1,))) for h in heads]
    nmat = [strict * a4[h][:n, :n] for h in heads]
    akv = [_mm(strict * a4[h][:n, n:], v[h]) for h in heads]
    nd = [same_block[0] * nmat[h] for h in heads]
    x = [_mm(nd[h], nd[h]) for h in heads]
    m = [eye - nd[h] for h in heads]
    m = [m[h] + _mm(m[h], x[h]) for h in heads]
    x = [_mm(x[h], x[h]) for h in heads]
    m = [m[h] + _mm(m[h], x[h]) for h in heads]
    for lvl in range(1, len(same_block)):
        ring = same_block[lvl] - same_block[lvl - 1]
        t = [_mm(m[h], ring * nmat[h]) for h in heads]
        m = [m[h] - _mm(t[h], m[h]) for h in heads]
    wu = [_mm(m[h], jnp.concatenate([pt[h], akv[h]], 1)) for h in heads]
    gh = [_mm(wu[h], qh[h], ((0,), (0,))) for h in heads]
    vk = [_mm(v[h], kth[h], ((0,), (0,))) for h in heads]
    lwu = [_mm(incl * a4[h][n:, :n], wu[h]) for h in heads]
    lv = [_mm(incl * a4[h][n:, n:], v[h]) for h in heads]
    for h in heads:
        c, tok, sl = where[h]
        g_ref[0, 0, c, :, sl] = eye * gam[h] - gh[h][:hd]
        h_ref[0, 0, c, :, sl] = vk[h] - gh[h][hd:]
        rh_ref[0, 0, tok, sl] = rt[h] - lwu[h][:, :hd]
        y0_ref[0, 0, tok, sl] = lv[h] - lwu[h][:, hd:]


def _rw_chunks(r, v, kh, lw, q, kt):
    bsz, n, rw = r.shape
    nc = n // RW_CHUNK
    per = RW_CHUNKS_PER_STEP
    tok = pl.BlockSpec((1, per * RW_CHUNK, rw), lambda d, b, c: (b, c, 0))
    dtok = pl.BlockSpec((1, 1, per * RW_CHUNK, rw), lambda d, b, c: (d, b, c, 0))
    mat = pl.BlockSpec((1, 1, per, RW_HEAD, rw), lambda d, b, c: (d, b, c, 0, 0))
    return pl.pallas_call(
        _rw_chunk_kernel,
        grid=(N_DIR, bsz, nc // per),
        in_specs=[tok, tok, tok, dtok, dtok, dtok],
        out_specs=[mat, mat, dtok, dtok],
        out_shape=[jax.ShapeDtypeStruct((N_DIR, bsz, nc, RW_HEAD, rw), F32)] * 2
        + [jax.ShapeDtypeStruct((N_DIR, bsz, n, rw), F32)] * 2,
        compiler_params=_params("parallel", "parallel", "parallel"),
        name="rw_chunks",
    )(r, v, kh, lw, q, kt)


def _rw_state_kernel(emit_y, s0_ref, gf_ref, gb_ref, hf_ref, hb_ref, rhf_ref, rhb_ref, y0f_ref,
                     y0b_ref, *rest):
    if emit_y:
        yf_ref, yb_ref, sfin_ref, s_scr = rest
    else:
        sfin_ref, s_scr = rest
    c = pl.program_id(0)
    hd = RW_HEAD
    n_dir, bsz = s_scr.shape[0], s_scr.shape[1]

    @pl.when(c == 0)
    def _():
        s_scr[...] = s0_ref[...]

    sls = [slice(h * hd, (h + 1) * hd) for h in range(s_scr.shape[-1] // hd)]
    g_refs, h_refs = (gf_ref, gb_ref), (hf_ref, hb_ref)
    rh_refs, y0_refs = (rhf_ref, rhb_ref), (y0f_ref, y0b_ref)
    chains = [(d, b) for d in range(n_dir) for b in range(bsz)]
    s_bf = {k: s_scr[k[0], k[1]].astype(BF16) for k in chains}
    g_bf = {k: g_refs[k[0]][0, k[1], 0].astype(BF16) for k in chains}
    s_new = {k: [_mm(s_bf[k][:, sl], g_bf[k][:, sl]) for sl in sls] for k in chains}
    if emit_y:
        y_refs = (yf_ref, yb_ref)
        rh_bf = {k: rh_refs[k[0]][0, k[1]].astype(BF16) for k in chains}
        y = {k: [_mm(rh_bf[k][:, sl], s_bf[k][:, sl], ((1,), (1,))) for sl in sls] for k in chains}
        for d, b in chains:
            y_refs[d][b] = jnp.concatenate(y[d, b], axis=-1) + y0_refs[d][0, b]
    for d, b in chains:
        s_scr[d, b] = jnp.concatenate(s_new[d, b], axis=-1) + h_refs[d][0, b, 0]

    @pl.when(c == pl.num_programs(0) - 1)
    def _():
        sfin_ref[...] = s_scr[...]


def _rw_state(s0, gmat, hmat, rh, y0, emit_y):
    n_dir, bsz, nc, hd, rw = gmat.shape
    n = rh.shape[2]
    chunk = lambda d, c: c + d * (nc - 1 - 2 * c)
    mat = lambda d: pl.BlockSpec((1, bsz, 1, hd, rw), lambda c: (d, 0, chunk(d, c), 0, 0))
    dtok = lambda d: pl.BlockSpec((1, bsz, RW_CHUNK, rw), lambda c: (d, 0, chunk(d, c), 0))
    st = pl.BlockSpec((n_dir, bsz, hd, rw), lambda c: (0, 0, 0, 0))
    out_specs = [st]
    out_shape = [jax.ShapeDtypeStruct((n_dir, bsz, hd, rw), F32)]
    if emit_y:
        out_specs = [pl.BlockSpec((bsz, RW_CHUNK, rw), lambda c, d=d: (0, chunk(d, c), 0))
                     for d in range(n_dir)] + out_specs
        out_shape = [jax.ShapeDtypeStruct((bsz, n, rw), F32)] * n_dir + out_shape
    return pl.pallas_call(
        functools.partial(_rw_state_kernel, emit_y),
        grid=(nc,),
        in_specs=[st, mat(0), mat(1), mat(0), mat(1), dtok(0), dtok(1), dtok(0), dtok(1)],
        out_specs=out_specs,
        out_shape=out_shape,
        scratch_shapes=[pltpu.VMEM((n_dir, bsz, hd, rw), F32)],
        compiler_params=_params("arbitrary"),
        name="rw_state_y" if emit_y else "rw_state",
    )(s0, gmat, gmat, hmat, hmat, rh, rh, y0, y0)


def _outproj_kernel(x_ref, y5_ref, yf_ref, yb_ref, bv_ref, g_ref, lnw_ref, lnb_ref, hsel_ref,
                    hselt_ref, wo_ref, gt1_ref, g2_ref, sc2_ref, sh2_ref, rw_ref, rb_ref,
                    x1_ref, f_ref, idx_ref, gate_ref, rank_ref, cnt_ref, carry):
    first = jnp.logical_and(pl.program_id(0) == 0, pl.program_id(1) == 0)

    @pl.when(first)
    def _():
        carry[...] = jnp.zeros_like(carry)

    inv_hd = 1.0 / RW_HEAD
    y = yf_ref[0] + yb_ref[0]
    mean = _mm_sel(_mm_sel(y, hsel_ref[...]), hselt_ref[...]) * inv_hd
    yc = y - mean
    var = _mm_sel(_mm_sel(yc * yc, hsel_ref[...]), hselt_ref[...]) * inv_hd
    yn = yc * lax.rsqrt(var + RW_GN_EPS) * lnw_ref[...] + lnb_ref[...]
    yr = (yn + bv_ref[0]) * g_ref[0]
    mix = _mm(jnp.concatenate([y5_ref[0], yr], axis=-1), wo_ref[...])
    x1 = x_ref[0] + gt1_ref[0] * mix
    x1_ref[0] = x1
    f = _rms_mod(x1, g2_ref[...], sc2_ref[0], sh2_ref[0])
    n_lines = f.shape[1] // LANES
    for s in range(n_lines):
        f_ref[0, pl.ds(s, f.shape[0], stride=n_lines), :] = f[:, s * LANES:(s + 1) * LANES]

    logits = _mm_f32(f, rw_ref[...]) + rb_ref[...]
    tm, ne = logits.shape
    eid = lax.broadcasted_iota(jnp.int32, (tm, ne), 1)
    work = logits
    sel = jnp.zeros((tm, ne), F32)
    idx_cols, val_cols = [], []
    for _ in range(TOP_K):
        top = jnp.max(work, axis=-1, keepdims=True)
        pick = jnp.min(jnp.where(work == top, eid, ne), axis=-1, keepdims=True)
        hit = eid == pick
        sel = jnp.where(hit, 1.0, sel)
        work = jnp.where(hit, -jnp.inf, work)
        idx_cols.append(pick)
        val_cols.append(top)
    exps = [jnp.exp(vk - val_cols[0]) for vk in val_cols]
    denom = exps[0] + exps[1] + exps[2] + exps[3]
    row = lax.broadcasted_iota(jnp.int32, (tm, tm), 0)
    colm = lax.broadcasted_iota(jnp.int32, (tm, tm), 1)
    before = _mm((colm < row).astype(F32), sel) + carry[0:1, :]
    rank_cols = [jnp.sum(jnp.where(eid == ic, before, 0.0), axis=-1, keepdims=True)
                 for ic in idx_cols]
    idx_ref[0] = jnp.concatenate(idx_cols, axis=-1)
    gate_ref[0] = jnp.concatenate([e / denom for e in exps], axis=-1)
    rank_ref[0] = jnp.concatenate(rank_cols, axis=-1).astype(jnp.int32)
    total = carry[0:1, :] + jnp.sum(sel, axis=0, keepdims=True)
    carry[...] = jnp.broadcast_to(total, carry.shape)
    cnt_ref[...] = jnp.broadcast_to(total, cnt_ref.shape).astype(jnp.int32)


def _outproj(x, y5, yf, yb, bv, g, ln_w, ln_b, hsel, hselt, wo, gt1, g2, sc2, sh2, router_w,
             router_b):
    bsz, n, d = x.shape
    rw = y5.shape[-1]
    ne = router_w.shape[-1]
    tm = ROW_TILE
    row = lambda w: pl.BlockSpec((1, tm, w), lambda b, i: (b, i, 0))
    vec = lambda w: pl.BlockSpec((1, w), lambda b, i: (0, 0))
    bvec = pl.BlockSpec((1, 1, d), lambda b, i: (b, 0, 0))
    full = lambda a: pl.BlockSpec(a.shape, lambda b, i: (0,) * a.ndim)
    return pl.pallas_call(
        _outproj_kernel,
        grid=(bsz, n // tm),
        in_specs=[row(d), row(rw), row(rw), row(rw),
                  row(rw), row(rw), vec(rw), vec(rw), full(hsel), full(hselt),
                  pl.BlockSpec(wo.shape, lambda b, i: (0, 0), pipeline_mode=pl.Buffered(1)),
                  bvec, vec(d), bvec, bvec, full(router_w), vec(ne)],
        out_specs=[row(d), pl.BlockSpec((1, tm * (d // LANES), LANES), lambda b, i: (b, i, 0)),
                   row(TOP_K), row(TOP_K), row(TOP_K),
                   pl.BlockSpec((8, ne), lambda b, i: (0, 0))],
        out_shape=[jax.ShapeDtypeStruct((bsz, n, d), F32),
                   jax.ShapeDtypeStruct((bsz, n * (d // LANES), LANES), F32),
                   jax.ShapeDtypeStruct((bsz, n, TOP_K), jnp.int32),
                   jax.ShapeDtypeStruct((bsz, n, TOP_K), F32),
                   jax.ShapeDtypeStruct((bsz, n, TOP_K), jnp.int32),
                   jax.ShapeDtypeStruct((8, ne), jnp.int32)],
        scratch_shapes=[pltpu.VMEM((8, ne), F32)],
        compiler_params=_params("arbitrary", "arbitrary"),
        name="outproj_router",
    )(x, y5, yf, yb, bv, g, ln_w.reshape(1, rw), ln_b.reshape(1, rw), hsel, hselt, wo, gt1,
      g2.reshape(1, d), sc2, sh2, router_w, router_b.reshape(1, ne))


def _dispatch_kernel(tile_rows, idx_ref, rank_ref, start_ref, fill_ref, f_ref, xs_ref, zeros, sem,
                     sem_fill):
    tr = tile_rows
    tm = f_ref.shape[0] // tr
    blk = zeros.shape[0]

    @pl.when(pl.program_id(0) == 0)
    def _():
        zeros[...] = jnp.zeros_like(zeros)

        def fill(b):
            dst = xs_ref.at[pl.ds(pl.multiple_of(b * blk, blk), blk)]
            return pltpu.make_async_copy(zeros, dst, sem_fill)

        def start(b, carry):
            @pl.when(fill_ref[b] == 1)
            def _():
                fill(b).start()
            return carry

        def wait(b, carry):
            @pl.when(fill_ref[b] == 1)
            def _():
                fill(b).wait()
            return carry

        lax.fori_loop(0, fill_ref.shape[0], start, 0)
        lax.fori_loop(0, fill_ref.shape[0], wait, 0)

    def issue(t, _):
        src = f_ref.at[pl.ds(pl.multiple_of(t * tr, tr), tr)]
        for k in range(TOP_K):
            a = t * TOP_K + k
            dst = start_ref[idx_ref[a]] + rank_ref[a]
            pltpu.make_async_copy(src, xs_ref.at[pl.ds(pl.multiple_of(dst * tr, tr), tr)],
                                  sem).start()
        return 0

    lax.fori_loop(0, tm, issue, 0)
    for _ in range(TOP_K):
        pltpu.make_async_copy(f_ref, xs_ref.at[pl.ds(0, tm * tr)], sem).wait()


def _dispatch(idx_flat, rank_flat, start, fill, f_lines, n_tok, n_slots):
    lanes = f_lines.shape[1]
    tr = f_lines.shape[0] // n_tok
    tm = ROW_TILE
    smem_tok = pl.BlockSpec((tm * TOP_K,), lambda i: (i,), memory_space=pltpu.SMEM)
    smem_all = lambda a: pl.BlockSpec(a.shape, lambda i: (0,), memory_space=pltpu.SMEM)
    return pl.pallas_call(
        functools.partial(_dispatch_kernel, tr),
        grid=(n_tok // tm,),
        in_specs=[smem_tok, smem_tok, smem_all(start), smem_all(fill),
                  pl.BlockSpec((tm * tr, lanes), lambda i: (i, 0))],
        out_specs=pl.BlockSpec(memory_space=pl.ANY),
        out_shape=jax.ShapeDtypeStruct((n_slots * tr, lanes), f_lines.dtype),
        scratch_shapes=[pltpu.VMEM((MOE_ROWS * tr, lanes), f_lines.dtype),
                        pltpu.SemaphoreType.DMA(()), pltpu.SemaphoreType.DMA(())],
        compiler_params=_params("arbitrary"),
        name="moe_dispatch",
    )(idx_flat, rank_flat, start, fill, f_lines)


def _moe_kernel(sbe_ref, sbrow_ref, sbn_ref, nsb_ref, xs_ref, wg_ref, wl_ref, wd_ref, bg_ref,
                bl_ref, bd_ref, y_ref, x_stage, x_bf, acc, y_stage, sem_in, sem_out):
    sb, j = pl.program_id(0), pl.program_id(1)
    nj = pl.num_programs(1)
    rb = MOE_ROWS
    n_tiles = y_stage.shape[1] // rb
    valid = sb < nsb_ref[0]

    def copy_in(row, slot):
        src = xs_ref.at[pl.ds(pl.multiple_of(row * n_tiles, rb * n_tiles), rb * n_tiles)]
        return pltpu.make_async_copy(src, x_stage.at[slot], sem_in.at[slot])

    def copy_out(row, slot):
        dst = y_ref.at[pl.ds(pl.multiple_of(row * n_tiles, rb * n_tiles), rb * n_tiles)]
        return pltpu.make_async_copy(y_stage.at[slot], dst, sem_out.at[slot])

    @pl.when(valid)
    def _():
        n_blk = sbn_ref[sb]
        row0 = sbrow_ref[sb] * rb

        @pl.when(j == 0)
        def _():
            copy_in(row0, 0).start()

            def load(k, carry):
                slot = k % 2

                @pl.when(k + 1 < n_blk)
                def _():
                    copy_in(row0 + (k + 1) * rb, 1 - slot).start()

                copy_in(row0 + k * rb, slot).wait()
                r = pl.multiple_of(k * rb, rb)
                for s in range(n_tiles):
                    x_bf[pl.ds(r, rb), s * LANES:(s + 1) * LANES] = (
                        x_stage[slot, pl.ds(s, rb, stride=n_tiles), :].astype(BF16))
                return carry

            lax.fori_loop(0, n_blk, load, 0)

            def clear(k, carry):
                acc[pl.ds(pl.multiple_of(k * rb, rb), rb), :] = jnp.broadcast_to(
                    bd_ref[0], (rb, acc.shape[1]))
                return carry

            lax.fori_loop(0, n_blk, clear, 0)

        def accumulate(r):
            x = x_bf[r, :]
            glu = jnp.dot(x, wg_ref[0].astype(BF16), preferred_element_type=F32) + bg_ref[0]
            lin = jnp.dot(x, wl_ref[0].astype(BF16), preferred_element_type=F32) + bl_ref[0]
            glu = jnp.minimum(glu, SWIGLU_LIMIT)
            lin = jnp.clip(lin, -SWIGLU_LIMIT, SWIGLU_LIMIT)
            act = ((lin + 1.0) * glu * jax.nn.sigmoid(SWIGLU_ALPHA * glu)).astype(BF16)
            acc[r, :] += jnp.dot(act, wd_ref[0].astype(BF16), preferred_element_type=F32)

        def row_pair(kk, carry):
            accumulate(pl.ds(pl.multiple_of(kk * 2 * rb, 2 * rb), 2 * rb))
            return carry

        lax.fori_loop(0, n_blk // 2, row_pair, 0)

        @pl.when(n_blk % 2 == 1)
        def _():
            accumulate(pl.ds(pl.multiple_of((n_blk - 1) * rb, rb), rb))

        @pl.when(j == nj - 1)
        def _():
            def store(k, carry):
                slot = k % 2

                @pl.when(k >= 2)
                def _():
                    copy_out(row0, slot).wait()

                r = pl.multiple_of(k * rb, rb)
                for s in range(n_tiles):
                    y_stage[slot, pl.ds(s, rb, stride=n_tiles), :] = (
                        acc[pl.ds(r, rb), s * LANES:(s + 1) * LANES])
                copy_out(row0 + k * rb, slot).start()
                return carry

            lax.fori_loop(0, n_blk, store, 0)

            @pl.when(n_blk >= 2)
            def _():
                copy_out(row0, n_blk % 2).wait()

            copy_out(row0, (n_blk - 1) % 2).wait()

    @pl.when(jnp.logical_not(valid))
    def _():
        q = (sb - nsb_ref[0]) * nj + j
        last = jnp.maximum(nsb_ref[0] - 1, 0)
        first_unused = sbrow_ref[last] + sbn_ref[last]
        n_unused = y_ref.shape[0] // (rb * n_tiles) - first_unused

        @pl.when(q < n_unused)
        def _():
            y_stage[0] = jnp.zeros(y_stage.shape[1:], y_stage.dtype)
            cp = copy_out((first_unused + q) * rb, 0)
            cp.start()
            cp.wait()


def _moe_experts(sb_e, sb_row, sb_n, n_sb, xs_lines, w_gu, b_gu, w_dn, b_dn):
    ne, d, two_de = w_gu.shape
    lanes = xs_lines.shape[1]
    n_slots = xs_lines.shape[0] // (d // lanes)
    de = two_de // 2
    th = MOE_HIDDEN_TILE
    nj = de // th
    n_groups = sb_e.shape[0]
    grp = lambda sb, nsb: jnp.maximum(jnp.minimum(sb, nsb[0] - 1), 0)
    tile = lambda sb, j, nsb: jnp.where(sb < nsb[0], j, nj - 1)
    grid_spec = pltpu.PrefetchScalarGridSpec(
        num_scalar_prefetch=4,
        grid=(n_groups, nj),
        in_specs=[pl.BlockSpec(memory_space=pl.ANY),
                  pl.BlockSpec((1, d, th), lambda sb, j, e, r, n, nsb:
                               (e[grp(sb, nsb)], 0, tile(sb, j, nsb))),
                  pl.BlockSpec((1, d, th), lambda sb, j, e, r, n, nsb:
                               (e[grp(sb, nsb)], 0, nj + tile(sb, j, nsb))),
                  pl.BlockSpec((1, th, d), lambda sb, j, e, r, n, nsb:
                               (e[grp(sb, nsb)], tile(sb, j, nsb), 0)),
                  pl.BlockSpec((1, 1, th), lambda sb, j, e, r, n, nsb:
                               (e[grp(sb, nsb)], 0, tile(sb, j, nsb))),
                  pl.BlockSpec((1, 1, th), lambda sb, j, e, r, n, nsb:
                               (e[grp(sb, nsb)], 0, nj + tile(sb, j, nsb))),
                  pl.BlockSpec((1, 1, d), lambda sb, j, e, r, n, nsb: (e[grp(sb, nsb)], 0, 0))],
        out_specs=pl.BlockSpec(memory_space=pl.ANY),
        scratch_shapes=[pltpu.VMEM((2, MOE_ROWS * (d // lanes), lanes), F32),
                        pltpu.VMEM((MOE_GROUP_ROWS, d), BF16),
                        pltpu.VMEM((MOE_GROUP_ROWS, d), F32),
                        pltpu.VMEM((2, MOE_ROWS * (d // lanes), lanes), F32),
                        pltpu.SemaphoreType.DMA((2,)), pltpu.SemaphoreType.DMA((2,))])
    return pl.pallas_call(
        _moe_kernel,
        grid_spec=grid_spec,
        out_shape=jax.ShapeDtypeStruct((n_slots * (d // lanes), lanes), F32),
        compiler_params=_params("arbitrary", "arbitrary"),
        name="moe_experts",
    )(sb_e, sb_row, sb_n, n_sb, xs_lines, w_gu, w_gu, w_dn, b_gu.reshape(ne, 1, two_de),
      b_gu.reshape(ne, 1, two_de), b_dn.reshape(ne, 1, d))


def _combine_kernel(idx_ref, rank_ref, start_ref, x1_ref, gate_ref, gt2_ref, fg_ref, y_ref,
                    o_ref, buf, sem):
    tm, d = x1_ref.shape
    tr = d // LANES

    def issue(t, _):
        dst = pl.ds(pl.multiple_of(t * tr, tr), tr)
        for k in range(TOP_K):
            a = t * TOP_K + k
            src = start_ref[idx_ref[a]] + rank_ref[a]
            pltpu.make_async_copy(y_ref.at[pl.ds(pl.multiple_of(src * tr, tr), tr)],
                                  buf.at[k, dst], sem).start()
        return 0

    lax.fori_loop(0, tm, issue, 0)
    for k in range(TOP_K):
        pltpu.make_async_copy(y_ref.at[pl.ds(0, tm * tr)], buf.at[k], sem).wait()
    gate = gate_ref[...]
    cols = []
    for s in range(tr):
        acc = gate[:, 0:1] * buf[0, pl.ds(s, tm, stride=tr), :]
        for k in range(1, TOP_K):
            acc = acc + gate[:, k:k + 1] * buf[k, pl.ds(s, tm, stride=tr), :]
        cols.append(acc)
    x2 = x1_ref[...] + gt2_ref[0] * jnp.concatenate(cols, axis=-1)
    ms = jnp.mean(x2 * x2, axis=-1, keepdims=True)
    o_ref[...] = x2 * lax.rsqrt(ms + NORM_EPS) * fg_ref[...]


def _combine(idx_flat, rank_flat, start, x1, gate4, gt2, fg, y_lines, tiles_per_batch):
    n_tok, d = x1.shape
    lanes = y_lines.shape[1]
    tm = ROW_TILE
    smem_tok = pl.BlockSpec((tm * TOP_K,), lambda i: (i,), memory_space=pltpu.SMEM)
    return pl.pallas_call(
        _combine_kernel,
        grid=(n_tok // tm,),
        in_specs=[smem_tok, smem_tok,
                  pl.BlockSpec(start.shape, lambda i: (0,), memory_space=pltpu.SMEM),
                  pl.BlockSpec((tm, d), lambda i: (i, 0)),
                  pl.BlockSpec((tm, TOP_K), lambda i: (i, 0)),
                  pl.BlockSpec((1, 1, d), lambda i: (i // tiles_per_batch, 0, 0)),
                  pl.BlockSpec((1, d), lambda i: (0, 0)),
                  pl.BlockSpec(memory_space=pl.ANY)],
        out_specs=pl.BlockSpec((tm, d), lambda i: (i, 0)),
        out_shape=jax.ShapeDtypeStruct((n_tok, d), F32),
        scratch_shapes=[pltpu.VMEM((TOP_K, tm * (d // lanes), lanes), F32),
                        pltpu.SemaphoreType.DMA(())],
        compiler_params=_params("arbitrary"),
        name="moe_combine_final",
    )(idx_flat, rank_flat, start, x1, gate4, gt2, fg.reshape(1, d), y_lines)


def _block_diag2(w):
    z = jnp.zeros_like(w[0])
    return jnp.concatenate([jnp.concatenate([w[0], z], 1), jnp.concatenate([z, w[1]], 1)], 0)


def kernel(x, c, ctx, c_ctx, mod_w, mod_b, norm1_g, w_in, s5_a_re, s5_a_im, s5_log_dt, s5_b_re,
           s5_b_im, s5_c_re, s5_c_im, s5_d, s5_glu_w, s5_glu_b, rw_mu, rw_w0, rw_w2, rw_a0, rw_a2,
           rw_g2, rw_k_k, rw_k_a, rw_r_k, rw_ln_w, rw_ln_b, w_out, norm2_g, router_w, router_b,
           exp_w_gu, exp_b_gu, exp_w_dn, exp_b_dn, final_g):
    assert mod_w.shape[0] == 1, "single-layer stack only"
    bsz, n_lat, d = x.shape
    n_ctx = ctx.shape[1]
    assert bsz == 2 and n_ctx % ROW_TILE == 0 and n_lat % ROW_TILE == 0
    s5w = s5_d.shape[-1]
    rww = rw_k_k.shape[-1]
    n_heads = rww // RW_HEAD
    ne = router_w.shape[-1]

    cond8 = jnp.zeros((8, d), F32).at[:bsz].set(c).at[bsz].set(c_ctx)
    mod = _adaln(cond8, mod_w[0], mod_b[0])
    sh1, sc1, gt1, sh2, sc2, gt2 = [m[:bsz, None, :] for m in jnp.split(mod, 6, axis=-1)]
    csh1, csc1 = [jnp.broadcast_to(m[bsz][None, None, :], (bsz, 1, d))
                  for m in jnp.split(mod, 6, axis=-1)[:2]]

    wu = w_in[0][:, :s5w].astype(BF16)
    wz = w_in[0][:, s5w:].astype(BF16)
    u_lat, z_lat = _inproj(x, norm1_g[0], sc1, sh1, wu, wz)
    u_ctx, z_ctx = _inproj(ctx, norm1_g[0], csc1, csh1, wu, wz)

    s5c = _s5_constants(s5_a_re[0], s5_a_im[0], s5_log_dt[0], s5_b_re[0], s5_b_im[0], s5_c_re[0],
                        s5_c_im[0], s5_d[0], s5_glu_w[0], s5_glu_b[0])
    y5 = _s5(jnp.concatenate([u_ctx, u_lat], axis=1), s5c, n_ctx)

    lanes_idx = jnp.arange(rww) // RW_HEAD
    hsel = (lanes_idx[:, None] == jnp.arange(LANES)[None, :]).astype(BF16)
    hselt = hsel.T
    feat_consts = (rw_mu[0], rw_w0[0].reshape(1, N_DIR * rww),
                   _block_diag2(rw_w2[0]).astype(BF16), rw_a0[0].reshape(1, N_DIR * rww),
                   _block_diag2(rw_a2[0]).astype(BF16), rw_g2[0].astype(BF16),
                   rw_k_k[0].reshape(1, rww), rw_k_a[0].reshape(1, rww),
                   rw_r_k[0].reshape(1, rww), hsel, hselt)
    fc = _rw_features(z_ctx, n_ctx, False, *feat_consts)
    fl = _rw_features(z_lat, GRID_W, True, *feat_consts)

    def scan_inputs(f):
        r, v, kh, _, _, lw, q, kt = f
        return r, v, kh, lw, q, kt

    gc, hc, rhc, y0c = _rw_chunks(*scan_inputs(fc))
    gl, hl, rhl, y0l = _rw_chunks(*scan_inputs(fl))
    s_zero = jnp.zeros((N_DIR, bsz, RW_HEAD, rww), F32)
    (s_ctx,) = _rw_state(s_zero, gc, hc, rhc, y0c, emit_y=False)
    y_f, y_b, _ = _rw_state(s_ctx, gl, hl, rhl, y0l, emit_y=True)

    x1, f, idx4, gate4, rank4, counts = _outproj(
        x, y5, y_f, y_b, fl[4], fl[3], rw_ln_w[0], rw_ln_b[0], hsel, hselt, w_out[0].astype(BF16),
        gt1, norm2_g[0], sc2, sh2, router_w[0], router_b[0])

    n_tok = bsz * n_lat
    cnt = counts[0]
    padded = (cnt + MOE_ROWS - 1) // MOE_ROWS * MOE_ROWS
    pend = jnp.cumsum(padded)
    start = (pend - padded).astype(jnp.int32)
    nb = n_tok * TOP_K // MOE_ROWS + ne
    n_slots = nb * MOE_ROWS
    blocks_per_group = MOE_GROUP_ROWS // MOE_ROWS
    n_groups_max = n_slots // MOE_GROUP_ROWS + ne
    groups_e = (padded + MOE_GROUP_ROWS - 1) // MOE_GROUP_ROWS
    groups_end = jnp.cumsum(groups_e)
    gidx = jnp.arange(n_groups_max)
    sb_e = jnp.minimum(jnp.sum(groups_end[None, :] <= gidx[:, None], axis=1), ne - 1)
    local = gidx - (groups_end - groups_e)[sb_e]
    sb_row = start[sb_e] // MOE_ROWS + local * blocks_per_group
    sb_n = jnp.clip(padded[sb_e] // MOE_ROWS - local * blocks_per_group, 0, blocks_per_group)
    n_sb = groups_end[-1].astype(jnp.int32).reshape(1)
    idx_flat = idx4.reshape(-1)
    rank_flat = rank4.reshape(-1)

    blk_end = (jnp.arange(nb) + 1) * MOE_ROWS
    fill = jnp.logical_or(jnp.any(blk_end[:, None] == pend[None, :], axis=1),
                          blk_end > pend[-1]).astype(jnp.int32)
    xs = _dispatch(idx_flat, rank_flat, start, fill, f.reshape(-1, LANES), n_tok, n_slots)
    y3 = _moe_experts(sb_e.astype(jnp.int32), sb_row.astype(jnp.int32), sb_n.astype(jnp.int32),
                      n_sb, xs, exp_w_gu[0], exp_b_gu[0], exp_w_dn[0], exp_b_dn[0])
    out = _combine(idx_flat, rank_flat, start, x1.reshape(n_tok, d), gate4.reshape(n_tok, TOP_K),
                   gt2, final_g, y3, n_lat // ROW_TILE)
    return out.reshape(bsz, n_lat, d)
```

```python
import functools
import math

import jax
import jax.numpy as jnp
from jax import lax
from jax.experimental import pallas as pl
from jax.experimental.pallas import tpu as pltpu

F32 = jnp.float32
BF16 = jnp.bfloat16
HIGHEST = lax.Precision.HIGHEST

LANES = 128
VMEM_LIMIT_BYTES = 56 * 1024 * 1024

NORM_EPS = 1e-5
N_DIR = 2
S5_GROUP = 16
S5_STATE = 64
S5_CHUNK = 8
S5_LANE_GROUPS = LANES // S5_GROUP
S5_SCAN_ROWS = 8
RW_HEAD = 64
RW_CHUNK = 64
RW_CHUNKS_PER_STEP = 2
RW_GN_EPS = 64e-5
GRID_W = 64
TOP_K = 4
SWIGLU_ALPHA = 1.702
SWIGLU_LIMIT = 7.0
ROW_TILE = 256
MOE_ROWS = 256
MOE_GROUP_ROWS = 1536
MOE_HIDDEN_TILE = 256


def _params(*sem):
    return pltpu.CompilerParams(dimension_semantics=sem, vmem_limit_bytes=VMEM_LIMIT_BYTES)


def _mm(a, b, dims=((1,), (0,))):
    return lax.dot_general(a.astype(BF16), b.astype(BF16), (dims, ((), ())),
                           preferred_element_type=F32)


def _split_bf16(a):
    hi = a.astype(BF16)
    return hi, (a - hi.astype(F32)).astype(BF16)


def _mm_sel(a, sel):
    hi, lo = _split_bf16(a)
    m = a.shape[0]
    out = jnp.dot(jnp.concatenate([hi, lo], axis=0), sel.astype(BF16), preferred_element_type=F32)
    return out[:m] + out[m:]


def _sel_mm(sel, a):
    hi, lo = _split_bf16(a)
    n = a.shape[1]
    out = jnp.dot(sel.astype(BF16), jnp.concatenate([hi, lo], axis=1), preferred_element_type=F32)
    return out[:, :n] + out[:, n:]


def _mm_f32(a, b, dims=((1,), (0,))):
    return lax.dot_general(a, b, (dims, ((), ())), precision=HIGHEST,
                           preferred_element_type=F32)


def _adaln_kernel(cond_ref, w_ref, b_ref, o_ref):
    c = cond_ref[...]
    o_ref[...] = _mm_f32(c * jax.nn.sigmoid(c), w_ref[...]) + b_ref[...]


def _adaln(cond8, w, b):
    d, n = w.shape
    tn = 1536
    return pl.pallas_call(
        _adaln_kernel,
        grid=(n // tn,),
        in_specs=[pl.BlockSpec((8, d), lambda j: (0, 0)),
                  pl.BlockSpec((d, tn), lambda j: (0, j)),
                  pl.BlockSpec((1, tn), lambda j: (0, j))],
        out_specs=pl.BlockSpec((8, tn), lambda j: (0, j)),
        out_shape=jax.ShapeDtypeStruct((8, n), F32),
        compiler_params=_params("parallel"),
        name="adaln",
    )(cond8, w, b.reshape(1, n))


def _rms_mod(x, g, sc, sh):
    y = x * lax.rsqrt(jnp.mean(x * x, axis=-1, keepdims=True) + NORM_EPS)
    return (y * g) * (1.0 + sc) + sh


def _inproj_kernel(x_ref, g_ref, sc_ref, sh_ref, wu_ref, wz_ref, u_ref, z_ref):
    h = _rms_mod(x_ref[0], g_ref[...], sc_ref[0], sh_ref[0]).astype(BF16)
    u_ref[0] = jnp.dot(h, wu_ref[...], preferred_element_type=F32)
    z_ref[0] = jnp.dot(h, wz_ref[...], preferred_element_type=F32)


def _inproj(x, g, sc, sh, wu, wz):
    bsz, n, d = x.shape
    nu, nz = wu.shape[1], wz.shape[1]
    tm = ROW_TILE
    const = dict(pipeline_mode=pl.Buffered(1))
    return pl.pallas_call(
        _inproj_kernel,
        grid=(bsz, n // tm),
        in_specs=[pl.BlockSpec((1, tm, d), lambda b, i: (b, i, 0)),
                  pl.BlockSpec((1, d), lambda b, i: (0, 0)),
                  pl.BlockSpec((1, 1, d), lambda b, i: (b, 0, 0)),
                  pl.BlockSpec((1, 1, d), lambda b, i: (b, 0, 0)),
                  pl.BlockSpec((d, nu), lambda b, i: (0, 0), **const),
                  pl.BlockSpec((d, nz), lambda b, i: (0, 0), **const)],
        out_specs=[pl.BlockSpec((1, tm, nu), lambda b, i: (b, i, 0)),
                   pl.BlockSpec((1, tm, nz), lambda b, i: (b, i, 0))],
        out_shape=[jax.ShapeDtypeStruct((bsz, n, nu), F32),
                   jax.ShapeDtypeStruct((bsz, n, nz), F32)],
        compiler_params=_params("parallel", "parallel"),
        name="inproj",
    )(x, g.reshape(1, d), sc, sh, wu, wz)


def _s5_constants(a_re, a_im, log_dt, b_re, b_im, c_re, c_im, d_skip, glu_w, glu_b):
    t = S5_CHUNK
    g, p = a_re.shape[1], a_re.shape[2]
    h = S5_GROUP
    gl = S5_LANE_GROUPS
    no = g // gl
    n = jnp.arange(t + 1, dtype=F32)[None, None, :, None]
    ar, ai = a_re.astype(F32)[:, :, None, :], a_im.astype(F32)[:, :, None, :]
    dt = jnp.exp(log_dt.astype(F32))[:, :, None, None]
    mag = jnp.exp(n * (ar * dt))
    pr, pi = mag * jnp.cos(n * (ai * dt)), mag * jnp.sin(n * (ai * dt))
    lr, li = pr[:, :, 1], pi[:, :, 1]
    ar, ai = ar[:, :, 0], ai[:, :, 0]
    den = ar * ar + ai * ai
    cf_re = (((lr - 1.0) * ar + li * ai) / den)[:, :, None, :]
    cf_im = ((li * ar - (lr - 1.0) * ai) / den)[:, :, None, :]
    br = jnp.swapaxes(b_re.astype(F32), 2, 3)
    bi = jnp.swapaxes(b_im.astype(F32), 2, 3)
    bb_re = cf_re * br - cf_im * bi
    bb_im = cf_re * bi + cf_im * br
    cr = jnp.swapaxes(c_re.astype(F32), 2, 3)[:, :, :, None, :]
    ci = jnp.swapaxes(c_im.astype(F32), 2, 3)[:, :, :, None, :]
    prt = jnp.swapaxes(pr, 2, 3)[..., None]
    pit = jnp.swapaxes(pi, 2, 3)[..., None]
    cl_re = cr * prt - ci * pit
    cl_im = cr * pit + ci * prt
    dims = (((2,), (1,)), ((0,), (0,)))
    flat = lambda a: a.reshape((N_DIR * g,) + a.shape[2:])
    lag = (lax.dot_general(flat(bb_re), flat(cl_re[:, :, :, :t]).reshape(N_DIR * g, p, t * h),
                           dims, precision=HIGHEST)
           - lax.dot_general(flat(bb_im), flat(cl_im[:, :, :, :t]).reshape(N_DIR * g, p, t * h),
                             dims, precision=HIGHEST)).reshape(N_DIR, g, h, t, h)
    s_idx = jnp.arange(t)[:, None]
    t_idx = jnp.arange(t)[None, :]
    lag_n = jnp.arange(t)[:, None, None]
    shift = jnp.stack([(t_idx - s_idx)[None] == lag_n, (s_idx - t_idx)[None] == lag_n]).astype(F32)
    kg = jnp.einsum('dnst,dginj->gsitj', shift, lag, precision=HIGHEST)
    k_rows = jnp.transpose(kg.reshape(no, gl, t, h, t * h), (0, 2, 1, 3, 4)).reshape(
        no, t * gl * h, t * h)
    ps_re = jnp.stack([pr[0, :, :t][:, ::-1], pr[1, :, :t]])[:, :, :, None, :]
    ps_im = jnp.stack([pi[0, :, :t][:, ::-1], pi[1, :, :t]])[:, :, :, None, :]
    f_re = ps_re * bb_re[:, :, None] - ps_im * bb_im[:, :, None]
    f_im = ps_re * bb_im[:, :, None] + ps_im * bb_re[:, :, None]
    fg = jnp.transpose(jnp.stack([f_re, f_im]), (2, 3, 4, 0, 1, 5))
    f_rows = jnp.transpose(fg.reshape(no, gl, t, h, 4 * p), (0, 2, 1, 3, 4)).reshape(
        no, t * gl * h, 4 * p)
    pick = lambda c: jnp.stack([c[0, :, :, 1:], c[1, :, :, 1:][:, :, ::-1]])
    eg = jnp.stack([pick(cl_re), -pick(cl_im)])
    eg = jnp.transpose(eg.reshape(2 * N_DIR, no, gl, p, t * h), (1, 0, 2, 3, 4))
    e_rows = eg.reshape(no, 4 * gl * p, t * h)
    lam_rows = [pr[0, :, t], pi[0, :, t], pr[1, :, t], pi[1, :, t]]
    fmat, emat, kmat = _s5_expand(f_rows, e_rows, k_rows)
    lam = jnp.stack(lam_rows, axis=1)
    lam = jnp.transpose(lam.reshape(no, gl, 4, p), (0, 2, 1, 3)).reshape(no, 4, gl * p)
    gmat = jnp.einsum('ab,oahk->oahbk', jnp.eye(gl, dtype=F32),
                      glu_w.astype(F32).reshape(no, gl, h, h)).reshape(no, gl * h, gl * h)
    dvec = d_skip.astype(F32).reshape(no, 1, gl * h)
    bvec = glu_b.astype(F32).reshape(no, 1, gl * h)
    return fmat, emat, kmat, lam, dvec, bvec, gmat.astype(BF16)


def _s5_expand_kernel(f_ref, e_ref, k_ref, rep_s_ref, rep_t_ref, fo_ref, eo_ref, ko_ref):
    def expand(x, rep, row_div, col_div):
        y = jnp.dot(x.astype(BF16), rep, preferred_element_type=F32)
        row = lax.broadcasted_iota(jnp.int32, y.shape, 0)
        col = lax.broadcasted_iota(jnp.int32, y.shape, 1)
        same = (row // row_div) % S5_LANE_GROUPS == (col // col_div) % S5_LANE_GROUPS
        return jnp.where(same, y, 0.0).astype(BF16)

    fo_ref[0] = expand(f_ref[0], rep_s_ref[...], S5_GROUP, S5_STATE)
    eo_ref[0] = expand(e_ref[0], rep_t_ref[...], S5_STATE, S5_GROUP)
    ko_ref[0] = expand(k_ref[0], rep_t_ref[...], S5_GROUP, S5_GROUP)


def _s5_expand(f_rows, e_rows, k_rows):
    no, rows_f, cols_f = f_rows.shape
    _, rows_e, cols_e = e_rows.shape
    gl = S5_LANE_GROUPS
    c = jnp.arange(cols_f * gl)
    rep_s = (jnp.arange(cols_f)[:, None]
             == (c // (gl * S5_STATE)) * S5_STATE + c % S5_STATE).astype(BF16)
    c = jnp.arange(cols_e * gl)
    rep_t = (jnp.arange(cols_e)[:, None]
             == (c // (gl * S5_GROUP)) * S5_GROUP + c % S5_GROUP).astype(BF16)
    blk = lambda r, cc: pl.BlockSpec((1, r, cc), lambda o: (o, 0, 0))
    full = lambda a: pl.BlockSpec(a.shape, lambda o: (0, 0))
    return pl.pallas_call(
        _s5_expand_kernel,
        grid=(no,),
        in_specs=[blk(rows_f, cols_f), blk(rows_e, cols_e), blk(rows_f, cols_e),
                  full(rep_s), full(rep_t)],
        out_specs=[blk(rows_f, cols_f * gl), blk(rows_e, cols_e * gl), blk(rows_f, cols_e * gl)],
        out_shape=[jax.ShapeDtypeStruct((no, rows_f, cols_f * gl), BF16),
                   jax.ShapeDtypeStruct((no, rows_e, cols_e * gl), BF16),
                   jax.ShapeDtypeStruct((no, rows_f, cols_e * gl), BF16)],
        compiler_params=_params("parallel"),
        name="s5_expand",
    )(f_rows, e_rows, k_rows, rep_s, rep_t)


def _s5_chunk_rows(u_ref, b, start, n_chunks):
    t = S5_CHUNK
    return jnp.concatenate(
        [u_ref[b, pl.ds(start + s, n_chunks, stride=t), :] for s in range(t)], axis=-1)


def _s5_in_kernel(u_ref, f_ref, z_ref):
    bsz, n_tok, _ = u_ref.shape
    for b in range(bsz):
        z_ref[b] = _mm(_s5_chunk_rows(u_ref, b, 0, n_tok // S5_CHUNK), f_ref[0])


def _s5_scan_kernel(n_ctx_chunks, z_ref, lam_ref, hs_ref):
    n_chunks = z_ref.shape[1]
    q = z_ref.shape[2] // 4
    rows = S5_SCAN_ROWS
    n_blocks, ctx_blocks = n_chunks // rows, n_ctx_chunks // rows
    lam = lam_ref[0]
    lf_re, lf_im, lb_re, lb_im = lam[0:1], lam[1:2], lam[2:3], lam[3:4]

    def block(k, carry):
        f_re, f_im, b_re, b_im = carry
        kb = jnp.where(k < ctx_blocks, ctx_blocks - 1 - k, n_blocks - 1 + ctx_blocks - k)
        rf = pl.multiple_of(k * rows, rows)
        rb = pl.multiple_of(kb * rows, rows)
        zf = z_ref[0, pl.ds(rf, rows), :]
        zb = z_ref[0, pl.ds(rb, rows), :]
        ent = [[], [], [], []]
        for j in range(rows):
            jb = rows - 1 - j
            for lst, val in zip(ent, (f_re, f_im, b_re, b_im)):
                lst.append(val)
            f_re, f_im = (lf_re * f_re - lf_im * f_im + zf[j:j + 1, 0:q],
                          lf_re * f_im + lf_im * f_re + zf[j:j + 1, 2 * q:3 * q])
            b_re, b_im = (lb_re * b_re - lb_im * b_im + zb[jb:jb + 1, q:2 * q],
                          lb_re * b_im + lb_im * b_re + zb[jb:jb + 1, 3 * q:4 * q])
        hs_ref[0, pl.ds(rf, rows), 0:q] = jnp.concatenate(ent[0], axis=0)
        hs_ref[0, pl.ds(rf, rows), 2 * q:3 * q] = jnp.concatenate(ent[1], axis=0)
        hs_ref[0, pl.ds(rb, rows), q:2 * q] = jnp.concatenate(ent[2][::-1], axis=0)
        hs_ref[0, pl.ds(rb, rows), 3 * q:4 * q] = jnp.concatenate(ent[3][::-1], axis=0)
        return f_re, f_im, b_re, b_im

    zero = jnp.zeros((1, q), F32)
    lax.fori_loop(0, n_blocks, block, (zero, zero, zero, zero))


def _s5_out_kernel(n_ctx, u_ref, hs_ref, e_ref, k_ref, g_ref, d_ref, b_ref, o_ref, y_scr):
    t = S5_CHUNK
    bsz, n_tok, lanes = u_ref.shape
    n_chunks = (n_tok - n_ctx) // t
    for b in range(bsz):
        x = _s5_chunk_rows(u_ref, b, n_ctx, n_chunks)
        y = _mm(x, k_ref[0]) + _mm(hs_ref[b, n_ctx // t:, :], e_ref[0])
        for s in range(t):
            y_scr[pl.ds(s, n_chunks, stride=t), :] = y[:, s * lanes:(s + 1) * lanes]
        y = jax.nn.gelu(y_scr[...] + d_ref[0] * u_ref[b, n_ctx:, :])
        gate = _mm(y, g_ref[0]) + b_ref[0]
        o_ref[b] = y * jax.nn.sigmoid(gate)


def _s5(u_all, consts, n_ctx):
    fmat, emat, kmat, lam, dvec, bvec, gmat = consts
    bsz, n_tok, width = u_all.shape
    t = S5_CHUNK
    n_chunks = n_tok // t
    no, rows_k, cols_f = fmat.shape
    assert n_ctx % (t * S5_SCAN_ROWS) == 0 and n_tok % (t * S5_SCAN_ROWS) == 0
    tok = pl.BlockSpec((1, n_tok, LANES), lambda o, b: (b, 0, o))
    state = pl.BlockSpec((1, n_chunks, cols_f), lambda o, b: (b, 0, o))
    full = lambda a: pl.BlockSpec((1,) + a.shape[1:], lambda o, b: (o, 0, 0))
    z = pl.pallas_call(
        _s5_in_kernel,
        grid=(no, bsz),
        in_specs=[tok, full(fmat)],
        out_specs=state,
        out_shape=jax.ShapeDtypeStruct((bsz, n_chunks, no * cols_f), F32),
        compiler_params=_params("parallel", "parallel"),
        name="s5_in",
    )(u_all, fmat)
    hs = pl.pallas_call(
        functools.partial(_s5_scan_kernel, n_ctx // t),
        grid=(no, bsz),
        in_specs=[state, full(lam)],
        out_specs=state,
        out_shape=jax.ShapeDtypeStruct((bsz, n_chunks, no * cols_f), F32),
        compiler_params=_params("parallel", "parallel"),
        name="s5_scan",
    )(z, lam)
    n_lat = n_tok - n_ctx
    return pl.pallas_call(
        functools.partial(_s5_out_kernel, n_ctx),
        grid=(no, bsz),
        in_specs=[tok, state, full(emat), full(kmat), full(gmat), full(dvec), full(bvec)],
        out_specs=pl.BlockSpec((1, n_lat, LANES), lambda o, b: (b, 0, o)),
        out_shape=jax.ShapeDtypeStruct((bsz, n_lat, width), F32),
        scratch_shapes=[pltpu.VMEM((n_lat, LANES), F32)],
        compiler_params=_params("parallel", "parallel"),
        name="s5_out",
    )(u_all, hs, emat, kmat, gmat, dvec, bvec)


def _rw_feat_kernel(width, has_vert, n_tiles, z_ref, zp_ref, zn_ref, mu_ref, w0_ref, w2_ref,
                    a0_ref, a2_ref, g2_ref, kk_ref, ka_ref, rk_ref, hsel_ref, hselt_ref,
                    r_ref, v_ref, kh_ref, g_ref, bv_ref, lw_ref, q_ref, kt_ref):
    z = z_ref[0]
    t0, cz = z.shape
    rw = r_ref.shape[-1]
    mu = mu_ref[...]
    tok = lax.broadcasted_iota(jnp.int32, (t0, 1), 0)
    col = tok % width
    left = pltpu.roll(z, 1, 0)
    right = pltpu.roll(z, t0 - 1, 0)
    out = z + jnp.where(col != 0, mu[0:1] * (left - z), 0.0)
    out = out + jnp.where(col != width - 1, mu[1:2] * (right - z), 0.0)
    if has_vert:
        i = pl.program_id(1)
        up = jnp.concatenate([zp_ref[0], z[:t0 - width]], axis=0)
        down = jnp.concatenate([z[width:], zn_ref[0]], axis=0)
        up_ok = jnp.logical_or(i > 0, tok >= width)
        down_ok = jnp.logical_or(i < n_tiles - 1, tok < t0 - width)
        out = out + jnp.where(up_ok, mu[2:3] * (up - z), 0.0)
        out = out + jnp.where(down_ok, mu[3:4] * (down - z), 0.0)
    r = out[:, 0:rw]
    k = out[:, rw:2 * rw]
    v = out[:, 2 * rw:3 * rw]
    o = 3 * rw
    lora = w2_ref.shape[0]
    xw = out[:, o:o + lora]
    xa = out[:, o + lora:o + 2 * lora]
    xg = out[:, o + 2 * lora:]
    dec = w0_ref[...] + _mm(jnp.tanh(xw), w2_ref[...])
    lw = -math.exp(-0.5) * jax.nn.sigmoid(dec)
    a = jax.nn.sigmoid(a0_ref[...] + _mm(xa, a2_ref[...]))
    g = _mm(jax.nn.sigmoid(xg), g2_ref[...])
    kk = k * kk_ref[...]
    ssq = _mm_sel(kk * kk, hsel_ref[...])
    inv = 1.0 / jnp.maximum(jnp.sqrt(ssq), 1e-12)
    kh = kk * _mm_sel(inv, hselt_ref[...])
    ka = ka_ref[...]
    kt_sum = jnp.zeros_like(k)
    for d in range(N_DIR):
        a_d = a[:, d * rw:(d + 1) * rw]
        kt_d = k * (1.0 + (a_d - 1.0) * ka)
        kt_sum = kt_sum + kt_d
        lw_ref[d, 0] = lw[:, d * rw:(d + 1) * rw]
        q_ref[d, 0] = a_d * kh
        kt_ref[d, 0] = kt_d
    bonus = _mm_sel(_mm_sel(r * kt_sum * rk_ref[...], hsel_ref[...]), hselt_ref[...])
    r_ref[0] = r
    v_ref[0] = v
    kh_ref[0] = kh
    g_ref[0] = g
    bv_ref[0] = bonus * v


def _rw_features(z, width, has_vert, mu, w0, w2blk, a0, a2blk, g2, k_k, k_a, r_k, hsel, hselt):
    bsz, n, cz = z.shape
    rw = k_k.shape[-1]
    t0 = ROW_TILE
    n_tiles = n // t0
    per = t0 // GRID_W
    nblk = n // GRID_W
    full = lambda a: pl.BlockSpec(a.shape, lambda b, i: (0,) * a.ndim)
    tok = pl.BlockSpec((1, t0, rw), lambda b, i: (b, i, 0))
    dtok = pl.BlockSpec((N_DIR, 1, t0, rw), lambda b, i: (0, b, i, 0))
    consts = (mu, w0, w2blk, a0, a2blk, g2, k_k, k_a, r_k, hsel, hselt)
    return pl.pallas_call(
        functools.partial(_rw_feat_kernel, width, has_vert, n_tiles),
        grid=(bsz, n_tiles),
        in_specs=[pl.BlockSpec((1, t0, cz), lambda b, i: (b, i, 0)),
                  pl.BlockSpec((1, GRID_W, cz), lambda b, i: (b, jnp.maximum(i * per - 1, 0), 0)),
                  pl.BlockSpec((1, GRID_W, cz),
                               lambda b, i: (b, jnp.minimum(i * per + per, nblk - 1), 0))]
        + [full(a) for a in consts],
        out_specs=[tok] * 5 + [dtok] * 3,
        out_shape=[jax.ShapeDtypeStruct((bsz, n, rw), F32)] * 5
        + [jax.ShapeDtypeStruct((N_DIR, bsz, n, rw), F32)] * 3,
        compiler_params=_params("parallel", "parallel"),
        name="rw_features",
    )(z, z, z, *consts)


def _rw_chunk_kernel(r_ref, v_ref, kh_ref, lw_ref, q_ref, kt_ref, g_ref, h_ref, rh_ref, y0_ref):
    rev = pl.program_id(0) == 1
    n = RW_CHUNK
    hd = RW_HEAD
    n_sub = lw_ref.shape[2] // n
    row = lax.broadcasted_iota(jnp.int32, (n, n), 0)
    col = lax.broadcasted_iota(jnp.int32, (n, n), 1)
    ahead = (row - col) * jnp.where(rev, -1, 1)
    incl = (ahead >= 0).astype(F32)
    strict = (ahead > 0).astype(F32)
    eye = (row == col).astype(F32)
    same_block = [(jnp.right_shift(row, s) == jnp.right_shift(col, s)).astype(F32)
                  for s in range(3, n.bit_length())]
    n_heads = lw_ref.shape[-1] // hd
    pt, rt, v, qt, ktt, qh, kth, gam, where = [], [], [], [], [], [], [], [], []
    for c in range(n_sub):
        tok = slice(c * n, (c + 1) * n)
        lw = lw_ref[0, 0, tok, :]
        b_incl = _sel_mm(incl, lw)
        btot = jnp.sum(lw, axis=0, keepdims=True)
        e_neg = jnp.exp(-b_incl)
        e_rem = jnp.exp(btot - b_incl)
        q_c, kt_c = q_ref[0, 0, tok, :], kt_ref[0, 0, tok, :]
        pt_all = kh_ref[0, tok, :] * jnp.exp(b_incl - lw)
        rt_all = r_ref[0, tok, :] * jnp.exp(b_incl)
        qt_all, ktt_all = q_c * e_neg, kt_c * e_neg
        qh_all, kth_all = q_c * e_rem, kt_c * e_rem
        gam_all = jnp.exp(btot)
        v_all = v_ref[0, tok, :]
        for h in range(n_heads):
            sl = slice(h * hd, (h + 1) * hd)
            for lst, arr in zip((pt, rt, v, qt, ktt, qh, kth, gam),
                                (pt_all, rt_all, v_all, qt_all, ktt_all, qh_all, kth_all, gam_all)):
                lst.append(arr[:, sl])
            where.append((c, tok, sl))
    heads = range(len(where))
    a4 = [_mm(jnp.concatenate([pt[h], rt[h]], 0), jnp.concatenate([qt[h], ktt[h]], 0),
              ((1,), (1,))) for h in heads]
    nmat = [strict * a4[h][:n, :n] for h in heads]
    akv = [_mm(strict * a4[h][:n, n:], v[h]) for h in heads]
    nd = [same_block[0] * nmat[h] for h in heads]
    x = [_mm(nd[h], nd[h]) for h in heads]
    m = [eye - nd[h] for h in heads]
    m = [m[h] + _mm(m[h], x[h]) for h in heads]
    x = [_mm(x[h], x[h]) for h in heads]
    m = [m[h] + _mm(m[h], x[h]) for h in heads]
    for lvl in range(1, len(same_block)):
        ring = same_block[lvl] - same_block[lvl - 1]
        t = [_mm(m[h], ring * nmat[h]) for h in heads]
        m = [m[h] - _mm(t[h], m[h]) for h in heads]
    wu = [_mm(m[h], jnp.concatenate([pt[h], akv[h]], 1)) for h in heads]
    gh = [_mm(wu[h], qh[h], ((0,), (0,))) for h in heads]
    vk = [_mm(v[h], kth[h], ((0,), (0,))) for h in heads]
    lwu = [_mm(incl * a4[h][n:, :n], wu[h]) for h in heads]
    lv = [_mm(incl * a4[h][n:, n:], v[h]) for h in heads]
    for h in heads:
        c, tok, sl = where[h]
        g_ref[0, 0, c, :, sl] = eye * gam[h] - gh[h][:hd]
        h_ref[0, 0, c, :, sl] = vk[h] - gh[h][hd:]
        rh_ref[0, 0, tok, sl] = rt[h] - lwu[h][:, :hd]
        y0_ref[0, 0, tok, sl] = lv[h] - lwu[h][:, hd:]


def _rw_chunks(r, v, kh, lw, q, kt):
    bsz, n, rw = r.shape
    nc = n // RW_CHUNK
    per = RW_CHUNKS_PER_STEP
    tok = pl.BlockSpec((1, per * RW_CHUNK, rw), lambda d, b, c: (b, c, 0))
    dtok = pl.BlockSpec((1, 1, per * RW_CHUNK, rw), lambda d, b, c: (d, b, c, 0))
    mat = pl.BlockSpec((1, 1, per, RW_HEAD, rw), lambda d, b, c: (d, b, c, 0, 0))
    return pl.pallas_call(
        _rw_chunk_kernel,
        grid=(N_DIR, bsz, nc // per),
        in_specs=[tok, tok, tok, dtok, dtok, dtok],
        out_specs=[mat, mat, dtok, dtok],
        out_shape=[jax.ShapeDtypeStruct((N_DIR, bsz, nc, RW_HEAD, rw), F32)] * 2
        + [jax.ShapeDtypeStruct((N_DIR, bsz, n, rw), F32)] * 2,
        compiler_params=_params("parallel", "parallel", "parallel"),
        name="rw_chunks",
    )(r, v, kh, lw, q, kt)


def _rw_state_kernel(emit_y, s0_ref, gf_ref, gb_ref, hf_ref, hb_ref, rhf_ref, rhb_ref, y0f_ref,
                     y0b_ref, *rest):
    if emit_y:
        yf_ref, yb_ref, sfin_ref, s_scr = rest
    else:
        sfin_ref, s_scr = rest
    c = pl.program_id(0)
    hd = RW_HEAD
    n_dir, bsz = s_scr.shape[0], s_scr.shape[1]

    @pl.when(c == 0)
    def _():
        s_scr[...] = s0_ref[...]

    sls = [slice(h * hd, (h + 1) * hd) for h in range(s_scr.shape[-1] // hd)]
    g_refs, h_refs = (gf_ref, gb_ref), (hf_ref, hb_ref)
    rh_refs, y0_refs = (rhf_ref, rhb_ref), (y0f_ref, y0b_ref)
    chains = [(d, b) for d in range(n_dir) for b in range(bsz)]
    s_bf = {k: s_scr[k[0], k[1]].astype(BF16) for k in chains}
    g_bf = {k: g_refs[k[0]][0, k[1], 0].astype(BF16) for k in chains}
    s_new = {k: [_mm(s_bf[k][:, sl], g_bf[k][:, sl]) for sl in sls] for k in chains}
    if emit_y:
        y_refs = (yf_ref, yb_ref)
        rh_bf = {k: rh_refs[k[0]][0, k[1]].astype(BF16) for k in chains}
        y = {k: [_mm(rh_bf[k][:, sl], s_bf[k][:, sl], ((1,), (1,))) for sl in sls] for k in chains}
        for d, b in chains:
            y_refs[d][b] = jnp.concatenate(y[d, b], axis=-1) + y0_refs[d][0, b]
    for d, b in chains:
        s_scr[d, b] = jnp.concatenate(s_new[d, b], axis=-1) + h_refs[d][0, b, 0]

    @pl.when(c == pl.num_programs(0) - 1)
    def _():
        sfin_ref[...] = s_scr[...]


def _rw_state(s0, gmat, hmat, rh, y0, emit_y):
    n_dir, bsz, nc, hd, rw = gmat.shape
    n = rh.shape[2]
    chunk = lambda d, c: c + d * (nc - 1 - 2 * c)
    mat = lambda d: pl.BlockSpec((1, bsz, 1, hd, rw), lambda c: (d, 0, chunk(d, c), 0, 0))
    dtok = lambda d: pl.BlockSpec((1, bsz, RW_CHUNK, rw), lambda c: (d, 0, chunk(d, c), 0))
    st = pl.BlockSpec((n_dir, bsz, hd, rw), lambda c: (0, 0, 0, 0))
    out_specs = [st]
    out_shape = [jax.ShapeDtypeStruct((n_dir, bsz, hd, rw), F32)]
    if emit_y:
        out_specs = [pl.BlockSpec((bsz, RW_CHUNK, rw), lambda c, d=d: (0, chunk(d, c), 0))
                     for d in range(n_dir)] + out_specs
        out_shape = [jax.ShapeDtypeStruct((bsz, n, rw), F32)] * n_dir + out_shape
    return pl.pallas_call(
        functools.partial(_rw_state_kernel, emit_y),
        grid=(nc,),
        in_specs=[st, mat(0), mat(1), mat(0), mat(1), dtok(0), dtok(1), dtok(0), dtok(1)],
        out_specs=out_specs,
        out_shape=out_shape,
        scratch_shapes=[pltpu.VMEM((n_dir, bsz, hd, rw), F32)],
        compiler_params=_params("arbitrary"),
        name="rw_state_y" if emit_y else "rw_state",
    )(s0, gmat, gmat, hmat, hmat, rh, rh, y0, y0)


def _outproj_kernel(x_ref, y5_ref, yf_ref, yb_ref, bv_ref, g_ref, lnw_ref, lnb_ref, hsel_ref,
                    hselt_ref, wo_ref, gt1_ref, g2_ref, sc2_ref, sh2_ref, rw_ref, rb_ref,
                    x1_ref, f_ref, idx_ref, gate_ref, rank_ref, cnt_ref, carry):
    first = jnp.logical_and(pl.program_id(0) == 0, pl.program_id(1) == 0)

    @pl.when(first)
    def _():
        carry[...] = jnp.zeros_like(carry)

    inv_hd = 1.0 / RW_HEAD
    y = yf_ref[0] + yb_ref[0]
    mean = _mm_sel(_mm_sel(y, hsel_ref[...]), hselt_ref[...]) * inv_hd
    yc = y - mean
    var = _mm_sel(_mm_sel(yc * yc, hsel_ref[...]), hselt_ref[...]) * inv_hd
    yn = yc * lax.rsqrt(var + RW_GN_EPS) * lnw_ref[...] + lnb_ref[...]
    yr = (yn + bv_ref[0]) * g_ref[0]
    mix = _mm(jnp.concatenate([y5_ref[0], yr], axis=-1), wo_ref[...])
    x1 = x_ref[0] + gt1_ref[0] * mix
    x1_ref[0] = x1
    f = _rms_mod(x1, g2_ref[...], sc2_ref[0], sh2_ref[0])
    n_lines = f.shape[1] // LANES
    for s in range(n_lines):
        f_ref[0, pl.ds(s, f.shape[0], stride=n_lines), :] = f[:, s * LANES:(s + 1) * LANES]

    logits = _mm_f32(f, rw_ref[...]) + rb_ref[...]
    tm, ne = logits.shape
    eid = lax.broadcasted_iota(jnp.int32, (tm, ne), 1)
    work = logits
    sel = jnp.zeros((tm, ne), F32)
    idx_cols, val_cols = [], []
    for _ in range(TOP_K):
        top = jnp.max(work, axis=-1, keepdims=True)
        pick = jnp.min(jnp.where(work == top, eid, ne), axis=-1, keepdims=True)
        hit = eid == pick
        sel = jnp.where(hit, 1.0, sel)
        work = jnp.where(hit, -jnp.inf, work)
        idx_cols.append(pick)
        val_cols.append(top)
    exps = [jnp.exp(vk - val_cols[0]) for vk in val_cols]
    denom = exps[0] + exps[1] + exps[2] + exps[3]
    row = lax.broadcasted_iota(jnp.int32, (tm, tm), 0)
    colm = lax.broadcasted_iota(jnp.int32, (tm, tm), 1)
    before = _mm((colm < row).astype(F32), sel) + carry[0:1, :]
    rank_cols = [jnp.sum(jnp.where(eid == ic, before, 0.0), axis=-1, keepdims=True)
                 for ic in idx_cols]
    idx_ref[0] = jnp.concatenate(idx_cols, axis=-1)
    gate_ref[0] = jnp.concatenate([e / denom for e in exps], axis=-1)
    rank_ref[0] = jnp.concatenate(rank_cols, axis=-1).astype(jnp.int32)
    total = carry[0:1, :] + jnp.sum(sel, axis=0, keepdims=True)
    carry[...] = jnp.broadcast_to(total, carry.shape)
    cnt_ref[...] = jnp.broadcast_to(total, cnt_ref.shape).astype(jnp.int32)


def _outproj(x, y5, yf, yb, bv, g, ln_w, ln_b, hsel, hselt, wo, gt1, g2, sc2, sh2, router_w,
             router_b):
    bsz, n, d = x.shape
    rw = y5.shape[-1]
    ne = router_w.shape[-1]
    tm = ROW_TILE
    row = lambda w: pl.BlockSpec((1, tm, w), lambda b, i: (b, i, 0))
    vec = lambda w: pl.BlockSpec((1, w), lambda b, i: (0, 0))
    bvec = pl.BlockSpec((1, 1, d), lambda b, i: (b, 0, 0))
    full = lambda a: pl.BlockSpec(a.shape, lambda b, i: (0,) * a.ndim)
    return pl.pallas_call(
        _outproj_kernel,
        grid=(bsz, n // tm),
        in_specs=[row(d), row(rw), row(rw), row(rw),
                  row(rw), row(rw), vec(rw), vec(rw), full(hsel), full(hselt),
                  pl.BlockSpec(wo.shape, lambda b, i: (0, 0), pipeline_mode=pl.Buffered(1)),
                  bvec, vec(d), bvec, bvec, full(router_w), vec(ne)],
        out_specs=[row(d), pl.BlockSpec((1, tm * (d // LANES), LANES), lambda b, i: (b, i, 0)),
                   row(TOP_K), row(TOP_K), row(TOP_K),
                   pl.BlockSpec((8, ne), lambda b, i: (0, 0))],
        out_shape=[jax.ShapeDtypeStruct((bsz, n, d), F32),
                   jax.ShapeDtypeStruct((bsz, n * (d // LANES), LANES), F32),
                   jax.ShapeDtypeStruct((bsz, n, TOP_K), jnp.int32),
                   jax.ShapeDtypeStruct((bsz, n, TOP_K), F32),
                   jax.ShapeDtypeStruct((bsz, n, TOP_K), jnp.int32),
                   jax.ShapeDtypeStruct((8, ne), jnp.int32)],
        scratch_shapes=[pltpu.VMEM((8, ne), F32)],
        compiler_params=_params("arbitrary", "arbitrary"),
        name="outproj_router",
    )(x, y5, yf, yb, bv, g, ln_w.reshape(1, rw), ln_b.reshape(1, rw), hsel, hselt, wo, gt1,
      g2.reshape(1, d), sc2, sh2, router_w, router_b.reshape(1, ne))


def _dispatch_kernel(tile_rows, line_ref, fill_ref, f_ref, xs_ref, zeros, sem, sem_fill):
    tr = tile_rows
    tm = f_ref.shape[0] // tr
    blk = zeros.shape[0]

    @pl.when(pl.program_id(0) == 0)
    def _():
        zeros[...] = jnp.zeros_like(zeros)

        def fill(b):
            dst = xs_ref.at[pl.ds(pl.multiple_of(b * blk, blk), blk)]
            return pltpu.make_async_copy(zeros, dst, sem_fill)

        def start(b, carry):
            @pl.when(fill_ref[b] == 1)
            def _():
                fill(b).start()
            return carry

        def wait(b, carry):
            @pl.when(fill_ref[b] == 1)
            def _():
                fill(b).wait()
            return carry

        lax.fori_loop(0, fill_ref.shape[0], start, 0)
        lax.fori_loop(0, fill_ref.shape[0], wait, 0)

    def issue(t, _):
        src = f_ref.at[pl.ds(pl.multiple_of(t * tr, tr), tr)]
        for k in range(TOP_K):
            dst = pl.multiple_of(line_ref[t * TOP_K + k], tr)
            pltpu.make_async_copy(src, xs_ref.at[pl.ds(dst, tr)], sem).start(priority=k % 2)
        return 0

    lax.fori_loop(0, tm, issue, 0)
    for _ in range(TOP_K):
        pltpu.make_async_copy(f_ref, xs_ref.at[pl.ds(0, tm * tr)], sem).wait()


def _dispatch(slot_flat, fill, f_lines, n_tok, n_slots):
    lanes = f_lines.shape[1]
    tr = f_lines.shape[0] // n_tok
    tm = ROW_TILE
    smem_tok = pl.BlockSpec((tm * TOP_K,), lambda i: (i,), memory_space=pltpu.SMEM)
    smem_all = lambda a: pl.BlockSpec(a.shape, lambda i: (0,), memory_space=pltpu.SMEM)
    return pl.pallas_call(
        functools.partial(_dispatch_kernel, tr),
        grid=(n_tok // tm,),
        in_specs=[smem_tok, smem_all(fill), pl.BlockSpec((tm * tr, lanes), lambda i: (i, 0))],
        out_specs=pl.BlockSpec(memory_space=pl.ANY),
        out_shape=jax.ShapeDtypeStruct((n_slots * tr, lanes), f_lines.dtype),
        scratch_shapes=[pltpu.VMEM((MOE_ROWS * tr, lanes), f_lines.dtype),
                        pltpu.SemaphoreType.DMA(()), pltpu.SemaphoreType.DMA(())],
        compiler_params=_params("arbitrary"),
        name="moe_dispatch",
    )(slot_flat * tr, fill, f_lines)


def _moe_kernel(sbe_ref, sbrow_ref, sbn_ref, nsb_ref, xs_ref, wg_ref, wl_ref, wd_ref, bg_ref,
                bl_ref, bd_ref, y_ref, x_stage, x_bf, acc, y_stage, sem_in, sem_out):
    sb, j = pl.program_id(0), pl.program_id(1)
    nj = pl.num_programs(1)
    rb = MOE_ROWS
    n_tiles = y_stage.shape[1] // rb
    valid = sb < nsb_ref[0]

    def copy_in(row, slot):
        src = xs_ref.at[pl.ds(pl.multiple_of(row * n_tiles, rb * n_tiles), rb * n_tiles)]
        return pltpu.make_async_copy(src, x_stage.at[slot], sem_in.at[slot])

    def copy_out(row, slot):
        dst = y_ref.at[pl.ds(pl.multiple_of(row * n_tiles, rb * n_tiles), rb * n_tiles)]
        return pltpu.make_async_copy(y_stage.at[slot], dst, sem_out.at[slot])

    @pl.when(valid)
    def _():
        n_blk = sbn_ref[sb]
        row0 = sbrow_ref[sb] * rb

        @pl.when(j == 0)
        def _():
            copy_in(row0, 0).start()

            def load(k, carry):
                slot = k % 2

                @pl.when(k + 1 < n_blk)
                def _():
                    copy_in(row0 + (k + 1) * rb, 1 - slot).start()

                copy_in(row0 + k * rb, slot).wait()
                r = pl.multiple_of(k * rb, rb)
                for s in range(n_tiles):
                    x_bf[pl.ds(r, rb), s * LANES:(s + 1) * LANES] = (
                        x_stage[slot, pl.ds(s, rb, stride=n_tiles), :].astype(BF16))
                return carry

            lax.fori_loop(0, n_blk, load, 0)

            def clear(k, carry):
                acc[pl.ds(pl.multiple_of(k * rb, rb), rb), :] = jnp.broadcast_to(
                    bd_ref[0], (rb, acc.shape[1]))
                return carry

            lax.fori_loop(0, n_blk, clear, 0)

        def accumulate(r):
            x = x_bf[r, :]
            glu = jnp.dot(x, wg_ref[0].astype(BF16), preferred_element_type=F32) + bg_ref[0]
            lin = jnp.dot(x, wl_ref[0].astype(BF16), preferred_element_type=F32) + bl_ref[0]
            glu = jnp.minimum(glu, SWIGLU_LIMIT)
            lin = jnp.clip(lin, -SWIGLU_LIMIT, SWIGLU_LIMIT)
            act = ((lin + 1.0) * glu * jax.nn.sigmoid(SWIGLU_ALPHA * glu)).astype(BF16)
            acc[r, :] += jnp.dot(act, wd_ref[0].astype(BF16), preferred_element_type=F32)

        def row_pair(kk, carry):
            accumulate(pl.ds(pl.multiple_of(kk * 2 * rb, 2 * rb), 2 * rb))
            return carry

        lax.fori_loop(0, n_blk // 2, row_pair, 0)

        @pl.when(n_blk % 2 == 1)
        def _():
            accumulate(pl.ds(pl.multiple_of((n_blk - 1) * rb, rb), rb))

        @pl.when(j == nj - 1)
        def _():
            def store(k, carry):
                slot = k % 2

                @pl.when(k >= 2)
                def _():
                    copy_out(row0, slot).wait()

                r = pl.multiple_of(k * rb, rb)
                for s in range(n_tiles):
                    y_stage[slot, pl.ds(s, rb, stride=n_tiles), :] = (
                        acc[pl.ds(r, rb), s * LANES:(s + 1) * LANES])
                copy_out(row0 + k * rb, slot).start()
                return carry

            lax.fori_loop(0, n_blk, store, 0)

            @pl.when(n_blk >= 2)
            def _():
                copy_out(row0, n_blk % 2).wait()

            copy_out(row0, (n_blk - 1) % 2).wait()

    @pl.when(jnp.logical_not(valid))
    def _():
        q = (sb - nsb_ref[0]) * nj + j
        last = jnp.maximum(nsb_ref[0] - 1, 0)
        first_unused = sbrow_ref[last] + sbn_ref[last]
        n_unused = y_ref.shape[0] // (rb * n_tiles) - first_unused

        @pl.when(q < n_unused)
        def _():
            y_stage[0] = jnp.zeros(y_stage.shape[1:], y_stage.dtype)
            cp = copy_out((first_unused + q) * rb, 0)
            cp.start()
            cp.wait()


def _moe_experts(sb_e, sb_row, sb_n, n_sb, xs_lines, w_gu, b_gu, w_dn, b_dn):
    ne, d, two_de = w_gu.shape
    lanes = xs_lines.shape[1]
    n_slots = xs_lines.shape[0] // (d // lanes)
    de = two_de // 2
    th = MOE_HIDDEN_TILE
    nj = de // th
    n_groups = sb_e.shape[0]
    grp = lambda sb, nsb: jnp.maximum(jnp.minimum(sb, nsb[0] - 1), 0)
    tile = lambda sb, j, nsb: jnp.where(sb < nsb[0], j, nj - 1)
    grid_spec = pltpu.PrefetchScalarGridSpec(
        num_scalar_prefetch=4,
        grid=(n_groups, nj),
        in_specs=[pl.BlockSpec(memory_space=pl.ANY),
                  pl.BlockSpec((1, d, th), lambda sb, j, e, r, n, nsb:
                               (e[grp(sb, nsb)], 0, tile(sb, j, nsb))),
                  pl.BlockSpec((1, d, th), lambda sb, j, e, r, n, nsb:
                               (e[grp(sb, nsb)], 0, nj + tile(sb, j, nsb))),
                  pl.BlockSpec((1, th, d), lambda sb, j, e, r, n, nsb:
                               (e[grp(sb, nsb)], tile(sb, j, nsb), 0)),
                  pl.BlockSpec((1, 1, th), lambda sb, j, e, r, n, nsb:
                               (e[grp(sb, nsb)], 0, tile(sb, j, nsb))),
                  pl.BlockSpec((1, 1, th), lambda sb, j, e, r, n, nsb:
                               (e[grp(sb, nsb)], 0, nj + tile(sb, j, nsb))),
                  pl.BlockSpec((1, 1, d), lambda sb, j, e, r, n, nsb: (e[grp(sb, nsb)], 0, 0))],
        out_specs=pl.BlockSpec(memory_space=pl.ANY),
        scratch_shapes=[pltpu.VMEM((2, MOE_ROWS * (d // lanes), lanes), F32),
                        pltpu.VMEM((MOE_GROUP_ROWS, d), BF16),
                        pltpu.VMEM((MOE_GROUP_ROWS, d), F32),
                        pltpu.VMEM((2, MOE_ROWS * (d // lanes), lanes), F32),
                        pltpu.SemaphoreType.DMA((2,)), pltpu.SemaphoreType.DMA((2,))])
    return pl.pallas_call(
        _moe_kernel,
        grid_spec=grid_spec,
        out_shape=jax.ShapeDtypeStruct((n_slots * (d // lanes), lanes), F32),
        compiler_params=_params("arbitrary", "arbitrary"),
        name="moe_experts",
    )(sb_e, sb_row, sb_n, n_sb, xs_lines, w_gu, w_gu, w_dn, b_gu.reshape(ne, 1, two_de),
      b_gu.reshape(ne, 1, two_de), b_dn.reshape(ne, 1, d))


def _combine_kernel(line_ref, x1_ref, gate_ref, gt2_ref, fg_ref, y_ref, o_ref, buf, sem):
    tm, d = x1_ref.shape
    tr = d // LANES

    def issue(t, _):
        dst = pl.ds(pl.multiple_of(t * tr, tr), tr)
        for k in range(TOP_K):
            src = pl.multiple_of(line_ref[t * TOP_K + k], tr)
            pltpu.make_async_copy(y_ref.at[pl.ds(src, tr)], buf.at[k, dst],
                                  sem).start(priority=k % 2)
        return 0

    lax.fori_loop(0, tm, issue, 0)
    for k in range(TOP_K):
        pltpu.make_async_copy(y_ref.at[pl.ds(0, tm * tr)], buf.at[k], sem).wait()
    gate = gate_ref[...]
    cols = []
    for s in range(tr):
        acc = gate[:, 0:1] * buf[0, pl.ds(s, tm, stride=tr), :]
        for k in range(1, TOP_K):
            acc = acc + gate[:, k:k + 1] * buf[k, pl.ds(s, tm, stride=tr), :]
        cols.append(acc)
    x2 = x1_ref[...] + gt2_ref[0] * jnp.concatenate(cols, axis=-1)
    ms = jnp.mean(x2 * x2, axis=-1, keepdims=True)
    o_ref[...] = x2 * lax.rsqrt(ms + NORM_EPS) * fg_ref[...]


def _combine(slot_flat, x1, gate4, gt2, fg, y_lines, tiles_per_batch):
    n_tok, d = x1.shape
    lanes = y_lines.shape[1]
    tm = ROW_TILE
    smem_tok = pl.BlockSpec((tm * TOP_K,), lambda i: (i,), memory_space=pltpu.SMEM)
    return pl.pallas_call(
        _combine_kernel,
        grid=(n_tok // tm,),
        in_specs=[smem_tok,
                  pl.BlockSpec((tm, d), lambda i: (i, 0)),
                  pl.BlockSpec((tm, TOP_K), lambda i: (i, 0)),
                  pl.BlockSpec((1, 1, d), lambda i: (i // tiles_per_batch, 0, 0)),
                  pl.BlockSpec((1, d), lambda i: (0, 0)),
                  pl.BlockSpec(memory_space=pl.ANY)],
        out_specs=pl.BlockSpec((tm, d), lambda i: (i, 0)),
        out_shape=jax.ShapeDtypeStruct((n_tok, d), F32),
        scratch_shapes=[pltpu.VMEM((TOP_K, tm * (d // lanes), lanes), F32),
                        pltpu.SemaphoreType.DMA(())],
        compiler_params=_params("arbitrary"),
        name="moe_combine_final",
    )(slot_flat * (d // lanes), x1, gate4, gt2, fg.reshape(1, d), y_lines)


def _block_diag2(w):
    z = jnp.zeros_like(w[0])
    return jnp.concatenate([jnp.concatenate([w[0], z], 1), jnp.concatenate([z, w[1]], 1)], 0)


def kernel(x, c, ctx, c_ctx, mod_w, mod_b, norm1_g, w_in, s5_a_re, s5_a_im, s5_log_dt, s5_b_re,
           s5_b_im, s5_c_re, s5_c_im, s5_d, s5_glu_w, s5_glu_b, rw_mu, rw_w0, rw_w2, rw_a0, rw_a2,
           rw_g2, rw_k_k, rw_k_a, rw_r_k, rw_ln_w, rw_ln_b, w_out, norm2_g, router_w, router_b,
           exp_w_gu, exp_b_gu, exp_w_dn, exp_b_dn, final_g):
    assert mod_w.shape[0] == 1, "single-layer stack only"
    bsz, n_lat, d = x.shape
    n_ctx = ctx.shape[1]
    assert bsz == 2 and n_ctx % ROW_TILE == 0 and n_lat % ROW_TILE == 0
    s5w = s5_d.shape[-1]
    rww = rw_k_k.shape[-1]
    n_heads = rww // RW_HEAD
    ne = router_w.shape[-1]

    cond8 = jnp.zeros((8, d), F32).at[:bsz].set(c).at[bsz].set(c_ctx)
    mod = _adaln(cond8, mod_w[0], mod_b[0])
    sh1, sc1, gt1, sh2, sc2, gt2 = [m[:bsz, None, :] for m in jnp.split(mod, 6, axis=-1)]
    csh1, csc1 = [jnp.broadcast_to(m[bsz][None, None, :], (bsz, 1, d))
                  for m in jnp.split(mod, 6, axis=-1)[:2]]

    wu = w_in[0][:, :s5w].astype(BF16)
    wz = w_in[0][:, s5w:].astype(BF16)
    u_lat, z_lat = _inproj(x, norm1_g[0], sc1, sh1, wu, wz)
    u_ctx, z_ctx = _inproj(ctx, norm1_g[0], csc1, csh1, wu, wz)

    s5c = _s5_constants(s5_a_re[0], s5_a_im[0], s5_log_dt[0], s5_b_re[0], s5_b_im[0], s5_c_re[0],
                        s5_c_im[0], s5_d[0], s5_glu_w[0], s5_glu_b[0])
    y5 = _s5(jnp.concatenate([u_ctx, u_lat], axis=1), s5c, n_ctx)

    lanes_idx = jnp.arange(rww) // RW_HEAD
    hsel = (lanes_idx[:, None] == jnp.arange(LANES)[None, :]).astype(BF16)
    hselt = hsel.T
    feat_consts = (rw_mu[0], rw_w0[0].reshape(1, N_DIR * rww),
                   _block_diag2(rw_w2[0]).astype(BF16), rw_a0[0].reshape(1, N_DIR * rww),
                   _block_diag2(rw_a2[0]).astype(BF16), rw_g2[0].astype(BF16),
                   rw_k_k[0].reshape(1, rww), rw_k_a[0].reshape(1, rww),
                   rw_r_k[0].reshape(1, rww), hsel, hselt)
    fc = _rw_features(z_ctx, n_ctx, False, *feat_consts)
    fl = _rw_features(z_lat, GRID_W, True, *feat_consts)

    def scan_inputs(f):
        r, v, kh, _, _, lw, q, kt = f
        return r, v, kh, lw, q, kt

    gc, hc, rhc, y0c = _rw_chunks(*scan_inputs(fc))
    gl, hl, rhl, y0l = _rw_chunks(*scan_inputs(fl))
    s_zero = jnp.zeros((N_DIR, bsz, RW_HEAD, rww), F32)
    (s_ctx,) = _rw_state(s_zero, gc, hc, rhc, y0c, emit_y=False)
    y_f, y_b, _ = _rw_state(s_ctx, gl, hl, rhl, y0l, emit_y=True)

    x1, f, idx4, gate4, rank4, counts = _outproj(
        x, y5, y_f, y_b, fl[4], fl[3], rw_ln_w[0], rw_ln_b[0], hsel, hselt, w_out[0].astype(BF16),
        gt1, norm2_g[0], sc2, sh2, router_w[0], router_b[0])

    n_tok = bsz * n_lat
    cnt = counts[0]
    padded = (cnt + MOE_ROWS - 1) // MOE_ROWS * MOE_ROWS
    pend = jnp.cumsum(padded)
    start = (pend - padded).astype(jnp.int32)
    nb = n_tok * TOP_K // MOE_ROWS + ne
    n_slots = nb * MOE_ROWS
    blocks_per_group = MOE_GROUP_ROWS // MOE_ROWS
    n_groups_max = n_slots // MOE_GROUP_ROWS + ne
    groups_e = (padded + MOE_GROUP_ROWS - 1) // MOE_GROUP_ROWS
    groups_end = jnp.cumsum(groups_e)
    gidx = jnp.arange(n_groups_max)
    sb_e = jnp.minimum(jnp.sum(groups_end[None, :] <= gidx[:, None], axis=1), ne - 1)
    local = gidx - (groups_end - groups_e)[sb_e]
    sb_row = start[sb_e] // MOE_ROWS + local * blocks_per_group
    sb_n = jnp.clip(padded[sb_e] // MOE_ROWS - local * blocks_per_group, 0, blocks_per_group)
    n_sb = groups_end[-1].astype(jnp.int32).reshape(1)
    slot_flat = (start[idx4] + rank4).reshape(-1).astype(jnp.int32)

    blk_end = (jnp.arange(nb) + 1) * MOE_ROWS
    fill = jnp.logical_or(jnp.any(blk_end[:, None] == pend[None, :], axis=1),
                          blk_end > pend[-1]).astype(jnp.int32)
    xs = _dispatch(slot_flat, fill, f.reshape(-1, LANES), n_tok, n_slots)
    y3 = _moe_experts(sb_e.astype(jnp.int32), sb_row.astype(jnp.int32), sb_n.astype(jnp.int32),
                      n_sb, xs, exp_w_gu[0], exp_b_gu[0], exp_w_dn[0], exp_b_dn[0])
    out = _combine(slot_flat, x1.reshape(n_tok, d), gate4.reshape(n_tok, TOP_K), gt2, final_g, y3,
                   n_lat // ROW_TILE)
    return out.reshape(bsz, n_lat, d)
```

```python
import functools
import math

import jax
import jax.numpy as jnp
from jax import lax
from jax.experimental import pallas as pl
from jax.experimental.pallas import tpu as pltpu

F32 = jnp.float32
BF16 = jnp.bfloat16
HIGHEST = lax.Precision.HIGHEST

LANES = 128
VMEM_LIMIT_BYTES = 56 * 1024 * 1024

NORM_EPS = 1e-5
N_DIR = 2
S5_GROUP = 16
S5_STATE = 64
S5_CHUNK = 8
S5_LANE_GROUPS = LANES // S5_GROUP
S5_SCAN_ROWS = 8
RW_HEAD = 64
RW_CHUNK = 64
RW_CHUNKS_PER_STEP = 2
RW_GN_EPS = 64e-5
GRID_W = 64
TOP_K = 4
SWIGLU_ALPHA = 1.702
SWIGLU_LIMIT = 7.0
ROW_TILE = 256
MOE_ROWS = 256
MOE_GROUP_ROWS = 1536
MOE_HIDDEN_TILE = 256


def _params(*sem):
    return pltpu.CompilerParams(dimension_semantics=sem, vmem_limit_bytes=VMEM_LIMIT_BYTES)


def _mm(a, b, dims=((1,), (0,))):
    return lax.dot_general(a.astype(BF16), b.astype(BF16), (dims, ((), ())),
                           preferred_element_type=F32)


def _split_bf16(a):
    hi = a.astype(BF16)
    return hi, (a - hi.astype(F32)).astype(BF16)


def _mm_sel(a, sel):
    hi, lo = _split_bf16(a)
    m = a.shape[0]
    out = jnp.dot(jnp.concatenate([hi, lo], axis=0), sel.astype(BF16), preferred_element_type=F32)
    return out[:m] + out[m:]


def _sel_mm(sel, a):
    hi, lo = _split_bf16(a)
    n = a.shape[1]
    out = jnp.dot(sel.astype(BF16), jnp.concatenate([hi, lo], axis=1), preferred_element_type=F32)
    return out[:, :n] + out[:, n:]


def _mm_f32(a, b, dims=((1,), (0,))):
    return lax.dot_general(a, b, (dims, ((), ())), precision=HIGHEST,
                           preferred_element_type=F32)


def _adaln_kernel(cond_ref, w_ref, b_ref, o_ref):
    c = cond_ref[...]
    o_ref[...] = _mm_f32(c * jax.nn.sigmoid(c), w_ref[...]) + b_ref[...]


def _adaln(cond8, w, b):
    d, n = w.shape
    tn = 1536
    return pl.pallas_call(
        _adaln_kernel,
        grid=(n // tn,),
        in_specs=[pl.BlockSpec((8, d), lambda j: (0, 0)),
                  pl.BlockSpec((d, tn), lambda j: (0, j)),
                  pl.BlockSpec((1, tn), lambda j: (0, j))],
        out_specs=pl.BlockSpec((8, tn), lambda j: (0, j)),
        out_shape=jax.ShapeDtypeStruct((8, n), F32),
        compiler_params=_params("parallel"),
        name="adaln",
    )(cond8, w, b.reshape(1, n))


def _rms_mod(x, g, sc, sh):
    y = x * lax.rsqrt(jnp.mean(x * x, axis=-1, keepdims=True) + NORM_EPS)
    return (y * g) * (1.0 + sc) + sh


def _inproj_kernel(x_ref, g_ref, sc_ref, sh_ref, wu_ref, wz_ref, u_ref, z_ref):
    h = _rms_mod(x_ref[0], g_ref[...], sc_ref[0], sh_ref[0]).astype(BF16)
    u_ref[0] = jnp.dot(h, wu_ref[...], preferred_element_type=F32)
    z_ref[0] = jnp.dot(h, wz_ref[...], preferred_element_type=F32)


def _inproj(x, g, sc, sh, wu, wz):
    bsz, n, d = x.shape
    nu, nz = wu.shape[1], wz.shape[1]
    tm = ROW_TILE
    const = dict(pipeline_mode=pl.Buffered(1))
    return pl.pallas_call(
        _inproj_kernel,
        grid=(bsz, n // tm),
        in_specs=[pl.BlockSpec((1, tm, d), lambda b, i: (b, i, 0)),
                  pl.BlockSpec((1, d), lambda b, i: (0, 0)),
                  pl.BlockSpec((1, 1, d), lambda b, i: (b, 0, 0)),
                  pl.BlockSpec((1, 1, d), lambda b, i: (b, 0, 0)),
                  pl.BlockSpec((d, nu), lambda b, i: (0, 0), **const),
                  pl.BlockSpec((d, nz), lambda b, i: (0, 0), **const)],
        out_specs=[pl.BlockSpec((1, tm, nu), lambda b, i: (b, i, 0)),
                   pl.BlockSpec((1, tm, nz), lambda b, i: (b, i, 0))],
        out_shape=[jax.ShapeDtypeStruct((bsz, n, nu), F32),
                   jax.ShapeDtypeStruct((bsz, n, nz), F32)],
        compiler_params=_params("parallel", "parallel"),
        name="inproj",
    )(x, g.reshape(1, d), sc, sh, wu, wz)


def _s5_constants(a_re, a_im, log_dt, b_re, b_im, c_re, c_im, d_skip, glu_w, glu_b):
    t = S5_CHUNK
    g, p = a_re.shape[1], a_re.shape[2]
    h = S5_GROUP
    gl = S5_LANE_GROUPS
    no = g // gl
    n = jnp.arange(t + 1, dtype=F32)[None, None, :, None]
    ar, ai = a_re.astype(F32)[:, :, None, :], a_im.astype(F32)[:, :, None, :]
    dt = jnp.exp(log_dt.astype(F32))[:, :, None, None]
    mag = jnp.exp(n * (ar * dt))
    pr, pi = mag * jnp.cos(n * (ai * dt)), mag * jnp.sin(n * (ai * dt))
    lr, li = pr[:, :, 1], pi[:, :, 1]
    ar, ai = ar[:, :, 0], ai[:, :, 0]
    den = ar * ar + ai * ai
    cf_re = (((lr - 1.0) * ar + li * ai) / den)[:, :, None, :]
    cf_im = ((li * ar - (lr - 1.0) * ai) / den)[:, :, None, :]
    br = jnp.swapaxes(b_re.astype(F32), 2, 3)
    bi = jnp.swapaxes(b_im.astype(F32), 2, 3)
    bb_re = cf_re * br - cf_im * bi
    bb_im = cf_re * bi + cf_im * br
    cr = jnp.swapaxes(c_re.astype(F32), 2, 3)[:, :, :, None, :]
    ci = jnp.swapaxes(c_im.astype(F32), 2, 3)[:, :, :, None, :]
    prt = jnp.swapaxes(pr, 2, 3)[..., None]
    pit = jnp.swapaxes(pi, 2, 3)[..., None]
    cl_re = cr * prt - ci * pit
    cl_im = cr * pit + ci * prt
    dims = (((2,), (1,)), ((0,), (0,)))
    flat = lambda a: a.reshape((N_DIR * g,) + a.shape[2:])
    lag = (lax.dot_general(flat(bb_re), flat(cl_re[:, :, :, :t]).reshape(N_DIR * g, p, t * h),
                           dims, precision=HIGHEST)
           - lax.dot_general(flat(bb_im), flat(cl_im[:, :, :, :t]).reshape(N_DIR * g, p, t * h),
                             dims, precision=HIGHEST)).reshape(N_DIR, g, h, t, h)
    s_idx = jnp.arange(t)[:, None]
    t_idx = jnp.arange(t)[None, :]
    lag_n = jnp.arange(t)[:, None, None]
    shift = jnp.stack([(t_idx - s_idx)[None] == lag_n, (s_idx - t_idx)[None] == lag_n]).astype(F32)
    kg = jnp.einsum('dnst,dginj->gsitj', shift, lag, precision=HIGHEST)
    k_rows = jnp.transpose(kg.reshape(no, gl, t, h, t * h), (0, 2, 1, 3, 4)).reshape(
        no, t * gl * h, t * h)
    ps_re = jnp.stack([pr[0, :, :t][:, ::-1], pr[1, :, :t]])[:, :, :, None, :]
    ps_im = jnp.stack([pi[0, :, :t][:, ::-1], pi[1, :, :t]])[:, :, :, None, :]
    f_re = ps_re * bb_re[:, :, None] - ps_im * bb_im[:, :, None]
    f_im = ps_re * bb_im[:, :, None] + ps_im * bb_re[:, :, None]
    fg = jnp.transpose(jnp.stack([f_re, f_im]), (2, 3, 4, 0, 1, 5))
    f_rows = jnp.transpose(fg.reshape(no, gl, t, h, 4 * p), (0, 2, 1, 3, 4)).reshape(
        no, t * gl * h, 4 * p)
    pick = lambda c: jnp.stack([c[0, :, :, 1:], c[1, :, :, 1:][:, :, ::-1]])
    eg = jnp.stack([pick(cl_re), -pick(cl_im)])
    eg = jnp.transpose(eg.reshape(2 * N_DIR, no, gl, p, t * h), (1, 0, 2, 3, 4))
    e_rows = eg.reshape(no, 4 * gl * p, t * h)
    lam_rows = [pr[0, :, t], pi[0, :, t], pr[1, :, t], pi[1, :, t]]
    fmat, emat, kmat = _s5_expand(f_rows, e_rows, k_rows)
    lam = jnp.stack(lam_rows, axis=1)
    lam = jnp.transpose(lam.reshape(no, gl, 4, p), (0, 2, 1, 3)).reshape(no, 4, gl * p)
    gmat = jnp.einsum('ab,oahk->oahbk', jnp.eye(gl, dtype=F32),
                      glu_w.astype(F32).reshape(no, gl, h, h)).reshape(no, gl * h, gl * h)
    dvec = d_skip.astype(F32).reshape(no, 1, gl * h)
    bvec = glu_b.astype(F32).reshape(no, 1, gl * h)
    return fmat, emat, kmat, lam, dvec, bvec, gmat.astype(BF16)


def _s5_expand_kernel(f_ref, e_ref, k_ref, rep_s_ref, rep_t_ref, fo_ref, eo_ref, ko_ref):
    def expand(x, rep, row_div, col_div):
        y = jnp.dot(x.astype(BF16), rep, preferred_element_type=F32)
        row = lax.broadcasted_iota(jnp.int32, y.shape, 0)
        col = lax.broadcasted_iota(jnp.int32, y.shape, 1)
        same = (row // row_div) % S5_LANE_GROUPS == (col // col_div) % S5_LANE_GROUPS
        return jnp.where(same, y, 0.0).astype(BF16)

    fo_ref[0] = expand(f_ref[0], rep_s_ref[...], S5_GROUP, S5_STATE)
    eo_ref[0] = expand(e_ref[0], rep_t_ref[...], S5_STATE, S5_GROUP)
    ko_ref[0] = expand(k_ref[0], rep_t_ref[...], S5_GROUP, S5_GROUP)


def _s5_expand(f_rows, e_rows, k_rows):
    no, rows_f, cols_f = f_rows.shape
    _, rows_e, cols_e = e_rows.shape
    gl = S5_LANE_GROUPS
    c = jnp.arange(cols_f * gl)
    rep_s = (jnp.arange(cols_f)[:, None]
             == (c // (gl * S5_STATE)) * S5_STATE + c % S5_STATE).astype(BF16)
    c = jnp.arange(cols_e * gl)
    rep_t = (jnp.arange(cols_e)[:, None]
             == (c // (gl * S5_GROUP)) * S5_GROUP + c % S5_GROUP).astype(BF16)
    blk = lambda r, cc: pl.BlockSpec((1, r, cc), lambda o: (o, 0, 0))
    full = lambda a: pl.BlockSpec(a.shape, lambda o: (0, 0))
    return pl.pallas_call(
        _s5_expand_kernel,
        grid=(no,),
        in_specs=[blk(rows_f, cols_f), blk(rows_e, cols_e), blk(rows_f, cols_e),
                  full(rep_s), full(rep_t)],
        out_specs=[blk(rows_f, cols_f * gl), blk(rows_e, cols_e * gl), blk(rows_f, cols_e * gl)],
        out_shape=[jax.ShapeDtypeStruct((no, rows_f, cols_f * gl), BF16),
                   jax.ShapeDtypeStruct((no, rows_e, cols_e * gl), BF16),
                   jax.ShapeDtypeStruct((no, rows_f, cols_e * gl), BF16)],
        compiler_params=_params("parallel"),
        name="s5_expand",
    )(f_rows, e_rows, k_rows, rep_s, rep_t)


def _s5_chunk_rows(u_ref, b, start, n_chunks):
    t = S5_CHUNK
    return jnp.concatenate(
        [u_ref[b, pl.ds(start + s, n_chunks, stride=t), :] for s in range(t)], axis=-1)


def _s5_in_kernel(u_ref, f_ref, z_ref):
    bsz, n_tok, _ = u_ref.shape
    for b in range(bsz):
        z_ref[b] = _mm(_s5_chunk_rows(u_ref, b, 0, n_tok // S5_CHUNK), f_ref[0])


def _s5_scan_kernel(n_ctx_chunks, z_ref, lam_ref, hs_ref):
    n_chunks = z_ref.shape[1]
    q = z_ref.shape[2] // 4
    rows = S5_SCAN_ROWS
    n_blocks, ctx_blocks = n_chunks // rows, n_ctx_chunks // rows
    lam = lam_ref[0]
    lf_re, lf_im, lb_re, lb_im = lam[0:1], lam[1:2], lam[2:3], lam[3:4]

    def block(k, carry):
        f_re, f_im, b_re, b_im = carry
        kb = jnp.where(k < ctx_blocks, ctx_blocks - 1 - k, n_blocks - 1 + ctx_blocks - k)
        rf = pl.multiple_of(k * rows, rows)
        rb = pl.multiple_of(kb * rows, rows)
        zf = z_ref[0, pl.ds(rf, rows), :]
        zb = z_ref[0, pl.ds(rb, rows), :]
        ent = [[], [], [], []]
        for j in range(rows):
            jb = rows - 1 - j
            for lst, val in zip(ent, (f_re, f_im, b_re, b_im)):
                lst.append(val)
            f_re, f_im = (lf_re * f_re - lf_im * f_im + zf[j:j + 1, 0:q],
                          lf_re * f_im + lf_im * f_re + zf[j:j + 1, 2 * q:3 * q])
            b_re, b_im = (lb_re * b_re - lb_im * b_im + zb[jb:jb + 1, q:2 * q],
                          lb_re * b_im + lb_im * b_re + zb[jb:jb + 1, 3 * q:4 * q])
        hs_ref[0, pl.ds(rf, rows), 0:q] = jnp.concatenate(ent[0], axis=0)
        hs_ref[0, pl.ds(rf, rows), 2 * q:3 * q] = jnp.concatenate(ent[1], axis=0)
        hs_ref[0, pl.ds(rb, rows), q:2 * q] = jnp.concatenate(ent[2][::-1], axis=0)
        hs_ref[0, pl.ds(rb, rows), 3 * q:4 * q] = jnp.concatenate(ent[3][::-1], axis=0)
        return f_re, f_im, b_re, b_im

    zero = jnp.zeros((1, q), F32)
    lax.fori_loop(0, n_blocks, block, (zero, zero, zero, zero))


def _s5_out_kernel(n_ctx, u_ref, hs_ref, e_ref, k_ref, g_ref, d_ref, b_ref, o_ref, y_scr):
    t = S5_CHUNK
    bsz, n_tok, lanes = u_ref.shape
    n_chunks = (n_tok - n_ctx) // t
    for b in range(bsz):
        x = _s5_chunk_rows(u_ref, b, n_ctx, n_chunks)
        y = _mm(x, k_ref[0]) + _mm(hs_ref[b, n_ctx // t:, :], e_ref[0])
        for s in range(t):
            y_scr[pl.ds(s, n_chunks, stride=t), :] = y[:, s * lanes:(s + 1) * lanes]
        y = jax.nn.gelu(y_scr[...] + d_ref[0] * u_ref[b, n_ctx:, :])
        gate = _mm(y, g_ref[0]) + b_ref[0]
        o_ref[b] = y * jax.nn.sigmoid(gate)


def _s5(u_all, consts, n_ctx):
    fmat, emat, kmat, lam, dvec, bvec, gmat = consts
    bsz, n_tok, width = u_all.shape
    t = S5_CHUNK
    n_chunks = n_tok // t
    no, rows_k, cols_f = fmat.shape
    assert n_ctx % (t * S5_SCAN_ROWS) == 0 and n_tok % (t * S5_SCAN_ROWS) == 0
    tok = pl.BlockSpec((1, n_tok, LANES), lambda o, b: (b, 0, o))
    state = pl.BlockSpec((1, n_chunks, cols_f), lambda o, b: (b, 0, o))
    full = lambda a: pl.BlockSpec((1,) + a.shape[1:], lambda o, b: (o, 0, 0))
    z = pl.pallas_call(
        _s5_in_kernel,
        grid=(no, bsz),
        in_specs=[tok, full(fmat)],
        out_specs=state,
        out_shape=jax.ShapeDtypeStruct((bsz, n_chunks, no * cols_f), F32),
        compiler_params=_params("parallel", "parallel"),
        name="s5_in",
    )(u_all, fmat)
    hs = pl.pallas_call(
        functools.partial(_s5_scan_kernel, n_ctx // t),
        grid=(no, bsz),
        in_specs=[state, full(lam)],
        out_specs=state,
        out_shape=jax.ShapeDtypeStruct((bsz, n_chunks, no * cols_f), F32),
        compiler_params=_params("parallel", "parallel"),
        name="s5_scan",
    )(z, lam)
    n_lat = n_tok - n_ctx
    return pl.pallas_call(
        functools.partial(_s5_out_kernel, n_ctx),
        grid=(no, bsz),
        in_specs=[tok, state, full(emat), full(kmat), full(gmat), full(dvec), full(bvec)],
        out_specs=pl.BlockSpec((1, n_lat, LANES), lambda o, b: (b, 0, o)),
        out_shape=jax.ShapeDtypeStruct((bsz, n_lat, width), F32),
        scratch_shapes=[pltpu.VMEM((n_lat, LANES), F32)],
        compiler_params=_params("parallel", "parallel"),
        name="s5_out",
    )(u_all, hs, emat, kmat, gmat, dvec, bvec)


def _rw_feat_kernel(width, has_vert, n_tiles, z_ref, zp_ref, zn_ref, mu_ref, w0_ref, w2_ref,
                    a0_ref, a2_ref, g2_ref, kk_ref, ka_ref, rk_ref, hsel_ref, hselt_ref,
                    r_ref, v_ref, kh_ref, g_ref, bv_ref, lw_ref, q_ref, kt_ref):
    z = z_ref[0]
    t0, cz = z.shape
    rw = r_ref.shape[-1]
    mu = mu_ref[...]
    tok = lax.broadcasted_iota(jnp.int32, (t0, 1), 0)
    col = tok % width
    left = pltpu.roll(z, 1, 0)
    right = pltpu.roll(z, t0 - 1, 0)
    out = z + jnp.where(col != 0, mu[0:1] * (left - z), 0.0)
    out = out + jnp.where(col != width - 1, mu[1:2] * (right - z), 0.0)
    if has_vert:
        i = pl.program_id(1)
        up = jnp.concatenate([zp_ref[0], z[:t0 - width]], axis=0)
        down = jnp.concatenate([z[width:], zn_ref[0]], axis=0)
        up_ok = jnp.logical_or(i > 0, tok >= width)
        down_ok = jnp.logical_or(i < n_tiles - 1, tok < t0 - width)
        out = out + jnp.where(up_ok, mu[2:3] * (up - z), 0.0)
        out = out + jnp.where(down_ok, mu[3:4] * (down - z), 0.0)
    r = out[:, 0:rw]
    k = out[:, rw:2 * rw]
    v = out[:, 2 * rw:3 * rw]
    o = 3 * rw
    lora = w2_ref.shape[0]
    xw = out[:, o:o + lora]
    xa = out[:, o + lora:o + 2 * lora]
    xg = out[:, o + 2 * lora:]
    dec = w0_ref[...] + _mm(jnp.tanh(xw), w2_ref[...])
    lw = -math.exp(-0.5) * jax.nn.sigmoid(dec)
    a = jax.nn.sigmoid(a0_ref[...] + _mm(xa, a2_ref[...]))
    g = _mm(jax.nn.sigmoid(xg), g2_ref[...])
    kk = k * kk_ref[...]
    ssq = _mm_sel(kk * kk, hsel_ref[...])
    inv = 1.0 / jnp.maximum(jnp.sqrt(ssq), 1e-12)
    kh = kk * _mm_sel(inv, hselt_ref[...])
    ka = ka_ref[...]
    kt_sum = jnp.zeros_like(k)
    for d in range(N_DIR):
        a_d = a[:, d * rw:(d + 1) * rw]
        kt_d = k * (1.0 + (a_d - 1.0) * ka)
        kt_sum = kt_sum + kt_d
        lw_ref[d, 0] = lw[:, d * rw:(d + 1) * rw]
        q_ref[d, 0] = a_d * kh
        kt_ref[d, 0] = kt_d
    bonus = _mm_sel(_mm_sel(r * kt_sum * rk_ref[...], hsel_ref[...]), hselt_ref[...])
    r_ref[0] = r
    v_ref[0] = v
    kh_ref[0] = kh
    g_ref[0] = g
    bv_ref[0] = bonus * v


def _rw_features(z, width, has_vert, mu, w0, w2blk, a0, a2blk, g2, k_k, k_a, r_k, hsel, hselt):
    bsz, n, cz = z.shape
    rw = k_k.shape[-1]
    t0 = ROW_TILE
    n_tiles = n // t0
    per = t0 // GRID_W
    nblk = n // GRID_W
    full = lambda a: pl.BlockSpec(a.shape, lambda b, i: (0,) * a.ndim)
    tok = pl.BlockSpec((1, t0, rw), lambda b, i: (b, i, 0))
    dtok = pl.BlockSpec((N_DIR, 1, t0, rw), lambda b, i: (0, b, i, 0))
    consts = (mu, w0, w2blk, a0, a2blk, g2, k_k, k_a, r_k, hsel, hselt)
    return pl.pallas_call(
        functools.partial(_rw_feat_kernel, width, has_vert, n_tiles),
        grid=(bsz, n_tiles),
        in_specs=[pl.BlockSpec((1, t0, cz), lambda b, i: (b, i, 0)),
                  pl.BlockSpec((1, GRID_W, cz), lambda b, i: (b, jnp.maximum(i * per - 1, 0), 0)),
                  pl.BlockSpec((1, GRID_W, cz),
                               lambda b, i: (b, jnp.minimum(i * per + per, nblk - 1), 0))]
        + [full(a) for a in consts],
        out_specs=[tok] * 5 + [dtok] * 3,
        out_shape=[jax.ShapeDtypeStruct((bsz, n, rw), F32)] * 5
        + [jax.ShapeDtypeStruct((N_DIR, bsz, n, rw), F32)] * 3,
        compiler_params=_params("parallel", "parallel"),
        name="rw_features",
    )(z, z, z, *consts)


def _rw_chunk_kernel(r_ref, v_ref, kh_ref, lw_ref, q_ref, kt_ref, g_ref, h_ref, rh_ref, y0_ref):
    rev = pl.program_id(0) == 1
    n = RW_CHUNK
    hd = RW_HEAD
    n_sub = lw_ref.shape[2] // n
    row = lax.broadcasted_iota(jnp.int32, (n, n), 0)
    col = lax.broadcasted_iota(jnp.int32, (n, n), 1)
    ahead = (row - col) * jnp.where(rev, -1, 1)
    incl = (ahead >= 0).astype(F32)
    strict = (ahead > 0).astype(F32)
    eye = (row == col).astype(F32)
    same_block = [(jnp.right_shift(row, s) == jnp.right_shift(col, s)).astype(F32)
                  for s in range(3, n.bit_length())]
    n_heads = lw_ref.shape[-1] // hd
    pt, rt, v, qt, ktt, qh, kth, gam, where = [], [], [], [], [], [], [], [], []
    for c in range(n_sub):
        tok = slice(c * n, (c + 1) * n)
        lw = lw_ref[0, 0, tok, :]
        b_incl = _sel_mm(incl, lw)
        btot = jnp.sum(lw, axis=0, keepdims=True)
        e_neg = jnp.exp(-b_incl)
        e_rem = jnp.exp(btot - b_incl)
        q_c, kt_c = q_ref[0, 0, tok, :], kt_ref[0, 0, tok, :]
        pt_all = kh_ref[0, tok, :] * jnp.exp(b_incl - lw)
        rt_all = r_ref[0, tok, :] * jnp.exp(b_incl)
        qt_all, ktt_all = q_c * e_neg, kt_c * e_neg
        qh_all, kth_all = q_c * e_rem, kt_c * e_rem
        gam_all = jnp.exp(btot)
        v_all = v_ref[0, tok, :]
        for h in range(n_heads):
            sl = slice(h * hd, (h + 1) * hd)
            for lst, arr in zip((pt, rt, v, qt, ktt, qh, kth, gam),
                                (pt_all, rt_all, v_all, qt_all, ktt_all, qh_all, kth_all, gam_all)):
                lst.append(arr[:, sl])
            where.append((c, tok, sl))
    heads = range(len(where))
    a4 = [_mm(jnp.concatenate([pt[h], rt[h]], 0), jnp.concatenate([qt[h], ktt[h]], 0),
              ((1,), (1,))) for h in heads]
    nmat = [strict * a4[h][:n, :n] for h in heads]
    akv = [_mm(strict * a4[h][:n, n:], v[h]) for h in heads]
    nd = [same_block[0] * nmat[h] for h in heads]
    x = [_mm(nd[h], nd[h]) for h in heads]
    m = [eye - nd[h] for h in heads]
    m = [m[h] + _mm(m[h], x[h]) for h in heads]
    x = [_mm(x[h], x[h]) for h in heads]
    m = [m[h] + _mm(m[h], x[h]) for h in heads]
    for lvl in range(1, len(same_block)):
        ring = same_block[lvl] - same_block[lvl - 1]
        t = [_mm(m[h], ring * nmat[h]) for h in heads]
        m = [m[h] - _mm(t[h], m[h]) for h in heads]
    wu = [_mm(m[h], jnp.concatenate([pt[h], akv[h]], 1)) for h in heads]
    gh = [_mm(wu[h], qh[h], ((0,), (0,))) for h in heads]
    vk = [_mm(v[h], kth[h], ((0,), (0,))) for h in heads]
    lwu = [_mm(incl * a4[h][n:, :n], wu[h]) for h in heads]
    lv = [_mm(incl * a4[h][n:, n:], v[h]) for h in heads]
    for h in heads:
        c, tok, sl = where[h]
        g_ref[0, 0, c, :, sl] = eye * gam[h] - gh[h][:hd]
        h_ref[0, 0, c, :, sl] = vk[h] - gh[h][hd:]
        rh_ref[0, 0, tok, sl] = rt[h] - lwu[h][:, :hd]
        y0_ref[0, 0, tok, sl] = lv[h] - lwu[h][:, hd:]


def _rw_chunks(r, v, kh, lw, q, kt):
    bsz, n, rw = r.shape
    nc = n // RW_CHUNK
    per = RW_CHUNKS_PER_STEP
    tok = pl.BlockSpec((1, per * RW_CHUNK, rw), lambda d, b, c: (b, c, 0))
    dtok = pl.BlockSpec((1, 1, per * RW_CHUNK, rw), lambda d, b, c: (d, b, c, 0))
    mat = pl.BlockSpec((1, 1, per, RW_HEAD, rw), lambda d, b, c: (d, b, c, 0, 0))
    return pl.pallas_call(
        _rw_chunk_kernel,
        grid=(N_DIR, bsz, nc // per),
        in_specs=[tok, tok, tok, dtok, dtok, dtok],
        out_specs=[mat, mat, dtok, dtok],
        out_shape=[jax.ShapeDtypeStruct((N_DIR, bsz, nc, RW_HEAD, rw), F32)] * 2
        + [jax.ShapeDtypeStruct((N_DIR, bsz, n, rw), F32)] * 2,
        compiler_params=_params("parallel", "parallel", "parallel"),
        name="rw_chunks",
    )(r, v, kh, lw, q, kt)


def _rw_state_kernel(emit_y, s0_ref, gf_ref, gb_ref, hf_ref, hb_ref, rhf_ref, rhb_ref, y0f_ref,
                     y0b_ref, *rest):
    if emit_y:
        yf_ref, yb_ref, sfin_ref, s_scr = rest
    else:
        sfin_ref, s_scr = rest
    c = pl.program_id(0)
    hd = RW_HEAD
    n_dir, bsz = s_scr.shape[0], s_scr.shape[1]

    @pl.when(c == 0)
    def _():
        s_scr[...] = s0_ref[...]

    sls = [slice(h * hd, (h + 1) * hd) for h in range(s_scr.shape[-1] // hd)]
    g_refs, h_refs = (gf_ref, gb_ref), (hf_ref, hb_ref)
    rh_refs, y0_refs = (rhf_ref, rhb_ref), (y0f_ref, y0b_ref)
    chains = [(d, b) for d in range(n_dir) for b in range(bsz)]
    s_bf = {k: s_scr[k[0], k[1]].astype(BF16) for k in chains}
    g_bf = {k: g_refs[k[0]][0, k[1], 0].astype(BF16) for k in chains}
    s_new = {k: [_mm(s_bf[k][:, sl], g_bf[k][:, sl]) for sl in sls] for k in chains}
    if emit_y:
        y_refs = (yf_ref, yb_ref)
        rh_bf = {k: rh_refs[k[0]][0, k[1]].astype(BF16) for k in chains}
        y = {k: [_mm(rh_bf[k][:, sl], s_bf[k][:, sl], ((1,), (1,))) for sl in sls] for k in chains}
        for d, b in chains:
            y_refs[d][b] = jnp.concatenate(y[d, b], axis=-1) + y0_refs[d][0, b]
    for d, b in chains:
        s_scr[d, b] = jnp.concatenate(s_new[d, b], axis=-1) + h_refs[d][0, b, 0]

    @pl.when(c == pl.num_programs(0) - 1)
    def _():
        sfin_ref[...] = s_scr[...]


def _rw_state(s0, gmat, hmat, rh, y0, emit_y):
    n_dir, bsz, nc, hd, rw = gmat.shape
    n = rh.shape[2]
    chunk = lambda d, c: c + d * (nc - 1 - 2 * c)
    mat = lambda d: pl.BlockSpec((1, bsz, 1, hd, rw), lambda c: (d, 0, chunk(d, c), 0, 0))
    dtok = lambda d: pl.BlockSpec((1, bsz, RW_CHUNK, rw), lambda c: (d, 0, chunk(d, c), 0))
    st = pl.BlockSpec((n_dir, bsz, hd, rw), lambda c: (0, 0, 0, 0))
    out_specs = [st]
    out_shape = [jax.ShapeDtypeStruct((n_dir, bsz, hd, rw), F32)]
    if emit_y:
        out_specs = [pl.BlockSpec((bsz, RW_CHUNK, rw), lambda c, d=d: (0, chunk(d, c), 0))
                     for d in range(n_dir)] + out_specs
        out_shape = [jax.ShapeDtypeStruct((bsz, n, rw), F32)] * n_dir + out_shape
    return pl.pallas_call(
        functools.partial(_rw_state_kernel, emit_y),
        grid=(nc,),
        in_specs=[st, mat(0), mat(1), mat(0), mat(1), dtok(0), dtok(1), dtok(0), dtok(1)],
        out_specs=out_specs,
        out_shape=out_shape,
        scratch_shapes=[pltpu.VMEM((n_dir, bsz, hd, rw), F32)],
        compiler_params=_params("arbitrary"),
        name="rw_state_y" if emit_y else "rw_state",
    )(s0, gmat, gmat, hmat, hmat, rh, rh, y0, y0)


def _outproj_kernel(x_ref, y5_ref, yf_ref, yb_ref, bv_ref, g_ref, lnw_ref, lnb_ref, hsel_ref,
                    hselt_ref, wo_ref, gt1_ref, g2_ref, sc2_ref, sh2_ref, rw_ref, rb_ref,
                    x1_ref, f_ref, idx_ref, gate_ref, rank_ref, cnt_ref, carry):
    first = jnp.logical_and(pl.program_id(0) == 0, pl.program_id(1) == 0)

    @pl.when(first)
    def _():
        carry[...] = jnp.zeros_like(carry)

    inv_hd = 1.0 / RW_HEAD
    y = yf_ref[0] + yb_ref[0]
    mean = _mm_sel(_mm_sel(y, hsel_ref[...]), hselt_ref[...]) * inv_hd
    yc = y - mean
    var = _mm_sel(_mm_sel(yc * yc, hsel_ref[...]), hselt_ref[...]) * inv_hd
    yn = yc * lax.rsqrt(var + RW_GN_EPS) * lnw_ref[...] + lnb_ref[...]
    yr = (yn + bv_ref[0]) * g_ref[0]
    mix = _mm(jnp.concatenate([y5_ref[0], yr], axis=-1), wo_ref[...])
    x1 = x_ref[0] + gt1_ref[0] * mix
    x1_ref[0] = x1
    f = _rms_mod(x1, g2_ref[...], sc2_ref[0], sh2_ref[0])
    n_lines = f.shape[1] // LANES
    for s in range(n_lines):
        f_ref[0, pl.ds(s, f.shape[0], stride=n_lines), :] = f[:, s * LANES:(s + 1) * LANES]

    logits = _mm_f32(f, rw_ref[...]) + rb_ref[...]
    tm, ne = logits.shape
    eid = lax.broadcasted_iota(jnp.int32, (tm, ne), 1)
    work = logits
    sel = jnp.zeros((tm, ne), F32)
    idx_cols, val_cols = [], []
    for _ in range(TOP_K):
        top = jnp.max(work, axis=-1, keepdims=True)
        pick = jnp.min(jnp.where(work == top, eid, ne), axis=-1, keepdims=True)
        hit = eid == pick
        sel = jnp.where(hit, 1.0, sel)
        work = jnp.where(hit, -jnp.inf, work)
        idx_cols.append(pick)
        val_cols.append(top)
    exps = [jnp.exp(vk - val_cols[0]) for vk in val_cols]
    denom = exps[0] + exps[1] + exps[2] + exps[3]
    row = lax.broadcasted_iota(jnp.int32, (tm, tm), 0)
    colm = lax.broadcasted_iota(jnp.int32, (tm, tm), 1)
    before = _mm((colm < row).astype(F32), sel) + carry[0:1, :]
    rank_cols = [jnp.sum(jnp.where(eid == ic, before, 0.0), axis=-1, keepdims=True)
                 for ic in idx_cols]
    idx_ref[0] = jnp.concatenate(idx_cols, axis=-1)
    gate_ref[0] = jnp.concatenate([e / denom for e in exps], axis=-1)
    rank_ref[0] = jnp.concatenate(rank_cols, axis=-1).astype(jnp.int32)
    total = carry[0:1, :] + jnp.sum(sel, axis=0, keepdims=True)
    carry[...] = jnp.broadcast_to(total, carry.shape)
    cnt_ref[...] = jnp.broadcast_to(total, cnt_ref.shape).astype(jnp.int32)


def _outproj(x, y5, yf, yb, bv, g, ln_w, ln_b, hsel, hselt, wo, gt1, g2, sc2, sh2, router_w,
             router_b):
    bsz, n, d = x.shape
    rw = y5.shape[-1]
    ne = router_w.shape[-1]
    tm = ROW_TILE
    row = lambda w: pl.BlockSpec((1, tm, w), lambda b, i: (b, i, 0))
    vec = lambda w: pl.BlockSpec((1, w), lambda b, i: (0, 0))
    bvec = pl.BlockSpec((1, 1, d), lambda b, i: (b, 0, 0))
    full = lambda a: pl.BlockSpec(a.shape, lambda b, i: (0,) * a.ndim)
    return pl.pallas_call(
        _outproj_kernel,
        grid=(bsz, n // tm),
        in_specs=[row(d), row(rw), row(rw), row(rw),
                  row(rw), row(rw), vec(rw), vec(rw), full(hsel), full(hselt),
                  pl.BlockSpec(wo.shape, lambda b, i: (0, 0), pipeline_mode=pl.Buffered(1)),
                  bvec, vec(d), bvec, bvec, full(router_w), vec(ne)],
        out_specs=[row(d), pl.BlockSpec((1, tm * (d // LANES), LANES), lambda b, i: (b, i, 0)),
                   row(TOP_K), row(TOP_K), row(TOP_K),
                   pl.BlockSpec((8, ne), lambda b, i: (0, 0))],
        out_shape=[jax.ShapeDtypeStruct((bsz, n, d), F32),
                   jax.ShapeDtypeStruct((bsz, n * (d // LANES), LANES), F32),
                   jax.ShapeDtypeStruct((bsz, n, TOP_K), jnp.int32),
                   jax.ShapeDtypeStruct((bsz, n, TOP_K), F32),
                   jax.ShapeDtypeStruct((bsz, n, TOP_K), jnp.int32),
                   jax.ShapeDtypeStruct((8, ne), jnp.int32)],
        scratch_shapes=[pltpu.VMEM((8, ne), F32)],
        compiler_params=_params("arbitrary", "arbitrary"),
        name="outproj_router",
    )(x, y5, yf, yb, bv, g, ln_w.reshape(1, rw), ln_b.reshape(1, rw), hsel, hselt, wo, gt1,
      g2.reshape(1, d), sc2, sh2, router_w, router_b.reshape(1, ne))


def _dispatch_kernel(tile_rows, line_ref, fill_ref, f_ref, xs_ref, zeros, sem, sem_fill):
    tr = tile_rows
    tm = f_ref.shape[0] // tr
    blk = zeros.shape[0]

    @pl.when(pl.program_id(0) == 0)
    def _():
        zeros[...] = jnp.zeros_like(zeros)

        def fill(b):
            dst = xs_ref.at[pl.ds(pl.multiple_of(b * blk, blk), blk)]
            return pltpu.make_async_copy(zeros, dst, sem_fill)

        def start(b, carry):
            @pl.when(fill_ref[b] == 1)
            def _():
                fill(b).start()
            return carry

        def wait(b, carry):
            @pl.when(fill_ref[b] == 1)
            def _():
                fill(b).wait()
            return carry

        lax.fori_loop(0, fill_ref.shape[0], start, 0)
        lax.fori_loop(0, fill_ref.shape[0], wait, 0)

    def issue(t, _):
        src = f_ref.at[pl.ds(pl.multiple_of(t * tr, tr), tr)]
        for k in range(TOP_K):
            dst = pl.multiple_of(line_ref[t * TOP_K + k], tr)
            pltpu.make_async_copy(src, xs_ref.at[pl.ds(dst, tr)], sem).start(priority=k % 2)
        return 0

    lax.fori_loop(0, tm, issue, 0)
    for _ in range(TOP_K):
        pltpu.make_async_copy(f_ref, xs_ref.at[pl.ds(0, tm * tr)], sem).wait()


def _dispatch(slot_flat, fill, f_lines, n_tok, n_slots):
    lanes = f_lines.shape[1]
    tr = f_lines.shape[0] // n_tok
    tm = ROW_TILE
    smem_tok = pl.BlockSpec((tm * TOP_K,), lambda i: (i,), memory_space=pltpu.SMEM)
    smem_all = lambda a: pl.BlockSpec(a.shape, lambda i: (0,), memory_space=pltpu.SMEM)
    return pl.pallas_call(
        functools.partial(_dispatch_kernel, tr),
        grid=(n_tok // tm,),
        in_specs=[smem_tok, smem_all(fill), pl.BlockSpec((tm * tr, lanes), lambda i: (i, 0))],
        out_specs=pl.BlockSpec(memory_space=pl.ANY),
        out_shape=jax.ShapeDtypeStruct((n_slots * tr, lanes), f_lines.dtype),
        scratch_shapes=[pltpu.VMEM((MOE_ROWS * tr, lanes), f_lines.dtype),
                        pltpu.SemaphoreType.DMA(()), pltpu.SemaphoreType.DMA(())],
        compiler_params=_params("arbitrary"),
        name="moe_dispatch",
    )(slot_flat * tr, fill, f_lines)


def _moe_kernel(sbe_ref, sbrow_ref, sbn_ref, nsb_ref, xs_ref, wg_ref, wl_ref, wd_ref, bg_ref,
                bl_ref, bd_ref, y_ref, x_stage, x_bf, acc, y_stage, sem_in, sem_out):
    sb, j = pl.program_id(0), pl.program_id(1)
    nj = pl.num_programs(1)
    rb = MOE_ROWS
    n_tiles = y_stage.shape[1] // rb
    valid = sb < nsb_ref[0]

    def copy_in(row, slot):
        src = xs_ref.at[pl.ds(pl.multiple_of(row * n_tiles, rb * n_tiles), rb * n_tiles)]
        return pltpu.make_async_copy(src, x_stage.at[slot], sem_in.at[slot])

    def copy_out(row, slot):
        dst = y_ref.at[pl.ds(pl.multiple_of(row * n_tiles, rb * n_tiles), rb * n_tiles)]
        return pltpu.make_async_copy(y_stage.at[slot], dst, sem_out.at[slot])

    @pl.when(valid)
    def _():
        n_blk = sbn_ref[sb]
        row0 = sbrow_ref[sb] * rb

        @pl.when(j == 0)
        def _():
            copy_in(row0, 0).start()

            def load(k, carry):
                slot = k % 2

                @pl.when(k + 1 < n_blk)
                def _():
                    copy_in(row0 + (k + 1) * rb, 1 - slot).start()

                copy_in(row0 + k * rb, slot).wait()
                r = pl.multiple_of(k * rb, rb)
                for s in range(n_tiles):
                    x_bf[pl.ds(r, rb), s * LANES:(s + 1) * LANES] = (
                        x_stage[slot, pl.ds(s, rb, stride=n_tiles), :].astype(BF16))
                return carry

            lax.fori_loop(0, n_blk, load, 0)

            def clear(k, carry):
                acc[pl.ds(pl.multiple_of(k * rb, rb), rb), :] = jnp.broadcast_to(
                    bd_ref[0], (rb, acc.shape[1]))
                return carry

            lax.fori_loop(0, n_blk, clear, 0)

        def accumulate(r):
            x = x_bf[r, :]
            glu = jnp.dot(x, wg_ref[0].astype(BF16), preferred_element_type=F32) + bg_ref[0]
            lin = jnp.dot(x, wl_ref[0].astype(BF16), preferred_element_type=F32) + bl_ref[0]
            glu = jnp.minimum(glu, SWIGLU_LIMIT)
            lin = jnp.clip(lin, -SWIGLU_LIMIT, SWIGLU_LIMIT)
            act = ((lin + 1.0) * glu * jax.nn.sigmoid(SWIGLU_ALPHA * glu)).astype(BF16)
            acc[r, :] += jnp.dot(act, wd_ref[0].astype(BF16), preferred_element_type=F32)

        def row_pair(kk, carry):
            accumulate(pl.ds(pl.multiple_of(kk * 2 * rb, 2 * rb), 2 * rb))
            return carry

        lax.fori_loop(0, n_blk // 2, row_pair, 0)

        @pl.when(n_blk % 2 == 1)
        def _():
            accumulate(pl.ds(pl.multiple_of((n_blk - 1) * rb, rb), rb))

        @pl.when(j == nj - 1)
        def _():
            def store(k, carry):
                slot = k % 2

                @pl.when(k >= 2)
                def _():
                    copy_out(row0, slot).wait()

                r = pl.multiple_of(k * rb, rb)
                for s in range(n_tiles):
                    y_stage[slot, pl.ds(s, rb, stride=n_tiles), :] = (
                        acc[pl.ds(r, rb), s * LANES:(s + 1) * LANES])
                copy_out(row0 + k * rb, slot).start()
                return carry

            lax.fori_loop(0, n_blk, store, 0)

            @pl.when(n_blk >= 2)
            def _():
                copy_out(row0, n_blk % 2).wait()

            copy_out(row0, (n_blk - 1) % 2).wait()

    @pl.when(jnp.logical_not(valid))
    def _():
        q = (sb - nsb_ref[0]) * nj + j
        last = jnp.maximum(nsb_ref[0] - 1, 0)
        first_unused = sbrow_ref[last] + sbn_ref[last]
        n_unused = y_ref.shape[0] // (rb * n_tiles) - first_unused

        @pl.when(q < n_unused)
        def _():
            y_stage[0] = jnp.zeros(y_stage.shape[1:], y_stage.dtype)
            cp = copy_out((first_unused + q) * rb, 0)
            cp.start()
            cp.wait()


def _moe_experts(sb_e, sb_row, sb_n, n_sb, xs_lines, w_gu, b_gu, w_dn, b_dn):
    ne, d, two_de = w_gu.shape
    lanes = xs_lines.shape[1]
    n_slots = xs_lines.shape[0] // (d // lanes)
    de = two_de // 2
    th = MOE_HIDDEN_TILE
    nj = de // th
    n_groups = sb_e.shape[0]
    grp = lambda sb, nsb: jnp.maximum(jnp.minimum(sb, nsb[0] - 1), 0)
    tile = lambda sb, j, nsb: jnp.where(sb < nsb[0], j, nj - 1)
    grid_spec = pltpu.PrefetchScalarGridSpec(
        num_scalar_prefetch=4,
        grid=(n_groups, nj),
        in_specs=[pl.BlockSpec(memory_space=pl.ANY),
                  pl.BlockSpec((1, d, th), lambda sb, j, e, r, n, nsb:
                               (e[grp(sb, nsb)], 0, tile(sb, j, nsb))),
                  pl.BlockSpec((1, d, th), lambda sb, j, e, r, n, nsb:
                               (e[grp(sb, nsb)], 0, nj + tile(sb, j, nsb))),
                  pl.BlockSpec((1, th, d), lambda sb, j, e, r, n, nsb:
                               (e[grp(sb, nsb)], tile(sb, j, nsb), 0)),
                  pl.BlockSpec((1, 1, th), lambda sb, j, e, r, n, nsb:
                               (e[grp(sb, nsb)], 0, tile(sb, j, nsb))),
                  pl.BlockSpec((1, 1, th), lambda sb, j, e, r, n, nsb:
                               (e[grp(sb, nsb)], 0, nj + tile(sb, j, nsb))),
                  pl.BlockSpec((1, 1, d), lambda sb, j, e, r, n, nsb: (e[grp(sb, nsb)], 0, 0))],
        out_specs=pl.BlockSpec(memory_space=pl.ANY),
        scratch_shapes=[pltpu.VMEM((2, MOE_ROWS * (d // lanes), lanes), F32),
                        pltpu.VMEM((MOE_GROUP_ROWS, d), BF16),
                        pltpu.VMEM((MOE_GROUP_ROWS, d), F32),
                        pltpu.VMEM((2, MOE_ROWS * (d // lanes), lanes), F32),
                        pltpu.SemaphoreType.DMA((2,)), pltpu.SemaphoreType.DMA((2,))])
    return pl.pallas_call(
        _moe_kernel,
        grid_spec=grid_spec,
        out_shape=jax.ShapeDtypeStruct((n_slots * (d // lanes), lanes), F32),
        compiler_params=_params("arbitrary", "arbitrary"),
        name="moe_experts",
    )(sb_e, sb_row, sb_n, n_sb, xs_lines, w_gu, w_gu, w_dn, b_gu.reshape(ne, 1, two_de),
      b_gu.reshape(ne, 1, two_de), b_dn.reshape(ne, 1, d))


def _combine_kernel(line_ref, x1_ref, gate_ref, gt2_ref, fg_ref, y_ref, o_ref, buf, sem):
    tm, d = x1_ref.shape
    tr = d // LANES

    def issue(t, _):
        dst = pl.ds(pl.multiple_of(t * tr, tr), tr)
        for k in range(TOP_K):
            src = pl.multiple_of(line_ref[t * TOP_K + k], tr)
            pltpu.make_async_copy(y_ref.at[pl.ds(src, tr)], buf.at[k, dst],
                                  sem).start(priority=k % 2)
        return 0

    lax.fori_loop(0, tm, issue, 0)
    for k in range(TOP_K):
        pltpu.make_async_copy(y_ref.at[pl.ds(0, tm * tr)], buf.at[k], sem).wait()
    gate = gate_ref[...]
    cols = []
    for s in range(tr):
        acc = gate[:, 0:1] * buf[0, pl.ds(s, tm, stride=tr), :]
        for k in range(1, TOP_K):
            acc = acc + gate[:, k:k + 1] * buf[k, pl.ds(s, tm, stride=tr), :]
        cols.append(acc)
    x2 = x1_ref[...] + gt2_ref[0] * jnp.concatenate(cols, axis=-1)
    ms = jnp.mean(x2 * x2, axis=-1, keepdims=True)
    o_ref[...] = x2 * lax.rsqrt(ms + NORM_EPS) * fg_ref[...]


def _combine(slot_flat, x1, gate4, gt2, fg, y_lines, tiles_per_batch):
    n_tok, d = x1.shape
    lanes = y_lines.shape[1]
    tm = ROW_TILE
    smem_tok = pl.BlockSpec((tm * TOP_K,), lambda i: (i,), memory_space=pltpu.SMEM)
    return pl.pallas_call(
        _combine_kernel,
        grid=(n_tok // tm,),
        in_specs=[smem_tok,
                  pl.BlockSpec((tm, d), lambda i: (i, 0)),
                  pl.BlockSpec((tm, TOP_K), lambda i: (i, 0)),
                  pl.BlockSpec((1, 1, d), lambda i: (i // tiles_per_batch, 0, 0)),
                  pl.BlockSpec((1, d), lambda i: (0, 0)),
                  pl.BlockSpec(memory_space=pl.ANY)],
        out_specs=pl.BlockSpec((tm, d), lambda i: (i, 0)),
        out_shape=jax.ShapeDtypeStruct((n_tok, d), F32),
        scratch_shapes=[pltpu.VMEM((TOP_K, tm * (d // lanes), lanes), F32),
                        pltpu.SemaphoreType.DMA(())],
        compiler_params=_params("arbitrary"),
        name="moe_combine_final",
    )(slot_flat * (d // lanes), x1, gate4, gt2, fg.reshape(1, d), y_lines)


def _block_diag2(w):
    z = jnp.zeros_like(w[0])
    return jnp.concatenate([jnp.concatenate([w[0], z], 1), jnp.concatenate([z, w[1]], 1)], 0)


def kernel(x, c, ctx, c_ctx, mod_w, mod_b, norm1_g, w_in, s5_a_re, s5_a_im, s5_log_dt, s5_b_re,
           s5_b_im, s5_c_re, s5_c_im, s5_d, s5_glu_w, s5_glu_b, rw_mu, rw_w0, rw_w2, rw_a0, rw_a2,
           rw_g2, rw_k_k, rw_k_a, rw_r_k, rw_ln_w, rw_ln_b, w_out, norm2_g, router_w, router_b,
           exp_w_gu, exp_b_gu, exp_w_dn, exp_b_dn, final_g):
    assert mod_w.shape[0] == 1, "single-layer stack only"
    bsz, n_lat, d = x.shape
    n_ctx = ctx.shape[1]
    assert bsz == 2 and n_ctx % ROW_TILE == 0 and n_lat % ROW_TILE == 0
    s5w = s5_d.shape[-1]
    rww = rw_k_k.shape[-1]
    n_heads = rww // RW_HEAD
    ne = router_w.shape[-1]

    cond8 = jnp.zeros((8, d), F32).at[:bsz].set(c).at[bsz].set(c_ctx)
    mod = _adaln(cond8, mod_w[0], mod_b[0])
    sh1, sc1, gt1, sh2, sc2, gt2 = [m[:bsz, None, :] for m in jnp.split(mod, 6, axis=-1)]
    csh1, csc1 = [jnp.broadcast_to(m[bsz][None, None, :], (bsz, 1, d))
                  for m in jnp.split(mod, 6, axis=-1)[:2]]

    wu = w_in[0][:, :s5w].astype(BF16)
    wz = w_in[0][:, s5w:].astype(BF16)
    u_lat, z_lat = _inproj(x, norm1_g[0], sc1, sh1, wu, wz)
    u_ctx, z_ctx = _inproj(ctx, norm1_g[0], csc1, csh1, wu, wz)

    s5c = _s5_constants(s5_a_re[0], s5_a_im[0], s5_log_dt[0], s5_b_re[0], s5_b_im[0], s5_c_re[0],
                        s5_c_im[0], s5_d[0], s5_glu_w[0], s5_glu_b[0])
    y5 = _s5(jnp.concatenate([u_ctx, u_lat], axis=1), s5c, n_ctx)

    lanes_idx = jnp.arange(rww) // RW_HEAD
    hsel = (lanes_idx[:, None] == jnp.arange(LANES)[None, :]).astype(BF16)
    hselt = hsel.T
    feat_consts = (rw_mu[0], rw_w0[0].reshape(1, N_DIR * rww),
                   _block_diag2(rw_w2[0]).astype(BF16), rw_a0[0].reshape(1, N_DIR * rww),
                   _block_diag2(rw_a2[0]).astype(BF16), rw_g2[0].astype(BF16),
                   rw_k_k[0].reshape(1, rww), rw_k_a[0].reshape(1, rww),
                   rw_r_k[0].reshape(1, rww), hsel, hselt)
    fc = _rw_features(z_ctx, n_ctx, False, *feat_consts)
    fl = _rw_features(z_lat, GRID_W, True, *feat_consts)

    def scan_inputs(f):
        r, v, kh, _, _, lw, q, kt = f
        return r, v, kh, lw, q, kt

    gc, hc, rhc, y0c = _rw_chunks(*scan_inputs(fc))
    gl, hl, rhl, y0l = _rw_chunks(*scan_inputs(fl))
    s_zero = jnp.zeros((N_DIR, bsz, RW_HEAD, rww), F32)
    (s_ctx,) = _rw_state(s_zero, gc, hc, rhc, y0c, emit_y=False)
    y_f, y_b, _ = _rw_state(s_ctx, gl, hl, rhl, y0l, emit_y=True)

    x1, f, idx4, gate4, rank4, counts = _outproj(
        x, y5, y_f, y_b, fl[4], fl[3], rw_ln_w[0], rw_ln_b[0], hsel, hselt, w_out[0].astype(BF16),
        gt1, norm2_g[0], sc2, sh2, router_w[0], router_b[0])

    n_tok = bsz * n_lat
    cnt = counts[0]
    padded = (cnt + MOE_ROWS - 1) // MOE_ROWS * MOE_ROWS
    pend = jnp.cumsum(padded)
    start = (pend - padded).astype(jnp.int32)
    nb = n_tok * TOP_K // MOE_ROWS + ne
    n_slots = nb * MOE_ROWS
    blocks_per_group = MOE_GROUP_ROWS // MOE_ROWS
    n_groups_max = n_slots // MOE_GROUP_ROWS + ne
    groups_e = (padded + MOE_GROUP_ROWS - 1) // MOE_GROUP_ROWS
    groups_end = jnp.cumsum(groups_e)
    gidx = jnp.arange(n_groups_max)
    sb_e = jnp.minimum(jnp.sum(groups_end[None, :] <= gidx[:, None], axis=1), ne - 1)
    local = gidx - (groups_end - groups_e)[sb_e]
    sb_row = start[sb_e] // MOE_ROWS + local * blocks_per_group
    sb_n = jnp.clip(padded[sb_e] // MOE_ROWS - local * blocks_per_group, 0, blocks_per_group)
    n_sb = groups_end[-1].astype(jnp.int32).reshape(1)
    first = jnp.sum(jnp.where(idx4[..., None] == jnp.arange(ne), start, 0), axis=-1)
    slot_flat = (first + rank4).reshape(-1).astype(jnp.int32)

    blk_end = (jnp.arange(nb) + 1) * MOE_ROWS
    fill = jnp.logical_or(jnp.any(blk_end[:, None] == pend[None, :], axis=1),
                          blk_end > pend[-1]).astype(jnp.int32)
    xs = _dispatch(slot_flat, fill, f.reshape(-1, LANES), n_tok, n_slots)
    y3 = _moe_experts(sb_e.astype(jnp.int32), sb_row.astype(jnp.int32), sb_n.astype(jnp.int32),
                      n_sb, xs, exp_w_gu[0], exp_b_gu[0], exp_w_dn[0], exp_b_dn[0])
    out = _combine(slot_flat, x1.reshape(n_tok, d), gate4.reshape(n_tok, TOP_K), gt2, final_g, y3,
                   n_lat // ROW_TILE)
    return out.reshape(bsz, n_lat, d)
```

```python
import functools
import math

import jax
import jax.numpy as jnp
from jax import lax
from jax.experimental import pallas as pl
from jax.experimental.pallas import tpu as pltpu

F32 = jnp.float32
BF16 = jnp.bfloat16
HIGHEST = lax.Precision.HIGHEST

LANES = 128
VMEM_LIMIT_BYTES = 56 * 1024 * 1024

NORM_EPS = 1e-5
N_DIR = 2
S5_GROUP = 16
S5_STATE = 64
S5_CHUNK = 8
S5_LANE_GROUPS = LANES // S5_GROUP
S5_SCAN_ROWS = 8
RW_HEAD = 64
RW_CHUNK = 64
RW_CHUNKS_PER_STEP = 2
RW_GN_EPS = 64e-5
GRID_W = 64
TOP_K = 4
SWIGLU_ALPHA = 1.702
SWIGLU_LIMIT = 7.0
ROW_TILE = 256
MOE_ROWS = 256
MOE_GROUP_ROWS = 1536
MOE_HIDDEN_TILE = 256


def _params(*sem):
    return pltpu.CompilerParams(dimension_semantics=sem, vmem_limit_bytes=VMEM_LIMIT_BYTES)


def _mm(a, b, dims=((1,), (0,))):
    return lax.dot_general(a.astype(BF16), b.astype(BF16), (dims, ((), ())),
                           preferred_element_type=F32)


def _split_bf16(a):
    hi = a.astype(BF16)
    return hi, (a - hi.astype(F32)).astype(BF16)


def _mm_sel(a, sel):
    hi, lo = _split_bf16(a)
    m = a.shape[0]
    out = jnp.dot(jnp.concatenate([hi, lo], axis=0), sel.astype(BF16), preferred_element_type=F32)
    return out[:m] + out[m:]


def _sel_mm(sel, a):
    hi, lo = _split_bf16(a)
    n = a.shape[1]
    out = jnp.dot(sel.astype(BF16), jnp.concatenate([hi, lo], axis=1), preferred_element_type=F32)
    return out[:, :n] + out[:, n:]


def _mm_f32(a, b, dims=((1,), (0,))):
    return lax.dot_general(a, b, (dims, ((), ())), precision=HIGHEST,
                           preferred_element_type=F32)


def _adaln_kernel(cond_ref, w_ref, b_ref, o_ref):
    c = cond_ref[...]
    o_ref[...] = _mm_f32(c * jax.nn.sigmoid(c), w_ref[...]) + b_ref[...]


def _adaln(cond8, w, b):
    d, n = w.shape
    tn = 1536
    return pl.pallas_call(
        _adaln_kernel,
        grid=(n // tn,),
        in_specs=[pl.BlockSpec((8, d), lambda j: (0, 0)),
                  pl.BlockSpec((d, tn), lambda j: (0, j)),
                  pl.BlockSpec((1, tn), lambda j: (0, j))],
        out_specs=pl.BlockSpec((8, tn), lambda j: (0, j)),
        out_shape=jax.ShapeDtypeStruct((8, n), F32),
        compiler_params=_params("parallel"),
        name="adaln",
    )(cond8, w, b.reshape(1, n))


def _rms_mod(x, g, sc, sh):
    y = x * lax.rsqrt(jnp.mean(x * x, axis=-1, keepdims=True) + NORM_EPS)
    return (y * g) * (1.0 + sc) + sh


def _inproj_kernel(x_ref, g_ref, sc_ref, sh_ref, wu_ref, wz_ref, u_ref, z_ref):
    h = _rms_mod(x_ref[0], g_ref[...], sc_ref[0], sh_ref[0]).astype(BF16)
    u_ref[0] = jnp.dot(h, wu_ref[...], preferred_element_type=F32)
    z_ref[0] = jnp.dot(h, wz_ref[...], preferred_element_type=F32)


def _inproj(x, g, sc, sh, wu, wz):
    bsz, n, d = x.shape
    nu, nz = wu.shape[1], wz.shape[1]
    tm = ROW_TILE
    const = dict(pipeline_mode=pl.Buffered(1))
    return pl.pallas_call(
        _inproj_kernel,
        grid=(bsz, n // tm),
        in_specs=[pl.BlockSpec((1, tm, d), lambda b, i: (b, i, 0)),
                  pl.BlockSpec((1, d), lambda b, i: (0, 0)),
                  pl.BlockSpec((1, 1, d), lambda b, i: (b, 0, 0)),
                  pl.BlockSpec((1, 1, d), lambda b, i: (b, 0, 0)),
                  pl.BlockSpec((d, nu), lambda b, i: (0, 0), **const),
                  pl.BlockSpec((d, nz), lambda b, i: (0, 0), **const)],
        out_specs=[pl.BlockSpec((1, tm, nu), lambda b, i: (b, i, 0)),
                   pl.BlockSpec((1, tm, nz), lambda b, i: (b, i, 0))],
        out_shape=[jax.ShapeDtypeStruct((bsz, n, nu), F32),
                   jax.ShapeDtypeStruct((bsz, n, nz), F32)],
        compiler_params=_params("parallel", "parallel"),
        name="inproj",
    )(x, g.reshape(1, d), sc, sh, wu, wz)


def _s5_constants(a_re, a_im, log_dt, b_re, b_im, c_re, c_im, d_skip, glu_w, glu_b):
    t = S5_CHUNK
    g, p = a_re.shape[1], a_re.shape[2]
    h = S5_GROUP
    gl = S5_LANE_GROUPS
    no = g // gl
    n = jnp.arange(t + 1, dtype=F32)[None, None, :, None]
    ar, ai = a_re.astype(F32)[:, :, None, :], a_im.astype(F32)[:, :, None, :]
    dt = jnp.exp(log_dt.astype(F32))[:, :, None, None]
    mag = jnp.exp(n * (ar * dt))
    pr, pi = mag * jnp.cos(n * (ai * dt)), mag * jnp.sin(n * (ai * dt))
    lr, li = pr[:, :, 1], pi[:, :, 1]
    ar, ai = ar[:, :, 0], ai[:, :, 0]
    den = ar * ar + ai * ai
    cf_re = (((lr - 1.0) * ar + li * ai) / den)[:, :, None, :]
    cf_im = ((li * ar - (lr - 1.0) * ai) / den)[:, :, None, :]
    br = jnp.swapaxes(b_re.astype(F32), 2, 3)
    bi = jnp.swapaxes(b_im.astype(F32), 2, 3)
    bb_re = cf_re * br - cf_im * bi
    bb_im = cf_re * bi + cf_im * br
    cr = jnp.swapaxes(c_re.astype(F32), 2, 3)[:, :, :, None, :]
    ci = jnp.swapaxes(c_im.astype(F32), 2, 3)[:, :, :, None, :]
    prt = jnp.swapaxes(pr, 2, 3)[..., None]
    pit = jnp.swapaxes(pi, 2, 3)[..., None]
    cl_re = cr * prt - ci * pit
    cl_im = cr * pit + ci * prt
    dims = (((2,), (1,)), ((0,), (0,)))
    flat = lambda a: a.reshape((N_DIR * g,) + a.shape[2:])
    lag = (lax.dot_general(flat(bb_re), flat(cl_re[:, :, :, :t]).reshape(N_DIR * g, p, t * h),
                           dims, precision=HIGHEST)
           - lax.dot_general(flat(bb_im), flat(cl_im[:, :, :, :t]).reshape(N_DIR * g, p, t * h),
                             dims, precision=HIGHEST)).reshape(N_DIR, g, h, t, h)
    s_idx = jnp.arange(t)[:, None]
    t_idx = jnp.arange(t)[None, :]
    lag_n = jnp.arange(t)[:, None, None]
    shift = jnp.stack([(t_idx - s_idx)[None] == lag_n, (s_idx - t_idx)[None] == lag_n]).astype(F32)
    kg = jnp.einsum('dnst,dginj->gsitj', shift, lag, precision=HIGHEST)
    k_rows = jnp.transpose(kg.reshape(no, gl, t, h, t * h), (0, 2, 1, 3, 4)).reshape(
        no, t * gl * h, t * h)
    ps_re = jnp.stack([pr[0, :, :t][:, ::-1], pr[1, :, :t]])[:, :, :, None, :]
    ps_im = jnp.stack([pi[0, :, :t][:, ::-1], pi[1, :, :t]])[:, :, :, None, :]
    f_re = ps_re * bb_re[:, :, None] - ps_im * bb_im[:, :, None]
    f_im = ps_re * bb_im[:, :, None] + ps_im * bb_re[:, :, None]
    fg = jnp.transpose(jnp.stack([f_re, f_im]), (2, 3, 4, 0, 1, 5))
    f_rows = jnp.transpose(fg.reshape(no, gl, t, h, 4 * p), (0, 2, 1, 3, 4)).reshape(
        no, t * gl * h, 4 * p)
    pick = lambda c: jnp.stack([c[0, :, :, 1:], c[1, :, :, 1:][:, :, ::-1]])
    eg = jnp.stack([pick(cl_re), -pick(cl_im)])
    eg = jnp.transpose(eg.reshape(2 * N_DIR, no, gl, p, t * h), (1, 0, 2, 3, 4))
    e_rows = eg.reshape(no, 4 * gl * p, t * h)
    lam_rows = [pr[0, :, t], pi[0, :, t], pr[1, :, t], pi[1, :, t]]
    fmat, emat, kmat = _s5_expand(f_rows, e_rows, k_rows)
    lam = jnp.stack(lam_rows, axis=1)
    lam = jnp.transpose(lam.reshape(no, gl, 4, p), (0, 2, 1, 3)).reshape(no, 4, gl * p)
    gmat = jnp.einsum('ab,oahk->oahbk', jnp.eye(gl, dtype=F32),
                      glu_w.astype(F32).reshape(no, gl, h, h)).reshape(no, gl * h, gl * h)
    dvec = d_skip.astype(F32).reshape(no, 1, gl * h)
    bvec = glu_b.astype(F32).reshape(no, 1, gl * h)
    return fmat, emat, kmat, lam, dvec, bvec, gmat.astype(BF16)


def _s5_expand_kernel(f_ref, e_ref, k_ref, rep_s_ref, rep_t_ref, fo_ref, eo_ref, ko_ref):
    def expand(x, rep, row_div, col_div):
        y = jnp.dot(x.astype(BF16), rep, preferred_element_type=F32)
        row = lax.broadcasted_iota(jnp.int32, y.shape, 0)
        col = lax.broadcasted_iota(jnp.int32, y.shape, 1)
        same = (row // row_div) % S5_LANE_GROUPS == (col // col_div) % S5_LANE_GROUPS
        return jnp.where(same, y, 0.0).astype(BF16)

    fo_ref[0] = expand(f_ref[0], rep_s_ref[...], S5_GROUP, S5_STATE)
    eo_ref[0] = expand(e_ref[0], rep_t_ref[...], S5_STATE, S5_GROUP)
    ko_ref[0] = expand(k_ref[0], rep_t_ref[...], S5_GROUP, S5_GROUP)


def _s5_expand(f_rows, e_rows, k_rows):
    no, rows_f, cols_f = f_rows.shape
    _, rows_e, cols_e = e_rows.shape
    gl = S5_LANE_GROUPS
    c = jnp.arange(cols_f * gl)
    rep_s = (jnp.arange(cols_f)[:, None]
             == (c // (gl * S5_STATE)) * S5_STATE + c % S5_STATE).astype(BF16)
    c = jnp.arange(cols_e * gl)
    rep_t = (jnp.arange(cols_e)[:, None]
             == (c // (gl * S5_GROUP)) * S5_GROUP + c % S5_GROUP).astype(BF16)
    blk = lambda r, cc: pl.BlockSpec((1, r, cc), lambda o: (o, 0, 0))
    full = lambda a: pl.BlockSpec(a.shape, lambda o: (0, 0))
    return pl.pallas_call(
        _s5_expand_kernel,
        grid=(no,),
        in_specs=[blk(rows_f, cols_f), blk(rows_e, cols_e), blk(rows_f, cols_e),
                  full(rep_s), full(rep_t)],
        out_specs=[blk(rows_f, cols_f * gl), blk(rows_e, cols_e * gl), blk(rows_f, cols_e * gl)],
        out_shape=[jax.ShapeDtypeStruct((no, rows_f, cols_f * gl), BF16),
                   jax.ShapeDtypeStruct((no, rows_e, cols_e * gl), BF16),
                   jax.ShapeDtypeStruct((no, rows_f, cols_e * gl), BF16)],
        compiler_params=_params("parallel"),
        name="s5_expand",
    )(f_rows, e_rows, k_rows, rep_s, rep_t)


def _s5_chunk_rows(u_ref, b, start, n_chunks):
    t = S5_CHUNK
    return jnp.concatenate(
        [u_ref[b, pl.ds(start + s, n_chunks, stride=t), :] for s in range(t)], axis=-1)


def _s5_in_kernel(u_ref, f_ref, z_ref):
    bsz, n_tok, _ = u_ref.shape
    for b in range(bsz):
        z_ref[b] = _mm(_s5_chunk_rows(u_ref, b, 0, n_tok // S5_CHUNK), f_ref[0])


def _s5_scan_kernel(n_ctx_chunks, z_ref, lam_ref, hs_ref):
    n_chunks = z_ref.shape[1]
    q = z_ref.shape[2] // 4
    rows = S5_SCAN_ROWS
    n_blocks, ctx_blocks = n_chunks // rows, n_ctx_chunks // rows
    lam = lam_ref[0]
    lf_re, lf_im, lb_re, lb_im = lam[0:1], lam[1:2], lam[2:3], lam[3:4]

    def block(k, carry):
        f_re, f_im, b_re, b_im = carry
        kb = jnp.where(k < ctx_blocks, ctx_blocks - 1 - k, n_blocks - 1 + ctx_blocks - k)
        rf = pl.multiple_of(k * rows, rows)
        rb = pl.multiple_of(kb * rows, rows)
        zf = z_ref[0, pl.ds(rf, rows), :]
        zb = z_ref[0, pl.ds(rb, rows), :]
        ent = [[], [], [], []]
        for j in range(rows):
            jb = rows - 1 - j
            for lst, val in zip(ent, (f_re, f_im, b_re, b_im)):
                lst.append(val)
            f_re, f_im = (lf_re * f_re - lf_im * f_im + zf[j:j + 1, 0:q],
                          lf_re * f_im + lf_im * f_re + zf[j:j + 1, 2 * q:3 * q])
            b_re, b_im = (lb_re * b_re - lb_im * b_im + zb[jb:jb + 1, q:2 * q],
                          lb_re * b_im + lb_im * b_re + zb[jb:jb + 1, 3 * q:4 * q])
        hs_ref[0, pl.ds(rf, rows), 0:q] = jnp.concatenate(ent[0], axis=0)
        hs_ref[0, pl.ds(rf, rows), 2 * q:3 * q] = jnp.concatenate(ent[1], axis=0)
        hs_ref[0, pl.ds(rb, rows), q:2 * q] = jnp.concatenate(ent[2][::-1], axis=0)
        hs_ref[0, pl.ds(rb, rows), 3 * q:4 * q] = jnp.concatenate(ent[3][::-1], axis=0)
        return f_re, f_im, b_re, b_im

    zero = jnp.zeros((1, q), F32)
    lax.fori_loop(0, n_blocks, block, (zero, zero, zero, zero))


def _s5_out_kernel(n_ctx, u_ref, hs_ref, e_ref, k_ref, g_ref, d_ref, b_ref, o_ref, y_scr):
    t = S5_CHUNK
    bsz, n_tok, lanes = u_ref.shape
    n_chunks = (n_tok - n_ctx) // t
    for b in range(bsz):
        x = _s5_chunk_rows(u_ref, b, n_ctx, n_chunks)
        y = _mm(x, k_ref[0]) + _mm(hs_ref[b, n_ctx // t:, :], e_ref[0])
        for s in range(t):
            y_scr[pl.ds(s, n_chunks, stride=t), :] = y[:, s * lanes:(s + 1) * lanes]
        y = jax.nn.gelu(y_scr[...] + d_ref[0] * u_ref[b, n_ctx:, :])
        gate = _mm(y, g_ref[0]) + b_ref[0]
        o_ref[b] = y * jax.nn.sigmoid(gate)


def _s5(u_all, consts, n_ctx):
    fmat, emat, kmat, lam, dvec, bvec, gmat = consts
    bsz, n_tok, width = u_all.shape
    t = S5_CHUNK
    n_chunks = n_tok // t
    no, rows_k, cols_f = fmat.shape
    assert n_ctx % (t * S5_SCAN_ROWS) == 0 and n_tok % (t * S5_SCAN_ROWS) == 0
    tok = pl.BlockSpec((1, n_tok, LANES), lambda o, b: (b, 0, o))
    state = pl.BlockSpec((1, n_chunks, cols_f), lambda o, b: (b, 0, o))
    full = lambda a: pl.BlockSpec((1,) + a.shape[1:], lambda o, b: (o, 0, 0))
    z = pl.pallas_call(
        _s5_in_kernel,
        grid=(no, bsz),
        in_specs=[tok, full(fmat)],
        out_specs=state,
        out_shape=jax.ShapeDtypeStruct((bsz, n_chunks, no * cols_f), F32),
        compiler_params=_params("parallel", "parallel"),
        name="s5_in",
    )(u_all, fmat)
    hs = pl.pallas_call(
        functools.partial(_s5_scan_kernel, n_ctx // t),
        grid=(no, bsz),
        in_specs=[state, full(lam)],
        out_specs=state,
        out_shape=jax.ShapeDtypeStruct((bsz, n_chunks, no * cols_f), F32),
        compiler_params=_params("parallel", "parallel"),
        name="s5_scan",
    )(z, lam)
    n_lat = n_tok - n_ctx
    return pl.pallas_call(
        functools.partial(_s5_out_kernel, n_ctx),
        grid=(no, bsz),
        in_specs=[tok, state, full(emat), full(kmat), full(gmat), full(dvec), full(bvec)],
        out_specs=pl.BlockSpec((1, n_lat, LANES), lambda o, b: (b, 0, o)),
        out_shape=jax.ShapeDtypeStruct((bsz, n_lat, width), F32),
        scratch_shapes=[pltpu.VMEM((n_lat, LANES), F32)],
        compiler_params=_params("parallel", "parallel"),
        name="s5_out",
    )(u_all, hs, emat, kmat, gmat, dvec, bvec)


def _rw_feat_kernel(width, has_vert, n_tiles, z_ref, zp_ref, zn_ref, mu_ref, w0_ref, w2_ref,
                    a0_ref, a2_ref, g2_ref, kk_ref, ka_ref, rk_ref, hsel_ref, hselt_ref,
                    r_ref, v_ref, kh_ref, g_ref, bv_ref, lw_ref, q_ref, kt_ref):
    z = z_ref[0]
    t0, cz = z.shape
    rw = r_ref.shape[-1]
    mu = mu_ref[...]
    tok = lax.broadcasted_iota(jnp.int32, (t0, 1), 0)
    col = tok % width
    left = pltpu.roll(z, 1, 0)
    right = pltpu.roll(z, t0 - 1, 0)
    out = z + jnp.where(col != 0, mu[0:1] * (left - z), 0.0)
    out = out + jnp.where(col != width - 1, mu[1:2] * (right - z), 0.0)
    if has_vert:
        i = pl.program_id(1)
        up = jnp.concatenate([zp_ref[0], z[:t0 - width]], axis=0)
        down = jnp.concatenate([z[width:], zn_ref[0]], axis=0)
        up_ok = jnp.logical_or(i > 0, tok >= width)
        down_ok = jnp.logical_or(i < n_tiles - 1, tok < t0 - width)
        out = out + jnp.where(up_ok, mu[2:3] * (up - z), 0.0)
        out = out + jnp.where(down_ok, mu[3:4] * (down - z), 0.0)
    r = out[:, 0:rw]
    k = out[:, rw:2 * rw]
    v = out[:, 2 * rw:3 * rw]
    o = 3 * rw
    lora = w2_ref.shape[0]
    xw = out[:, o:o + lora]
    xa = out[:, o + lora:o + 2 * lora]
    xg = out[:, o + 2 * lora:]
    dec = w0_ref[...] + _mm(jnp.tanh(xw), w2_ref[...])
    lw = -math.exp(-0.5) * jax.nn.sigmoid(dec)
    a = jax.nn.sigmoid(a0_ref[...] + _mm(xa, a2_ref[...]))
    g = _mm(jax.nn.sigmoid(xg), g2_ref[...])
    kk = k * kk_ref[...]
    ssq = _mm_sel(kk * kk, hsel_ref[...])
    inv = 1.0 / jnp.maximum(jnp.sqrt(ssq), 1e-12)
    kh = kk * _mm_sel(inv, hselt_ref[...])
    ka = ka_ref[...]
    kt_sum = jnp.zeros_like(k)
    for d in range(N_DIR):
        a_d = a[:, d * rw:(d + 1) * rw]
        kt_d = k * (1.0 + (a_d - 1.0) * ka)
        kt_sum = kt_sum + kt_d
        lw_ref[d, 0] = lw[:, d * rw:(d + 1) * rw]
        q_ref[d, 0] = a_d * kh
        kt_ref[d, 0] = kt_d
    bonus = _mm_sel(_mm_sel(r * kt_sum * rk_ref[...], hsel_ref[...]), hselt_ref[...])
    r_ref[0] = r
    v_ref[0] = v
    kh_ref[0] = kh
    g_ref[0] = g
    bv_ref[0] = bonus * v


def _rw_features(z, width, has_vert, mu, w0, w2blk, a0, a2blk, g2, k_k, k_a, r_k, hsel, hselt):
    bsz, n, cz = z.shape
    rw = k_k.shape[-1]
    t0 = ROW_TILE
    n_tiles = n // t0
    per = t0 // GRID_W
    nblk = n // GRID_W
    full = lambda a: pl.BlockSpec(a.shape, lambda b, i: (0,) * a.ndim)
    tok = pl.BlockSpec((1, t0, rw), lambda b, i: (b, i, 0))
    dtok = pl.BlockSpec((N_DIR, 1, t0, rw), lambda b, i: (0, b, i, 0))
    consts = (mu, w0, w2blk, a0, a2blk, g2, k_k, k_a, r_k, hsel, hselt)
    return pl.pallas_call(
        functools.partial(_rw_feat_kernel, width, has_vert, n_tiles),
        grid=(bsz, n_tiles),
        in_specs=[pl.BlockSpec((1, t0, cz), lambda b, i: (b, i, 0)),
                  pl.BlockSpec((1, GRID_W, cz), lambda b, i: (b, jnp.maximum(i * per - 1, 0), 0)),
                  pl.BlockSpec((1, GRID_W, cz),
                               lambda b, i: (b, jnp.minimum(i * per + per, nblk - 1), 0))]
        + [full(a) for a in consts],
        out_specs=[tok] * 5 + [dtok] * 3,
        out_shape=[jax.ShapeDtypeStruct((bsz, n, rw), F32)] * 5
        + [jax.ShapeDtypeStruct((N_DIR, bsz, n, rw), F32)] * 3,
        compiler_params=_params("parallel", "parallel"),
        name="rw_features",
    )(z, z, z, *consts)


def _rw_chunk_kernel(r_ref, v_ref, kh_ref, lw_ref, q_ref, kt_ref, g_ref, h_ref, rh_ref, y0_ref):
    rev = pl.program_id(0) == 1
    n = RW_CHUNK
    hd = RW_HEAD
    n_sub = lw_ref.shape[2] // n
    row = lax.broadcasted_iota(jnp.int32, (n, n), 0)
    col = lax.broadcasted_iota(jnp.int32, (n, n), 1)
    ahead = (row - col) * jnp.where(rev, -1, 1)
    incl = (ahead >= 0).astype(F32)
    strict = (ahead > 0).astype(F32)
    eye = (row == col).astype(F32)
    same_block = [(jnp.right_shift(row, s) == jnp.right_shift(col, s)).astype(F32)
                  for s in range(3, n.bit_length())]
    n_heads = lw_ref.shape[-1] // hd
    pt, rt, v, qt, ktt, qh, kth, gam, where = [], [], [], [], [], [], [], [], []
    for c in range(n_sub):
        tok = slice(c * n, (c + 1) * n)
        lw = lw_ref[0, 0, tok, :]
        b_incl = _sel_mm(incl, lw)
        btot = jnp.sum(lw, axis=0, keepdims=True)
        e_neg = jnp.exp(-b_incl)
        e_rem = jnp.exp(btot - b_incl)
        q_c, kt_c = q_ref[0, 0, tok, :], kt_ref[0, 0, tok, :]
        pt_all = kh_ref[0, tok, :] * jnp.exp(b_incl - lw)
        rt_all = r_ref[0, tok, :] * jnp.exp(b_incl)
        qt_all, ktt_all = q_c * e_neg, kt_c * e_neg
        qh_all, kth_all = q_c * e_rem, kt_c * e_rem
        gam_all = jnp.exp(btot)
        v_all = v_ref[0, tok, :]
        for h in range(n_heads):
            sl = slice(h * hd, (h + 1) * hd)
            for lst, arr in zip((pt, rt, v, qt, ktt, qh, kth, gam),
                                (pt_all, rt_all, v_all, qt_all, ktt_all, qh_all, kth_all, gam_all)):
                lst.append(arr[:, sl])
            where.append((c, tok, sl))
    heads = range(len(where))
    a4 = [_mm(jnp.concatenate([pt[h], rt[h]], 0), jnp.concatenate([qt[h], ktt[h]], 0),
              ((1,), (1,))) for h in heads]
    nmat = [strict * a4[h][:n, :n] for h in heads]
    akv = [_mm(strict * a4[h][:n, n:], v[h]) for h in heads]
    nd = [same_block[0] * nmat[h] for h in heads]
    x = [_mm(nd[h], nd[h]) for h in heads]
    m = [eye - nd[h] for h in heads]
    m = [m[h] + _mm(m[h], x[h]) for h in heads]
    x = [_mm(x[h], x[h]) for h in heads]
    m = [m[h] + _mm(m[h], x[h]) for h in heads]
    for lvl in range(1, len(same_block)):
        ring = same_block[lvl] - same_block[lvl - 1]
        t = [_mm(m[h], ring * nmat[h]) for h in heads]
        m = [m[h] - _mm(t[h], m[h]) for h in heads]
    wu = [_mm(m[h], jnp.concatenate([pt[h], akv[h]], 1)) for h in heads]
    gh = [_mm(wu[h], qh[h], ((0,), (0,))) for h in heads]
    vk = [_mm(v[h], kth[h], ((0,), (0,))) for h in heads]
    lwu = [_mm(incl * a4[h][n:, :n], wu[h]) for h in heads]
    lv = [_mm(incl * a4[h][n:, n:], v[h]) for h in heads]
    for h in heads:
        c, tok, sl = where[h]
        g_ref[0, 0, c, :, sl] = eye * gam[h] - gh[h][:hd]
        h_ref[0, 0, c, :, sl] = vk[h] - gh[h][hd:]
        rh_ref[0, 0, tok, sl] = rt[h] - lwu[h][:, :hd]
        y0_ref[0, 0, tok, sl] = lv[h] - lwu[h][:, hd:]


def _rw_chunks(r, v, kh, lw, q, kt):
    bsz, n, rw = r.shape
    nc = n // RW_CHUNK
    per = RW_CHUNKS_PER_STEP
    tok = pl.BlockSpec((1, per * RW_CHUNK, rw), lambda d, b, c: (b, c, 0))
    dtok = pl.BlockSpec((1, 1, per * RW_CHUNK, rw), lambda d, b, c: (d, b, c, 0))
    mat = pl.BlockSpec((1, 1, per, RW_HEAD, rw), lambda d, b, c: (d, b, c, 0, 0))
    return pl.pallas_call(
        _rw_chunk_kernel,
        grid=(N_DIR, bsz, nc // per),
        in_specs=[tok, tok, tok, dtok, dtok, dtok],
        out_specs=[mat, mat, dtok, dtok],
        out_shape=[jax.ShapeDtypeStruct((N_DIR, bsz, nc, RW_HEAD, rw), F32)] * 2
        + [jax.ShapeDtypeStruct((N_DIR, bsz, n, rw), F32)] * 2,
        compiler_params=_params("parallel", "parallel", "parallel"),
        name="rw_chunks",
    )(r, v, kh, lw, q, kt)


def _rw_state_kernel(emit_y, s0_ref, gf_ref, gb_ref, hf_ref, hb_ref, rhf_ref, rhb_ref, y0f_ref,
                     y0b_ref, *rest):
    if emit_y:
        yf_ref, yb_ref, sfin_ref, s_scr = rest
    else:
        sfin_ref, s_scr = rest
    c = pl.program_id(0)
    hd = RW_HEAD
    n_dir, bsz = s_scr.shape[0], s_scr.shape[1]

    @pl.when(c == 0)
    def _():
        s_scr[...] = s0_ref[...]

    sls = [slice(h * hd, (h + 1) * hd) for h in range(s_scr.shape[-1] // hd)]
    g_refs, h_refs = (gf_ref, gb_ref), (hf_ref, hb_ref)
    rh_refs, y0_refs = (rhf_ref, rhb_ref), (y0f_ref, y0b_ref)
    chains = [(d, b) for d in range(n_dir) for b in range(bsz)]
    s_bf = {k: s_scr[k[0], k[1]].astype(BF16) for k in chains}
    g_bf = {k: g_refs[k[0]][0, k[1], 0].astype(BF16) for k in chains}
    s_new = {k: [_mm(s_bf[k][:, sl], g_bf[k][:, sl]) for sl in sls] for k in chains}
    if emit_y:
        y_refs = (yf_ref, yb_ref)
        rh_bf = {k: rh_refs[k[0]][0, k[1]].astype(BF16) for k in chains}
        y = {k: [_mm(rh_bf[k][:, sl], s_bf[k][:, sl], ((1,), (1,))) for sl in sls] for k in chains}
        for d, b in chains:
            y_refs[d][b] = jnp.concatenate(y[d, b], axis=-1) + y0_refs[d][0, b]
    for d, b in chains:
        s_scr[d, b] = jnp.concatenate(s_new[d, b], axis=-1) + h_refs[d][0, b, 0]

    @pl.when(c == pl.num_programs(0) - 1)
    def _():
        sfin_ref[...] = s_scr[...]


def _rw_state(s0, gmat, hmat, rh, y0, emit_y):
    n_dir, bsz, nc, hd, rw = gmat.shape
    n = rh.shape[2]
    chunk = lambda d, c: c + d * (nc - 1 - 2 * c)
    mat = lambda d: pl.BlockSpec((1, bsz, 1, hd, rw), lambda c: (d, 0, chunk(d, c), 0, 0))
    dtok = lambda d: pl.BlockSpec((1, bsz, RW_CHUNK, rw), lambda c: (d, 0, chunk(d, c), 0))
    st = pl.BlockSpec((n_dir, bsz, hd, rw), lambda c: (0, 0, 0, 0))
    out_specs = [st]
    out_shape = [jax.ShapeDtypeStruct((n_dir, bsz, hd, rw), F32)]
    if emit_y:
        out_specs = [pl.BlockSpec((bsz, RW_CHUNK, rw), lambda c, d=d: (0, chunk(d, c), 0))
                     for d in range(n_dir)] + out_specs
        out_shape = [jax.ShapeDtypeStruct((bsz, n, rw), F32)] * n_dir + out_shape
    return pl.pallas_call(
        functools.partial(_rw_state_kernel, emit_y),
        grid=(nc,),
        in_specs=[st, mat(0), mat(1), mat(0), mat(1), dtok(0), dtok(1), dtok(0), dtok(1)],
        out_specs=out_specs,
        out_shape=out_shape,
        scratch_shapes=[pltpu.VMEM((n_dir, bsz, hd, rw), F32)],
        compiler_params=_params("arbitrary"),
        name="rw_state_y" if emit_y else "rw_state",
    )(s0, gmat, gmat, hmat, hmat, rh, rh, y0, y0)


def _outproj_kernel(x_ref, y5_ref, yf_ref, yb_ref, bv_ref, g_ref, lnw_ref, lnb_ref, hsel_ref,
                    hselt_ref, wo_ref, gt1_ref, g2_ref, sc2_ref, sh2_ref, rw_ref, rb_ref,
                    x1_ref, f_ref, idx_ref, gate_ref, rank_ref, cnt_ref, carry):
    first = jnp.logical_and(pl.program_id(0) == 0, pl.program_id(1) == 0)

    @pl.when(first)
    def _():
        carry[...] = jnp.zeros_like(carry)

    inv_hd = 1.0 / RW_HEAD
    y = yf_ref[0] + yb_ref[0]
    mean = _mm_sel(_mm_sel(y, hsel_ref[...]), hselt_ref[...]) * inv_hd
    yc = y - mean
    var = _mm_sel(_mm_sel(yc * yc, hsel_ref[...]), hselt_ref[...]) * inv_hd
    yn = yc * lax.rsqrt(var + RW_GN_EPS) * lnw_ref[...] + lnb_ref[...]
    yr = (yn + bv_ref[0]) * g_ref[0]
    mix = _mm(jnp.concatenate([y5_ref[0], yr], axis=-1), wo_ref[...])
    x1 = x_ref[0] + gt1_ref[0] * mix
    x1_ref[0] = x1
    f = _rms_mod(x1, g2_ref[...], sc2_ref[0], sh2_ref[0])
    n_lines = f.shape[1] // LANES
    for s in range(n_lines):
        f_ref[0, pl.ds(s, f.shape[0], stride=n_lines), :] = f[:, s * LANES:(s + 1) * LANES]

    f_hi, f_lo = _split_bf16(f)
    w_hi, w_lo = _split_bf16(rw_ref[...])
    tm = f.shape[0]
    both = jnp.dot(jnp.concatenate([f_hi, f_lo], axis=0), w_hi, preferred_element_type=F32)
    logits = (both[:tm] + both[tm:] + jnp.dot(f_hi, w_lo, preferred_element_type=F32)
              + rb_ref[...])
    tm, ne = logits.shape
    eid = lax.broadcasted_iota(jnp.int32, (tm, ne), 1)
    work = logits
    sel = jnp.zeros((tm, ne), F32)
    idx_cols, val_cols = [], []
    for _ in range(TOP_K):
        top = jnp.max(work, axis=-1, keepdims=True)
        pick = jnp.min(jnp.where(work == top, eid, ne), axis=-1, keepdims=True)
        hit = eid == pick
        sel = jnp.where(hit, 1.0, sel)
        work = jnp.where(hit, -jnp.inf, work)
        idx_cols.append(pick)
        val_cols.append(top)
    exps = [jnp.exp(vk - val_cols[0]) for vk in val_cols]
    denom = exps[0] + exps[1] + exps[2] + exps[3]
    row = lax.broadcasted_iota(jnp.int32, (tm, tm), 0)
    colm = lax.broadcasted_iota(jnp.int32, (tm, tm), 1)
    before = _mm((colm < row).astype(F32), sel) + carry[0:1, :]
    rank_cols = [jnp.sum(jnp.where(eid == ic, before, 0.0), axis=-1, keepdims=True)
                 for ic in idx_cols]
    idx_ref[0] = jnp.concatenate(idx_cols, axis=-1)
    gate_ref[0] = jnp.concatenate([e / denom for e in exps], axis=-1)
    rank_ref[0] = jnp.concatenate(rank_cols, axis=-1).astype(jnp.int32)
    total = carry[0:1, :] + jnp.sum(sel, axis=0, keepdims=True)
    carry[...] = jnp.broadcast_to(total, carry.shape)
    cnt_ref[...] = jnp.broadcast_to(total, cnt_ref.shape).astype(jnp.int32)


def _outproj(x, y5, yf, yb, bv, g, ln_w, ln_b, hsel, hselt, wo, gt1, g2, sc2, sh2, router_w,
             router_b):
    bsz, n, d = x.shape
    rw = y5.shape[-1]
    ne = router_w.shape[-1]
    tm = ROW_TILE
    row = lambda w: pl.BlockSpec((1, tm, w), lambda b, i: (b, i, 0))
    vec = lambda w: pl.BlockSpec((1, w), lambda b, i: (0, 0))
    bvec = pl.BlockSpec((1, 1, d), lambda b, i: (b, 0, 0))
    full = lambda a: pl.BlockSpec(a.shape, lambda b, i: (0,) * a.ndim)
    return pl.pallas_call(
        _outproj_kernel,
        grid=(bsz, n // tm),
        in_specs=[row(d), row(rw), row(rw), row(rw),
                  row(rw), row(rw), vec(rw), vec(rw), full(hsel), full(hselt),
                  pl.BlockSpec(wo.shape, lambda b, i: (0, 0), pipeline_mode=pl.Buffered(1)),
                  bvec, vec(d), bvec, bvec, full(router_w), vec(ne)],
        out_specs=[row(d), pl.BlockSpec((1, tm * (d // LANES), LANES), lambda b, i: (b, i, 0)),
                   row(TOP_K), row(TOP_K), row(TOP_K),
                   pl.BlockSpec((8, ne), lambda b, i: (0, 0))],
        out_shape=[jax.ShapeDtypeStruct((bsz, n, d), F32),
                   jax.ShapeDtypeStruct((bsz, n * (d // LANES), LANES), F32),
                   jax.ShapeDtypeStruct((bsz, n, TOP_K), jnp.int32),
                   jax.ShapeDtypeStruct((bsz, n, TOP_K), F32),
                   jax.ShapeDtypeStruct((bsz, n, TOP_K), jnp.int32),
                   jax.ShapeDtypeStruct((8, ne), jnp.int32)],
        scratch_shapes=[pltpu.VMEM((8, ne), F32)],
        compiler_params=_params("arbitrary", "arbitrary"),
        name="outproj_router",
    )(x, y5, yf, yb, bv, g, ln_w.reshape(1, rw), ln_b.reshape(1, rw), hsel, hselt, wo, gt1,
      g2.reshape(1, d), sc2, sh2, router_w, router_b.reshape(1, ne))


def _dispatch_kernel(tile_rows, line_ref, fill_ref, f_ref, xs_ref, zeros, sem, sem_fill):
    tr = tile_rows
    tm = f_ref.shape[0] // tr
    blk = zeros.shape[0]

    @pl.when(pl.program_id(0) == 0)
    def _():
        zeros[...] = jnp.zeros_like(zeros)

        def fill(b):
            dst = xs_ref.at[pl.ds(pl.multiple_of(b * blk, blk), blk)]
            return pltpu.make_async_copy(zeros, dst, sem_fill)

        def start(b, carry):
            @pl.when(fill_ref[b] == 1)
            def _():
                fill(b).start()
            return carry

        def wait(b, carry):
            @pl.when(fill_ref[b] == 1)
            def _():
                fill(b).wait()
            return carry

        lax.fori_loop(0, fill_ref.shape[0], start, 0)
        lax.fori_loop(0, fill_ref.shape[0], wait, 0)

    def issue(t, _):
        src = f_ref.at[pl.ds(pl.multiple_of(t * tr, tr), tr)]
        for k in range(TOP_K):
            dst = pl.multiple_of(line_ref[t * TOP_K + k], tr)
            pltpu.make_async_copy(src, xs_ref.at[pl.ds(dst, tr)], sem).start(priority=k % 2)
        return 0

    lax.fori_loop(0, tm, issue, 0)
    for _ in range(TOP_K):
        pltpu.make_async_copy(f_ref, xs_ref.at[pl.ds(0, tm * tr)], sem).wait()


def _dispatch(slot_flat, fill, f_lines, n_tok, n_slots):
    lanes = f_lines.shape[1]
    tr = f_lines.shape[0] // n_tok
    tm = ROW_TILE
    smem_tok = pl.BlockSpec((tm * TOP_K,), lambda i: (i,), memory_space=pltpu.SMEM)
    smem_all = lambda a: pl.BlockSpec(a.shape, lambda i: (0,), memory_space=pltpu.SMEM)
    return pl.pallas_call(
        functools.partial(_dispatch_kernel, tr),
        grid=(n_tok // tm,),
        in_specs=[smem_tok, smem_all(fill), pl.BlockSpec((tm * tr, lanes), lambda i: (i, 0))],
        out_specs=pl.BlockSpec(memory_space=pl.ANY),
        out_shape=jax.ShapeDtypeStruct((n_slots * tr, lanes), f_lines.dtype),
        scratch_shapes=[pltpu.VMEM((MOE_ROWS * tr, lanes), f_lines.dtype),
                        pltpu.SemaphoreType.DMA(()), pltpu.SemaphoreType.DMA(())],
        compiler_params=_params("arbitrary"),
        name="moe_dispatch",
    )(slot_flat * tr, fill, f_lines)


def _moe_kernel(sbe_ref, sbrow_ref, sbn_ref, nsb_ref, xs_ref, wg_ref, wl_ref, wd_ref, bg_ref,
                bl_ref, bd_ref, y_ref, x_stage, x_bf, acc, y_stage, sem_in, sem_out):
    sb, j = pl.program_id(0), pl.program_id(1)
    nj = pl.num_programs(1)
    rb = MOE_ROWS
    n_tiles = y_stage.shape[1] // rb
    valid = sb < nsb_ref[0]

    def copy_in(row, slot):
        src = xs_ref.at[pl.ds(pl.multiple_of(row * n_tiles, rb * n_tiles), rb * n_tiles)]
        return pltpu.make_async_copy(src, x_stage.at[slot], sem_in.at[slot])

    def copy_out(row, slot):
        dst = y_ref.at[pl.ds(pl.multiple_of(row * n_tiles, rb * n_tiles), rb * n_tiles)]
        return pltpu.make_async_copy(y_stage.at[slot], dst, sem_out.at[slot])

    @pl.when(valid)
    def _():
        n_blk = sbn_ref[sb]
        row0 = sbrow_ref[sb] * rb

        @pl.when(j == 0)
        def _():
            copy_in(row0, 0).start()

            def load(k, carry):
                slot = k % 2

                @pl.when(k + 1 < n_blk)
                def _():
                    copy_in(row0 + (k + 1) * rb, 1 - slot).start()

                copy_in(row0 + k * rb, slot).wait()
                r = pl.multiple_of(k * rb, rb)
                for s in range(n_tiles):
                    x_bf[pl.ds(r, rb), s * LANES:(s + 1) * LANES] = (
                        x_stage[slot, pl.ds(s, rb, stride=n_tiles), :].astype(BF16))
                return carry

            lax.fori_loop(0, n_blk, load, 0)

            def clear(k, carry):
                acc[pl.ds(pl.multiple_of(k * rb, rb), rb), :] = jnp.broadcast_to(
                    bd_ref[0], (rb, acc.shape[1]))
                return carry

            lax.fori_loop(0, n_blk, clear, 0)

        def accumulate(r):
            x = x_bf[r, :]
            glu = jnp.dot(x, wg_ref[0].astype(BF16), preferred_element_type=F32) + bg_ref[0]
            lin = jnp.dot(x, wl_ref[0].astype(BF16), preferred_element_type=F32) + bl_ref[0]
            glu = jnp.minimum(glu, SWIGLU_LIMIT)
            lin = jnp.clip(lin, -SWIGLU_LIMIT, SWIGLU_LIMIT)
            act = ((lin + 1.0) * glu * jax.nn.sigmoid(SWIGLU_ALPHA * glu)).astype(BF16)
            acc[r, :] += jnp.dot(act, wd_ref[0].astype(BF16), preferred_element_type=F32)

        def row_pair(kk, carry):
            accumulate(pl.ds(pl.multiple_of(kk * 2 * rb, 2 * rb), 2 * rb))
            return carry

        lax.fori_loop(0, n_blk // 2, row_pair, 0)

        @pl.when(n_blk % 2 == 1)
        def _():
            accumulate(pl.ds(pl.multiple_of((n_blk - 1) * rb, rb), rb))

        @pl.when(j == nj - 1)
        def _():
            def store(k, carry):
                slot = k % 2

                @pl.when(k >= 2)
                def _():
                    copy_out(row0, slot).wait()

                r = pl.multiple_of(k * rb, rb)
                for s in range(n_tiles):
                    y_stage[slot, pl.ds(s, rb, stride=n_tiles), :] = (
                        acc[pl.ds(r, rb), s * LANES:(s + 1) * LANES])
                copy_out(row0 + k * rb, slot).start()
                return carry

            lax.fori_loop(0, n_blk, store, 0)

            @pl.when(n_blk >= 2)
            def _():
                copy_out(row0, n_blk % 2).wait()

            copy_out(row0, (n_blk - 1) % 2).wait()

    @pl.when(jnp.logical_not(valid))
    def _():
        q = (sb - nsb_ref[0]) * nj + j
        last = jnp.maximum(nsb_ref[0] - 1, 0)
        first_unused = sbrow_ref[last] + sbn_ref[last]
        n_unused = y_ref.shape[0] // (rb * n_tiles) - first_unused

        @pl.when(q < n_unused)
        def _():
            y_stage[0] = jnp.zeros(y_stage.shape[1:], y_stage.dtype)
            cp = copy_out((first_unused + q) * rb, 0)
            cp.start()
            cp.wait()


def _moe_experts(sb_e, sb_row, sb_n, n_sb, xs_lines, w_gu, b_gu, w_dn, b_dn):
    ne, d, two_de = w_gu.shape
    lanes = xs_lines.shape[1]
    n_slots = xs_lines.shape[0] // (d // lanes)
    de = two_de // 2
    th = MOE_HIDDEN_TILE
    nj = de // th
    n_groups = sb_e.shape[0]
    grp = lambda sb, nsb: jnp.maximum(jnp.minimum(sb, nsb[0] - 1), 0)
    tile = lambda sb, j, nsb: jnp.where(sb < nsb[0], j, nj - 1)
    grid_spec = pltpu.PrefetchScalarGridSpec(
        num_scalar_prefetch=4,
        grid=(n_groups, nj),
        in_specs=[pl.BlockSpec(memory_space=pl.ANY),
                  pl.BlockSpec((1, d, th), lambda sb, j, e, r, n, nsb:
                               (e[grp(sb, nsb)], 0, tile(sb, j, nsb))),
                  pl.BlockSpec((1, d, th), lambda sb, j, e, r, n, nsb:
                               (e[grp(sb, nsb)], 0, nj + tile(sb, j, nsb))),
                  pl.BlockSpec((1, th, d), lambda sb, j, e, r, n, nsb:
                               (e[grp(sb, nsb)], tile(sb, j, nsb), 0)),
                  pl.BlockSpec((1, 1, th), lambda sb, j, e, r, n, nsb:
                               (e[grp(sb, nsb)], 0, tile(sb, j, nsb))),
                  pl.BlockSpec((1, 1, th), lambda sb, j, e, r, n, nsb:
                               (e[grp(sb, nsb)], 0, nj + tile(sb, j, nsb))),
                  pl.BlockSpec((1, 1, d), lambda sb, j, e, r, n, nsb: (e[grp(sb, nsb)], 0, 0))],
        out_specs=pl.BlockSpec(memory_space=pl.ANY),
        scratch_shapes=[pltpu.VMEM((2, MOE_ROWS * (d // lanes), lanes), F32),
                        pltpu.VMEM((MOE_GROUP_ROWS, d), BF16),
                        pltpu.VMEM((MOE_GROUP_ROWS, d), F32),
                        pltpu.VMEM((2, MOE_ROWS * (d // lanes), lanes), F32),
                        pltpu.SemaphoreType.DMA((2,)), pltpu.SemaphoreType.DMA((2,))])
    return pl.pallas_call(
        _moe_kernel,
        grid_spec=grid_spec,
        out_shape=jax.ShapeDtypeStruct((n_slots * (d // lanes), lanes), F32),
        compiler_params=_params("arbitrary", "arbitrary"),
        name="moe_experts",
    )(sb_e, sb_row, sb_n, n_sb, xs_lines, w_gu, w_gu, w_dn, b_gu.reshape(ne, 1, two_de),
      b_gu.reshape(ne, 1, two_de), b_dn.reshape(ne, 1, d))


def _combine_kernel(line_ref, x1_ref, gate_ref, gt2_ref, fg_ref, y_ref, o_ref, buf, sem):
    tm, d = x1_ref.shape
    tr = d // LANES

    def issue(t, _):
        dst = pl.ds(pl.multiple_of(t * tr, tr), tr)
        for k in range(TOP_K):
            src = pl.multiple_of(line_ref[t * TOP_K + k], tr)
            pltpu.make_async_copy(y_ref.at[pl.ds(src, tr)], buf.at[k, dst],
                                  sem).start(priority=k % 2)
        return 0

    lax.fori_loop(0, tm, issue, 0)
    for k in range(TOP_K):
        pltpu.make_async_copy(y_ref.at[pl.ds(0, tm * tr)], buf.at[k], sem).wait()
    gate = gate_ref[...]
    cols = []
    for s in range(tr):
        acc = gate[:, 0:1] * buf[0, pl.ds(s, tm, stride=tr), :]
        for k in range(1, TOP_K):
            acc = acc + gate[:, k:k + 1] * buf[k, pl.ds(s, tm, stride=tr), :]
        cols.append(acc)
    x2 = x1_ref[...] + gt2_ref[0] * jnp.concatenate(cols, axis=-1)
    ms = jnp.mean(x2 * x2, axis=-1, keepdims=True)
    o_ref[...] = x2 * lax.rsqrt(ms + NORM_EPS) * fg_ref[...]


def _combine(slot_flat, x1, gate4, gt2, fg, y_lines, tiles_per_batch):
    n_tok, d = x1.shape
    lanes = y_lines.shape[1]
    tm = ROW_TILE
    smem_tok = pl.BlockSpec((tm * TOP_K,), lambda i: (i,), memory_space=pltpu.SMEM)
    return pl.pallas_call(
        _combine_kernel,
        grid=(n_tok // tm,),
        in_specs=[smem_tok,
                  pl.BlockSpec((tm, d), lambda i: (i, 0)),
                  pl.BlockSpec((tm, TOP_K), lambda i: (i, 0)),
                  pl.BlockSpec((1, 1, d), lambda i: (i // tiles_per_batch, 0, 0)),
                  pl.BlockSpec((1, d), lambda i: (0, 0)),
                  pl.BlockSpec(memory_space=pl.ANY)],
        out_specs=pl.BlockSpec((tm, d), lambda i: (i, 0)),
        out_shape=jax.ShapeDtypeStruct((n_tok, d), F32),
        scratch_shapes=[pltpu.VMEM((TOP_K, tm * (d // lanes), lanes), F32),
                        pltpu.SemaphoreType.DMA(())],
        compiler_params=_params("arbitrary"),
        name="moe_combine_final",
    )(slot_flat * (d // lanes), x1, gate4, gt2, fg.reshape(1, d), y_lines)


def _block_diag2(w):
    z = jnp.zeros_like(w[0])
    return jnp.concatenate([jnp.concatenate([w[0], z], 1), jnp.concatenate([z, w[1]], 1)], 0)


def kernel(x, c, ctx, c_ctx, mod_w, mod_b, norm1_g, w_in, s5_a_re, s5_a_im, s5_log_dt, s5_b_re,
           s5_b_im, s5_c_re, s5_c_im, s5_d, s5_glu_w, s5_glu_b, rw_mu, rw_w0, rw_w2, rw_a0, rw_a2,
           rw_g2, rw_k_k, rw_k_a, rw_r_k, rw_ln_w, rw_ln_b, w_out, norm2_g, router_w, router_b,
           exp_w_gu, exp_b_gu, exp_w_dn, exp_b_dn, final_g):
    assert mod_w.shape[0] == 1, "single-layer stack only"
    bsz, n_lat, d = x.shape
    n_ctx = ctx.shape[1]
    assert bsz == 2 and n_ctx % ROW_TILE == 0 and n_lat % ROW_TILE == 0
    s5w = s5_d.shape[-1]
    rww = rw_k_k.shape[-1]
    ne = router_w.shape[-1]

    cond8 = jnp.zeros((8, d), F32).at[:bsz].set(c).at[bsz].set(c_ctx)
    mod = _adaln(cond8, mod_w[0], mod_b[0])
    sh1, sc1, gt1, sh2, sc2, gt2 = [m[:bsz, None, :] for m in jnp.split(mod, 6, axis=-1)]
    csh1, csc1 = [jnp.broadcast_to(m[bsz][None, None, :], (bsz, 1, d))
                  for m in jnp.split(mod, 6, axis=-1)[:2]]

    wu = w_in[0][:, :s5w].astype(BF16)
    wz = w_in[0][:, s5w:].astype(BF16)
    u_lat, z_lat = _inproj(x, norm1_g[0], sc1, sh1, wu, wz)
    u_ctx, z_ctx = _inproj(ctx, norm1_g[0], csc1, csh1, wu, wz)

    s5c = _s5_constants(s5_a_re[0], s5_a_im[0], s5_log_dt[0], s5_b_re[0], s5_b_im[0], s5_c_re[0],
                        s5_c_im[0], s5_d[0], s5_glu_w[0], s5_glu_b[0])
    y5 = _s5(jnp.concatenate([u_ctx, u_lat], axis=1), s5c, n_ctx)

    lanes_idx = jnp.arange(rww) // RW_HEAD
    hsel = (lanes_idx[:, None] == jnp.arange(LANES)[None, :]).astype(BF16)
    hselt = hsel.T
    feat_consts = (rw_mu[0], rw_w0[0].reshape(1, N_DIR * rww),
                   _block_diag2(rw_w2[0]).astype(BF16), rw_a0[0].reshape(1, N_DIR * rww),
                   _block_diag2(rw_a2[0]).astype(BF16), rw_g2[0].astype(BF16),
                   rw_k_k[0].reshape(1, rww), rw_k_a[0].reshape(1, rww),
                   rw_r_k[0].reshape(1, rww), hsel, hselt)
    fc = _rw_features(z_ctx, n_ctx, False, *feat_consts)
    fl = _rw_features(z_lat, GRID_W, True, *feat_consts)

    def scan_inputs(f):
        r, v, kh, _, _, lw, q, kt = f
        return r, v, kh, lw, q, kt

    gc, hc, rhc, y0c = _rw_chunks(*scan_inputs(fc))
    gl, hl, rhl, y0l = _rw_chunks(*scan_inputs(fl))
    s_zero = jnp.zeros((N_DIR, bsz, RW_HEAD, rww), F32)
    (s_ctx,) = _rw_state(s_zero, gc, hc, rhc, y0c, emit_y=False)
    y_f, y_b, _ = _rw_state(s_ctx, gl, hl, rhl, y0l, emit_y=True)

    x1, f, idx4, gate4, rank4, counts = _outproj(
        x, y5, y_f, y_b, fl[4], fl[3], rw_ln_w[0], rw_ln_b[0], hsel, hselt, w_out[0].astype(BF16),
        gt1, norm2_g[0], sc2, sh2, router_w[0], router_b[0])

    n_tok = bsz * n_lat
    cnt = counts[0]
    padded = (cnt + MOE_ROWS - 1) // MOE_ROWS * MOE_ROWS
    pend = jnp.cumsum(padded)
    start = (pend - padded).astype(jnp.int32)
    nb = n_tok * TOP_K // MOE_ROWS + ne
    n_slots = nb * MOE_ROWS
    blocks_per_group = MOE_GROUP_ROWS // MOE_ROWS
    n_groups_max = -(-(n_slots + ne * (MOE_GROUP_ROWS - MOE_ROWS)) // MOE_GROUP_ROWS)
    groups_e = (padded + MOE_GROUP_ROWS - 1) // MOE_GROUP_ROWS
    groups_end = jnp.cumsum(groups_e)
    gidx = jnp.arange(n_groups_max)
    sb_e = jnp.minimum(jnp.sum(groups_end[None, :] <= gidx[:, None], axis=1), ne - 1)
    local = gidx - (groups_end - groups_e)[sb_e]
    sb_row = start[sb_e] // MOE_ROWS + local * blocks_per_group
    sb_n = jnp.clip(padded[sb_e] // MOE_ROWS - local * blocks_per_group, 0, blocks_per_group)
    n_sb = groups_end[-1].astype(jnp.int32).reshape(1)
    first = jnp.sum(jnp.where(idx4[..., None] == jnp.arange(ne), start, 0), axis=-1)
    slot_flat = (first + rank4).reshape(-1).astype(jnp.int32)

    blk_end = (jnp.arange(nb) + 1) * MOE_ROWS
    fill = jnp.logical_or(jnp.any(blk_end[:, None] == pend[None, :], axis=1),
                          blk_end > pend[-1]).astype(jnp.int32)
    xs = _dispatch(slot_flat, fill, f.reshape(-1, LANES), n_tok, n_slots)
    y3 = _moe_experts(sb_e.astype(jnp.int32), sb_row.astype(jnp.int32), sb_n.astype(jnp.int32),
                      n_sb, xs, exp_w_gu[0], exp_b_gu[0], exp_w_dn[0], exp_b_dn[0])
    out = _combine(slot_flat, x1.reshape(n_tok, d), gate4.reshape(n_tok, TOP_K), gt2, final_g, y3,
                   n_lat // ROW_TILE)
    return out.reshape(bsz, n_lat, d)
```
